```python
import jax, jax.numpy as jnp
from jax import lax
import numpy as np

D_MODEL = 1024
BATCH = 2
SEQ = 8192
DEPTH = 2

GRID_W = 64
CTX_LEN = 256
N_MIXERS = 2
BLK = 128
WINDOW = 128
ROPE_BASE = 10000.0
EPS = 1e-6
NEG_INF = -1e30

A_HEADS = 16
A_KV_HEADS = 4
A_GROUP = A_HEADS // A_KV_HEADS
A_HEAD_DIM = 64
A_WIDTH = A_HEADS * A_HEAD_DIM
A_KV_WIDTH = A_KV_HEADS * A_HEAD_DIM
A_IN = A_WIDTH + 2 * A_KV_WIDTH + A_WIDTH

B_HEADS = 16
B_NOPE = 64
B_ROPE = 32
B_V = 64
B_Q_RANK = 256
B_KV_RANK = 128
B_WIDTH = B_HEADS * B_V
B_IN = B_Q_RANK + B_KV_RANK + B_ROPE + B_WIDTH

kernel_name = "hybrid_swa_sink_mla_dit_prefix"


def rmsnorm(x, g):
    xf = x.astype(jnp.float32)
    y = xf * lax.rsqrt(jnp.mean(xf * xf, axis=-1, keepdims=True) + EPS)
    return (y * g.astype(jnp.float32)).astype(x.dtype)


def axial_rope_tables(n_tokens, rot_dim, dtype):
    n_rows = n_tokens // GRID_W
    row = jnp.broadcast_to(jnp.arange(n_rows)[:, None], (n_rows, GRID_W)).reshape(-1)
    col = jnp.broadcast_to(jnp.arange(GRID_W)[None, :], (n_rows, GRID_W)).reshape(-1)
    nf = rot_dim // 4
    inv = ROPE_BASE ** (-jnp.arange(nf, dtype=jnp.float32) / nf)
    ar = row.astype(jnp.float32)[:, None] * inv
    ac = col.astype(jnp.float32)[:, None] * inv
    ang = jnp.concatenate([ar, ar, ac, ac], axis=-1)
    return jnp.cos(ang).astype(dtype), jnp.sin(ang).astype(dtype)


def apply_rope(x, cos, sin):
    a1, a2, b1, b2 = jnp.split(x, 4, axis=-1)
    rot = jnp.concatenate([-a2, a1, -b2, b1], axis=-1)
    shp = (x.shape[1],) + (1,) * (x.ndim - 3) + (x.shape[-1],)
    return x * cos.reshape(shp) + rot * sin.reshape(shp)


def modulate(x, g, shift, scale):
    return rmsnorm(x, g) * (1 + scale) + shift


def mixer_a(h, hc, w_in, sinks, w_o, cos, sin, ctx_out):
    bsz, S, _ = h.shape
    dt = h.dtype
    q, k, v, g = jnp.split(h @ w_in, [A_WIDTH, A_WIDTH + A_KV_WIDTH, A_WIDTH + 2 * A_KV_WIDTH], axis=-1)
    q = apply_rope(q.reshape(bsz, S, A_KV_HEADS, A_GROUP, A_HEAD_DIM), cos, sin)
    k = apply_rope(k.reshape(bsz, S, A_KV_HEADS, A_HEAD_DIM), cos, sin)
    v = v.reshape(bsz, S, A_KV_HEADS, A_HEAD_DIM)
    kvc = hc @ w_in[:, A_WIDTH:A_WIDTH + 2 * A_KV_WIDTH]
    kc, vc = jnp.split(kvc.reshape(bsz, -1, 2 * A_KV_HEADS, A_HEAD_DIM), 2, axis=2)
    sink_logit = sinks.astype(jnp.float32).reshape(A_KV_HEADS, A_GROUP)
    scale = A_HEAD_DIM ** -0.5
    kp = jnp.pad(k, ((0, 0), (BLK, BLK), (0, 0), (0, 0)))
    vp = jnp.pad(v, ((0, 0), (BLK, BLK), (0, 0), (0, 0)))

    def block(i):
        qs = lax.dynamic_slice_in_dim(q, i * BLK, BLK, axis=1)
        ks = lax.dynamic_slice_in_dim(kp, i * BLK, 3 * BLK, axis=1)
        vs = lax.dynamic_slice_in_dim(vp, i * BLK, 3 * BLK, axis=1)
        q_pos = i * BLK + jnp.arange(BLK)
        k_pos = (i - 1) * BLK + jnp.arange(3 * BLK)
        valid = (jnp.abs(q_pos[:, None] - k_pos[None, :]) <= WINDOW) & (k_pos >= 0)[None, :] & (k_pos < S)[None, :]
        s_loc = jnp.einsum('bqhgd,bkhd->bhgqk', qs, ks).astype(jnp.float32) * scale
        s_loc = jnp.where(valid, s_loc, NEG_INF)
        s_ctx = jnp.einsum('bqhgd,bkhd->bhgqk', qs, kc).astype(jnp.float32) * scale
        sink = jnp.broadcast_to(sink_logit[None, :, :, None, None], s_loc.shape[:-1] + (1,))
        p = jax.nn.softmax(jnp.concatenate([s_loc, s_ctx, sink], axis=-1), axis=-1).astype(dt)
        n_loc = 3 * BLK
        return (jnp.einsum('bhgqk,bkhd->bqhgd', p[..., :n_loc], vs)
                + jnp.einsum('bhgqk,bkhd->bqhgd', p[..., n_loc:-1], vc))

    o = lax.map(block, jnp.arange(S // BLK))
    o = jnp.moveaxis(o, 0, 1).reshape(bsz, S, A_WIDTH)
    y = (o * jax.nn.silu(g)) @ w_o
    if not ctx_out:
        return y, None
    C = hc.shape[1]
    qc = (hc @ w_in[:, :A_WIDTH]).reshape(bsz, C, A_KV_HEADS, A_GROUP, A_HEAD_DIM)
    gc = hc @ w_in[:, A_WIDTH + 2 * A_KV_WIDTH:]
    s = jnp.einsum('bqhgd,bkhd->bhgqk', qc, kc).astype(jnp.float32) * scale
    sink = jnp.broadcast_to(sink_logit[None, :, :, None, None], s.shape[:-1] + (1,))
    p = jax.nn.softmax(jnp.concatenate([s, sink], axis=-1), axis=-1)[..., :-1].astype(dt)
    oc = jnp.einsum('bhgqk,bkhd->bqhgd', p, vc).reshape(bsz, C, A_WIDTH)
    yc = (oc * jax.nn.silu(gc)) @ w_o
    return y, yc


def mixer_b(h, hc, w_in, q_norm_g, w_uq, kv_norm_g, w_ukv, w_o, cos, sin, ctx_out):
    bsz, S, _ = h.shape
    C = hc.shape[1]
    dt = h.dtype
    splits = [B_Q_RANK, B_Q_RANK + B_KV_RANK, B_Q_RANK + B_KV_RANK + B_ROPE]
    cq, ckv, kr, g = jnp.split(h @ w_in, splits, axis=-1)
    q = (rmsnorm(cq, q_norm_g) @ w_uq).reshape(bsz, S, B_HEADS, B_NOPE + B_ROPE)
    qn, qr = q[..., :B_NOPE], apply_rope(q[..., B_NOPE:], cos, sin)
    kv = (rmsnorm(ckv, kv_norm_g) @ w_ukv).reshape(bsz, S, B_HEADS, B_NOPE + B_V)
    kn, v = kv[..., :B_NOPE], kv[..., B_NOPE:]
    kr = apply_rope(kr, cos, sin)
    ckv_c, kr_c = jnp.split(hc @ w_in[:, B_Q_RANK:splits[2]], [B_KV_RANK], axis=-1)
    kv_c = (rmsnorm(ckv_c, kv_norm_g) @ w_ukv).reshape(bsz, C, B_HEADS, B_NOPE + B_V)
    kn_c, v_c = kv_c[..., :B_NOPE], kv_c[..., B_NOPE:]
    kn_all = jnp.concatenate([kn_c, kn], axis=1)
    kr_all = jnp.concatenate([kr_c, kr], axis=1)
    v_all = jnp.concatenate([v_c, v], axis=1)
    scale = (B_NOPE + B_ROPE) ** -0.5

    def block(i):
        qn_b = lax.dynamic_slice_in_dim(qn, i * BLK, BLK, axis=1)
        qr_b = lax.dynamic_slice_in_dim(qr, i * BLK, BLK, axis=1)
        s = (jnp.einsum('bqhd,bkhd->bhqk', qn_b, kn_all)
             + jnp.einsum('bqhr,bkr->bhqk', qr_b, kr_all)).astype(jnp.float32) * scale
        p = jax.nn.softmax(s, axis=-1).astype(dt)
        return jnp.einsum('bhqk,bkhd->bqhd', p, v_all)

    o = lax.map(block, jnp.arange(S // BLK))
    o = jnp.moveaxis(o, 0, 1).reshape(bsz, S, B_WIDTH)
    y = (o * jax.nn.silu(g)) @ w_o
    if not ctx_out:
        return y, None
    qc = (rmsnorm(hc @ w_in[:, :B_Q_RANK], q_norm_g) @ w_uq).reshape(bsz, C, B_HEADS, B_NOPE + B_ROPE)
    gc = hc @ w_in[:, splits[2]:]
    s = (jnp.einsum('bqhd,bkhd->bhqk', qc[..., :B_NOPE], kn_c)
         + jnp.einsum('bqhr,bkr->bhqk', qc[..., B_NOPE:], kr_c)).astype(jnp.float32) * scale
    p = jax.nn.softmax(s, axis=-1).astype(dt)
    oc = jnp.einsum('bhqk,bkhd->bqhd', p, v_c).reshape(bsz, C, B_WIDTH)
    yc = (oc * jax.nn.silu(gc)) @ w_o
    return y, yc


def setup_inputs(seed: int = 0) -> dict:
    key = jax.random.key(seed)
    ks = jax.random.split(key, 24)
    f = jnp.float32
    D = D_MODEL

    def w(k, shape, fan_in):
        return jax.random.normal(k, shape, f) * fan_in ** -0.5

    def gain(k, n):
        return 1.0 + 0.02 * jax.random.normal(k, (n,), f)

    return {
        "x": jax.random.normal(ks[0], (BATCH, SEQ, D), f),
        "c": jax.random.normal(ks[1], (BATCH, D), f),
        "ctx": jax.random.normal(ks[2], (BATCH, CTX_LEN, D), f),
        "c_ctx": jax.random.normal(ks[3], (D,), f),
        "norm_g_0": gain(ks[4], D),
        "ada_w_0": w(ks[5], (D, 3 * D), D),
        "ada_b_0": 0.02 * jax.random.normal(ks[6], (3 * D,), f),
        "a_w_in_0": w(ks[7], (D, A_IN), D),
        "a_sinks_0": 0.5 * jax.random.normal(ks[8], (A_HEADS,), f),
        "a_w_o_0": w(ks[9], (A_WIDTH, D), A_WIDTH),
        "norm_g_1": gain(ks[10], D),
        "ada_w_1": w(ks[11], (D, 3 * D), D),
        "ada_b_1": 0.02 * jax.random.normal(ks[12], (3 * D,), f),
        "b_w_in_1": w(ks[13], (D, B_IN), D),
        "b_q_norm_1": gain(ks[14], B_Q_RANK),
        "b_w_uq_1": w(ks[15], (B_Q_RANK, B_HEADS * (B_NOPE + B_ROPE)), B_Q_RANK),
        "b_kv_norm_1": gain(ks[16], B_KV_RANK),
        "b_w_ukv_1": w(ks[17], (B_KV_RANK, B_HEADS * (B_NOPE + B_V)), B_KV_RANK),
        "b_w_o_1": w(ks[18], (B_WIDTH, D), B_WIDTH),
        "final_g": gain(ks[19], D),
    }


def reference(x, c, ctx, c_ctx, norm_g_0, ada_w_0, ada_b_0, a_w_in_0, a_sinks_0, a_w_o_0,
              norm_g_1, ada_w_1, ada_b_1, b_w_in_1, b_q_norm_1, b_w_uq_1, b_kv_norm_1,
              b_w_ukv_1, b_w_o_1, final_g):
    S = x.shape[1]
    cos_a, sin_a = axial_rope_tables(S, A_HEAD_DIM, x.dtype)
    cos_b, sin_b = axial_rope_tables(S, B_ROPE, x.dtype)
    layers = [
        (norm_g_0, ada_w_0, ada_b_0, (a_w_in_0, a_sinks_0, a_w_o_0)),
        (norm_g_1, ada_w_1, ada_b_1, (b_w_in_1, b_q_norm_1, b_w_uq_1, b_kv_norm_1, b_w_ukv_1, b_w_o_1)),
    ]
    for i in range(DEPTH):
        g_norm, ada_w, ada_b, mp = layers[i]
        last = i == DEPTH - 1
        shift, scale, gate = jnp.split(jax.nn.silu(c) @ ada_w + ada_b, 3, axis=-1)
        shift_c, scale_c, gate_c = jnp.split(jax.nn.silu(c_ctx) @ ada_w + ada_b, 3, axis=-1)
        h = modulate(x, g_norm, shift[:, None], scale[:, None])
        hc = modulate(ctx, g_norm, shift_c, scale_c)
        if i % N_MIXERS == 0:
            y, yc = mixer_a(h, hc, *mp, cos_a, sin_a, not last)
        else:
            y, yc = mixer_b(h, hc, *mp, cos_b, sin_b, not last)
        x = x + gate[:, None] * y
        if not last:
            ctx = ctx + gate_c * yc
    return rmsnorm(x, final_g)
```

```python
import functools
import math

import jax
import jax.numpy as jnp
from jax import lax
from jax.experimental import pallas as pl
from jax.experimental.pallas import tpu as pltpu

F32 = jnp.float32
BF16 = jnp.bfloat16

GRID_W = 64
ROPE_BASE = 10000.0
EPS = 1e-6
NEG_INF = -1e30
LOG2E = 1.4426950408889634
LANES = 128
WINDOW = 128

A_HEADS, A_KV_HEADS, A_HEAD_DIM = 16, 4, 64
A_WIDTH = A_HEADS * A_HEAD_DIM
A_KV_WIDTH = A_KV_HEADS * A_HEAD_DIM
B_HEADS, B_NOPE, B_ROPE, B_V = 16, 64, 32, 64
B_Q_RANK, B_KV_RANK = 256, 128
B_WIDTH = B_HEADS * B_V
B_FEAT = 2 * LANES

VMEM_LIMIT = 56 * 1024 * 1024


def _silu(x):
    return x * (1.0 / (1.0 + jnp.exp(-x)))


def _rms(x, g):
    return x * lax.rsqrt(jnp.mean(x * x, axis=-1, keepdims=True) + EPS) * g


def _rope(x, c, sa, sb, quarter):
    return (x * c + pltpu.roll(x, LANES - quarter, 1) * sa + pltpu.roll(x, quarter, 1) * sb)


def _dot(a, b):
    return jnp.dot(a, b, preferred_element_type=F32)


def _dot_nt(a, b):
    return lax.dot_general(a, b, (((1,), (1,)), ((), ())), preferred_element_type=F32)


def _mod_kernel(c_ref, w_ref, b_ref, o_ref):
    a = _silu(c_ref[...])
    o_ref[...] = jnp.dot(a, w_ref[...], precision=lax.Precision.HIGHEST,
                         preferred_element_type=F32) + b_ref[...]


def _modulation(cc, ada_w, ada_b):
    d = ada_w.shape[0]
    n = ada_w.shape[1]
    return pl.pallas_call(
        _mod_kernel,
        grid=(n // d,),
        in_specs=[pl.BlockSpec((8, d), lambda j: (0, 0)),
                  pl.BlockSpec((d, d), lambda j: (0, j)),
                  pl.BlockSpec((1, d), lambda j: (0, j))],
        out_specs=pl.BlockSpec((8, d), lambda j: (0, j)),
        out_shape=jax.ShapeDtypeStruct((8, n), F32),
        name="adaln_mod",
    )(cc, ada_w, ada_b.reshape(1, n))


def _proj_a_kernel(x_ref, mod_ref, ng_ref, wq_ref, wk_ref, wv_ref, wg_ref, c_ref, sa_ref, sb_ref,
                   q_ref, kd_ref, vd_ref, g_ref, *, qscale):
    mod = mod_ref[0]
    h = (_rms(x_ref[0], ng_ref[...]) * (1.0 + mod[1:2]) + mod[0:1]).astype(BF16)
    c, sa, sb = c_ref[...], sa_ref[...], sb_ref[...]
    quarter = A_HEAD_DIM // 4
    q = _dot(h, wq_ref[...])
    for j in range(A_WIDTH // LANES):
        sl = slice(j * LANES, (j + 1) * LANES)
        q_ref[0, :, sl] = (_rope(q[:, sl], c, sa, sb, quarter) * qscale).astype(BF16)
    lo = lax.broadcasted_iota(jnp.int32, c.shape, 1) < A_HEAD_DIM
    k = _dot(h, wk_ref[...])
    v = _dot(h, wv_ref[...])
    for m in range(A_KV_WIDTH // LANES):
        sl = slice(m * LANES, (m + 1) * LANES)
        kp = _rope(k[:, sl], c, sa, sb, quarter)
        vp = v[:, sl]
        ks = pltpu.roll(kp, A_HEAD_DIM, 1)
        vs = pltpu.roll(vp, A_HEAD_DIM, 1)
        kd_ref[0, :, 2 * m * LANES:(2 * m + 1) * LANES] = jnp.where(lo, kp, ks).astype(BF16)
        kd_ref[0, :, (2 * m + 1) * LANES:(2 * m + 2) * LANES] = jnp.where(lo, ks, kp).astype(BF16)
        vd_ref[0, :, 2 * m * LANES:(2 * m + 1) * LANES] = jnp.where(lo, vp, vs).astype(BF16)
        vd_ref[0, :, (2 * m + 1) * LANES:(2 * m + 2) * LANES] = jnp.where(lo, vs, vp).astype(BF16)
    g_ref[0] = _silu(_dot(h, wg_ref[...])).astype(BF16)


def _proj_a(x, mod, norm_g, wq, wk, wv, wg, tabs, tm):
    bsz, s, d = x.shape
    kdw = A_KV_HEADS * LANES
    full = lambda a: pl.BlockSpec(a.shape, lambda b, t: (0,) * a.ndim)
    tab = pl.BlockSpec((tm, LANES), lambda b, t: (t, 0))
    tok = lambda w: pl.BlockSpec((1, tm, w), lambda b, t: (b, t, 0))
    return pl.pallas_call(
        functools.partial(_proj_a_kernel, qscale=A_HEAD_DIM ** -0.5 * LOG2E),
        grid=(bsz, s // tm),
        in_specs=[tok(d), pl.BlockSpec((1, 3, d), lambda b, t: (b, 0, 0)), full(norm_g),
                  full(wq), full(wk), full(wv), full(wg), tab, tab, tab],
        out_specs=[tok(A_WIDTH), tok(kdw), tok(kdw), tok(A_WIDTH)],
        out_shape=[jax.ShapeDtypeStruct((bsz, s, A_WIDTH), BF16),
                   jax.ShapeDtypeStruct((bsz, s, kdw), BF16),
                   jax.ShapeDtypeStruct((bsz, s, kdw), BF16),
                   jax.ShapeDtypeStruct((bsz, s, A_WIDTH), BF16)],
        compiler_params=pltpu.CompilerParams(
            dimension_semantics=("parallel", "parallel"), vmem_limit_bytes=VMEM_LIMIT),
        name="proj_a",
    )(x, mod, norm_g, wq, wk, wv, wg, *tabs)


def _attn_a_core(sink_ref, q, kd_loc, vd_loc, kd_ctx, vd_ctx, bias_ok, tq):
    group = A_HEADS // A_KV_HEADS
    lo = lax.broadcasted_iota(jnp.int32, (tq, LANES), 1) < A_HEAD_DIM
    zero = jnp.zeros((tq, LANES), BF16)
    outs = []
    for g in range(A_KV_HEADS):
        parts = []
        for p in range(group // 2):
            qp = q[:, (2 * g + p) * LANES:(2 * g + p + 1) * LANES]
            parts += [jnp.where(lo, qp, zero), jnp.where(lo, zero, qp)]
        qg = jnp.concatenate(parts, axis=0)
        sink = jnp.concatenate(
            [jnp.full((tq, 1), sink_ref[group * g + j] * LOG2E, F32) for j in range(group)], axis=0)
        ksl = slice(g * LANES, (g + 1) * LANES)
        s_ctx = _dot_nt(qg, kd_ctx[:, ksl])
        m = jnp.maximum(sink, jnp.max(s_ctx, axis=1, keepdims=True))
        if kd_loc is not None:
            s_loc = jnp.where(bias_ok, _dot_nt(qg, kd_loc[:, ksl]), NEG_INF)
            m = jnp.maximum(m, jnp.max(s_loc, axis=1, keepdims=True))
        e_ctx = jnp.exp2(s_ctx - m)
        l = jnp.exp2(sink - m) + jnp.sum(e_ctx, axis=1, keepdims=True)
        o = _dot(e_ctx.astype(BF16), vd_ctx[:, ksl])
        if kd_loc is not None:
            e_loc = jnp.exp2(s_loc - m)
            l = l + jnp.sum(e_loc, axis=1, keepdims=True)
            o = o + _dot(e_loc.astype(BF16), vd_loc[:, ksl])
        o = o * (1.0 / l)
        for p in range(group // 2):
            outs.append(jnp.where(lo, o[(2 * p) * tq:(2 * p + 1) * tq],
                                  o[(2 * p + 1) * tq:(2 * p + 2) * tq]))
    return jnp.concatenate(outs, axis=1)


def _attn_a_finish(o, g_ref, x_ref, mod_ref, wo_ref, out_ref):
    z = (o * g_ref[0].astype(F32)).astype(BF16)
    out_ref[0] = x_ref[0] + mod_ref[0][2:3] * _dot(z, wo_ref[...])


def _attn_a_kernel(sink_ref, q_ref, kp_ref, kc_ref, kn_ref, vp_ref, vc_ref, vn_ref, kx_ref, vx_ref,
                   g_ref, x_ref, mod_ref, wo_ref, out_ref, *, tq, seq):
    i = pl.program_id(1)
    kd_loc = jnp.concatenate([kp_ref[0], kc_ref[0], kn_ref[0]], axis=0)
    vd_loc = jnp.concatenate([vp_ref[0], vc_ref[0], vn_ref[0]], axis=0)
    nk = tq + 2 * WINDOW
    group = A_HEADS // A_KV_HEADS
    row = lax.broadcasted_iota(jnp.int32, (group * tq, nk), 0) & (tq - 1)
    col = lax.broadcasted_iota(jnp.int32, (group * tq, nk), 1)
    kpos = i * tq - WINDOW + col
    ok = (jnp.abs(row + WINDOW - col) <= WINDOW) & (kpos >= 0) & (kpos < seq)
    o = _attn_a_core(sink_ref, q_ref[0], kd_loc, vd_loc, kx_ref[0], vx_ref[0], ok, tq)
    _attn_a_finish(o, g_ref, x_ref, mod_ref, wo_ref, out_ref)


def _attn_a_ctx_kernel(sink_ref, q_ref, kx_ref, vx_ref, g_ref, x_ref, mod_ref, wo_ref, out_ref, *, tq):
    o = _attn_a_core(sink_ref, q_ref[0], None, None, kx_ref[0], vx_ref[0], None, tq)
    _attn_a_finish(o, g_ref, x_ref, mod_ref, wo_ref, out_ref)


def _attn_a(sinks, q, kd, vd, kdc, vdc, gact, x, mod, wo, tq):
    bsz, s, d = x.shape
    c = kdc.shape[1]
    kdw = kd.shape[2]
    assert tq & (tq - 1) == 0 and tq % WINDOW == 0
    r = tq // WINDOW
    nblk = s // WINDOW
    tok = lambda w: pl.BlockSpec((1, tq, w), lambda b, i: (b, i, 0))
    prev = pl.BlockSpec((1, WINDOW, kdw), lambda b, i: (b, jnp.maximum(i * r - 1, 0), 0))
    nxt = pl.BlockSpec((1, WINDOW, kdw), lambda b, i: (b, jnp.minimum((i + 1) * r, nblk - 1), 0))
    ctx = pl.BlockSpec((1, c, kdw), lambda b, i: (b, 0, 0))
    return pl.pallas_call(
        functools.partial(_attn_a_kernel, tq=tq, seq=s),
        grid=(bsz, s // tq),
        in_specs=[pl.BlockSpec(memory_space=pltpu.SMEM), tok(A_WIDTH),
                  prev, tok(kdw), nxt, prev, tok(kdw), nxt, ctx, ctx,
                  tok(A_WIDTH), tok(d), pl.BlockSpec((1, 3, d), lambda b, i: (b, 0, 0)),
                  pl.BlockSpec(wo.shape, lambda b, i: (0, 0))],
        out_specs=tok(d),
        out_shape=jax.ShapeDtypeStruct((bsz, s, d), F32),
        compiler_params=pltpu.CompilerParams(
            dimension_semantics=("parallel", "parallel"), vmem_limit_bytes=VMEM_LIMIT),
        name="attn_a",
    )(sinks, q, kd, kd, kd, vd, vd, vd, kdc, vdc, gact, x, mod, wo)


def _attn_a_ctx(sinks, qc, kdc, vdc, gc, ctx, mod, wo):
    bsz, c, d = ctx.shape
    kdw = kdc.shape[2]
    tok = lambda w: pl.BlockSpec((1, c, w), lambda b: (b, 0, 0))
    return pl.pallas_call(
        functools.partial(_attn_a_ctx_kernel, tq=c),
        grid=(bsz,),
        in_specs=[pl.BlockSpec(memory_space=pltpu.SMEM), tok(A_WIDTH), tok(kdw), tok(kdw),
                  tok(A_WIDTH), tok(d), pl.BlockSpec((1, 3, d), lambda b: (b, 0, 0)),
                  pl.BlockSpec(wo.shape, lambda b: (0, 0))],
        out_specs=tok(d),
        out_shape=jax.ShapeDtypeStruct((bsz, c, d), F32),
        compiler_params=pltpu.CompilerParams(
            dimension_semantics=("parallel",), vmem_limit_bytes=VMEM_LIMIT),
        name="attn_a_ctx",
    )(sinks, qc, kdc, vdc, gc, ctx, mod, wo)


def _kext(a, kvg, c, sa, sb):
    ckv = _rms(a[:, B_Q_RANK:B_Q_RANK + B_KV_RANK], kvg)
    kr = a[:, B_Q_RANK + B_KV_RANK:]
    if c is not None:
        kr = _rope(kr, c, sa, sb, B_ROPE // 4)
    return jnp.concatenate([ckv, kr], axis=1).astype(BF16)


def _proj_b_kernel(x_ref, mod_ref, ng_ref, w1_ref, wg_ref, qg_ref, wqn_ref, wqr_ref, wabs_ref,
                   kvg_ref, c_ref, sa_ref, sb_ref, qp_ref, k_ref, g_ref, *, qscale, tm):
    mod = mod_ref[0]
    h = (_rms(x_ref[0], ng_ref[...]) * (1.0 + mod[1:2]) + mod[0:1]).astype(BF16)
    c, sa, sb = c_ref[...], sa_ref[...], sb_ref[...]
    a = _dot(h, w1_ref[...])
    k_ref[0] = _kext(a, kvg_ref[...], c, sa, sb)
    g_ref[0] = _silu(_dot(h, wg_ref[...])).astype(BF16)
    cq = _rms(a[:, :B_Q_RANK], qg_ref[...]).astype(BF16)
    qn = _dot(cq, wqn_ref[...]).astype(BF16)
    qr = _dot(cq, wqr_ref[...])
    lane = lax.broadcasted_iota(jnp.int32, (tm, LANES), 1)
    heads_per_chunk = LANES // B_ROPE
    for cidx in range(B_HEADS * B_ROPE // LANES):
        qrc = _rope(qr[:, cidx * LANES:(cidx + 1) * LANES], c, sa, sb, B_ROPE // 4) * qscale
        for j in range(heads_per_chunk):
            hd = cidx * heads_per_chunk + j
            keep = (lane >= j * B_ROPE) & (lane < (j + 1) * B_ROPE)
            part = jnp.where(keep, qrc, 0.0).astype(BF16)
            for t in range(tm // LANES):
                qp_ref[0, t, hd, :, LANES:] = part[t * LANES:(t + 1) * LANES]
    for m in range(B_HEADS // 2):
        qa = (_dot(qn[:, m * LANES:(m + 1) * LANES], wabs_ref[m]) * qscale).astype(BF16)
        for half in range(2):
            for t in range(tm // LANES):
                qp_ref[0, t, 2 * m + half, :, :LANES] = (
                    qa[t * LANES:(t + 1) * LANES, half * LANES:(half + 1) * LANES])


def _proj_b(x, mod, norm_g, w1, wg, qg, wqn, wqr, wabs, kvg, tabs, tm):
    bsz, s, d = x.shape
    full = lambda a: pl.BlockSpec(a.shape, lambda b, t: (0,) * a.ndim)
    tab = pl.BlockSpec((tm, LANES), lambda b, t: (t, 0))
    tok = lambda w: pl.BlockSpec((1, tm, w), lambda b, t: (b, t, 0))
    nt = tm // LANES
    return pl.pallas_call(
        functools.partial(_proj_b_kernel, qscale=(B_NOPE + B_ROPE) ** -0.5 * LOG2E, tm=tm),
        grid=(bsz, s // tm),
        in_specs=[tok(d), pl.BlockSpec((1, 3, d), lambda b, t: (b, 0, 0)), full(norm_g),
                  full(w1), full(wg), full(qg), full(wqn), full(wqr), full(wabs), full(kvg),
                  tab, tab, tab],
        out_specs=[pl.BlockSpec((1, nt, B_HEADS, LANES, B_FEAT), lambda b, t: (b, t, 0, 0, 0)),
                   tok(B_FEAT), tok(B_WIDTH)],
        out_shape=[jax.ShapeDtypeStruct((bsz, s // LANES, B_HEADS, LANES, B_FEAT), BF16),
                   jax.ShapeDtypeStruct((bsz, s, B_FEAT), BF16),
                   jax.ShapeDtypeStruct((bsz, s, B_WIDTH), BF16)],
        compiler_params=pltpu.CompilerParams(
            dimension_semantics=("parallel", "parallel"), vmem_limit_bytes=VMEM_LIMIT),
        name="proj_b",
    )(x, mod, norm_g, w1, wg, qg, wqn, wqr, wabs, kvg, *tabs)


def _proj_b_ctx_kernel(x_ref, mod_ref, ng_ref, w1_ref, kvg_ref, k_ref):
    mod = mod_ref[0]
    h = (_rms(x_ref[0], ng_ref[...]) * (1.0 + mod[1:2]) + mod[0:1]).astype(BF16)
    k_ref[0] = _kext(_dot(h, w1_ref[...]), kvg_ref[...], None, None, None)


def _proj_b_ctx(ctx, mod, norm_g, w1, kvg):
    bsz, c, d = ctx.shape
    full = lambda a: pl.BlockSpec(a.shape, lambda b: (0,) * a.ndim)
    return pl.pallas_call(
        _proj_b_ctx_kernel,
        grid=(bsz,),
        in_specs=[pl.BlockSpec((1, c, d), lambda b: (b, 0, 0)),
                  pl.BlockSpec((1, 3, d), lambda b: (b, 0, 0)), full(norm_g), full(w1), full(kvg)],
        out_specs=pl.BlockSpec((1, c, B_FEAT), lambda b: (b, 0, 0)),
        out_shape=jax.ShapeDtypeStruct((bsz, c, B_FEAT), BF16),
        compiler_params=pltpu.CompilerParams(dimension_semantics=("parallel",)),
        name="proj_b_ctx",
    )(ctx, mod, norm_g, w1, kvg)


def _flash_b_kernel(q_ref, k_ref, g_ref, x_ref, mod_ref, wv_ref, wo_ref, fg_ref, out_ref,
                    m_ref, l_ref, acc_ref, *, tq, tk, nk):
    rows = B_HEADS * tq
    m_ref[...] = jnp.full((rows, 1), NEG_INF, F32)
    l_ref[...] = jnp.zeros((rows, 1), F32)
    acc_ref[...] = jnp.zeros((rows, LANES), F32)

    def body(j, carry):
        kc = k_ref[0, pl.ds(pl.multiple_of(j * tk, tk), tk), :]
        s = _dot_nt(q_ref[0].reshape(rows, B_FEAT), kc)
        m_old = m_ref[...]
        m_new = jnp.maximum(m_old, jnp.max(s, axis=1, keepdims=True))
        alpha = jnp.exp2(m_old - m_new)
        e = jnp.exp2(s - m_new)
        l_ref[...] = alpha * l_ref[...] + jnp.sum(e, axis=1, keepdims=True)
        acc_ref[...] = alpha * acc_ref[...] + _dot(e.astype(BF16), kc[:, :LANES])
        m_ref[...] = m_new
        return carry

    lax.fori_loop(0, nk // tk, body, 0)
    o_lat = (acc_ref[...] * (1.0 / l_ref[...])).astype(BF16)
    outs = []
    for m in range(B_HEADS // 2):
        pair = jnp.concatenate([o_lat[(2 * m) * tq:(2 * m + 1) * tq],
                                o_lat[(2 * m + 1) * tq:(2 * m + 2) * tq]], axis=1)
        outs.append(_dot(pair, wv_ref[m]))
    o = jnp.concatenate(outs, axis=1)
    z = (o * g_ref[0].astype(F32)).astype(BF16)
    xo = x_ref[0] + mod_ref[0][2:3] * _dot(z, wo_ref[...])
    out_ref[0] = _rms(xo, fg_ref[...])


def _flash_b(qp, kall, gact, x, mod, wv, wo, final_g, tk):
    bsz, s, d = x.shape
    nk = kall.shape[1]
    tq = LANES
    assert nk % tk == 0
    rows = B_HEADS * tq
    tok = lambda w: pl.BlockSpec((1, tq, w), lambda b, i: (b, i, 0))
    full = lambda a: pl.BlockSpec(a.shape, lambda b, i: (0,) * a.ndim)
    return pl.pallas_call(
        functools.partial(_flash_b_kernel, tq=tq, tk=tk, nk=nk),
        grid=(bsz, s // tq),
        in_specs=[pl.BlockSpec((1, None, B_HEADS, tq, B_FEAT), lambda b, i: (b, i, 0, 0, 0)),
                  pl.BlockSpec((1, nk, B_FEAT), lambda b, i: (b, 0, 0)),
                  tok(B_WIDTH), tok(d), pl.BlockSpec((1, 3, d), lambda b, i: (b, 0, 0)),
                  full(wv), full(wo), full(final_g)],
        out_specs=tok(d),
        out_shape=jax.ShapeDtypeStruct((bsz, s, d), F32),
        scratch_shapes=[pltpu.VMEM((rows, 1), F32), pltpu.VMEM((rows, 1), F32),
                        pltpu.VMEM((rows, LANES), F32)],
        compiler_params=pltpu.CompilerParams(
            dimension_semantics=("parallel", "parallel"), vmem_limit_bytes=VMEM_LIMIT),
        name="flash_b",
    )(qp, kall, gact, x, mod, wv, wo, final_g)


def _rope_tables(n_tokens, rot_dim, n_ctx):
    n_rows = n_tokens // GRID_W
    row = jnp.broadcast_to(jnp.arange(n_rows)[:, None], (n_rows, GRID_W)).reshape(-1)
    col = jnp.broadcast_to(jnp.arange(GRID_W)[None, :], (n_rows, GRID_W)).reshape(-1)
    nf = rot_dim // 4
    inv = ROPE_BASE ** (-jnp.arange(nf, dtype=F32) / nf)
    ar = row.astype(F32)[:, None] * inv
    ac = col.astype(F32)[:, None] * inv
    ang = jnp.concatenate([ar, ar, ac, ac], axis=-1)
    reps = LANES // rot_dim
    cos = jnp.tile(jnp.cos(ang), (1, reps))
    sin = jnp.tile(jnp.sin(ang), (1, reps))
    first = (jnp.arange(LANES) % (2 * nf)) < nf
    sa = jnp.where(first, -sin, 0.0)
    sb = jnp.where(first, 0.0, sin)
    pad = lambda t, v: jnp.concatenate([jnp.full((n_ctx, LANES), v, F32), t], axis=0)
    return pad(cos, 1.0), pad(sa, 0.0), pad(sb, 0.0)


def _split_mod(modout, bsz):
    d = modout.shape[1] // 3
    m = modout.reshape(8, 3, d)
    lat = m[:bsz]
    ctx = jnp.broadcast_to(m[bsz:bsz + 1], (bsz, 3, d))
    return lat, ctx


def kernel(x, c, ctx, c_ctx, norm_g_0, ada_w_0, ada_b_0, a_w_in_0, a_sinks_0, a_w_o_0, norm_g_1,
           ada_w_1, ada_b_1, b_w_in_1, b_q_norm_1, b_w_uq_1, b_kv_norm_1, b_w_ukv_1, b_w_o_1, final_g):
    bsz, s, d = x.shape
    n_ctx = ctx.shape[1]
    assert bsz + 1 <= 8
    cc = jnp.concatenate([c, c_ctx[None], jnp.zeros((8 - bsz - 1, d), F32)], axis=0)
    row = lambda g: g.reshape(1, -1)

    mod_x, mod_c = _split_mod(_modulation(cc, ada_w_0, ada_b_0), bsz)
    w = a_w_in_0.astype(BF16)
    wq, wk = w[:, :A_WIDTH], w[:, A_WIDTH:A_WIDTH + A_KV_WIDTH]
    wv, wg = w[:, A_WIDTH + A_KV_WIDTH:A_WIDTH + 2 * A_KV_WIDTH], w[:, A_WIDTH + 2 * A_KV_WIDTH:]
    tabs_a = _rope_tables(s, A_HEAD_DIM, n_ctx)
    tabs_ac = tuple(t[:n_ctx] for t in tabs_a)
    tabs_ax = tuple(t[n_ctx:] for t in tabs_a)
    wo_a = a_w_o_0.astype(BF16)
    q, kd, vd, gact = _proj_a(x, mod_x, row(norm_g_0), wq, wk, wv, wg, tabs_ax, tm=512)
    qc, kdc, vdc, gc = _proj_a(ctx, mod_c, row(norm_g_0), wq, wk, wv, wg, tabs_ac, tm=n_ctx)
    x1 = _attn_a(a_sinks_0, q, kd, vd, kdc, vdc, gact, x, mod_x, wo_a, tq=256)
    ctx1 = _attn_a_ctx(a_sinks_0, qc, kdc, vdc, gc, ctx, mod_c, wo_a)

    mod_x, mod_c = _split_mod(_modulation(cc, ada_w_1, ada_b_1), bsz)
    wb = b_w_in_1.astype(BF16)
    n_kv = B_Q_RANK + B_KV_RANK
    w1 = jnp.concatenate([wb[:, :n_kv]] + [wb[:, n_kv:n_kv + B_ROPE]] * (LANES // B_ROPE), axis=1)
    wgb = wb[:, n_kv + B_ROPE:]
    wuq = b_w_uq_1.astype(BF16).reshape(B_Q_RANK, B_HEADS, B_NOPE + B_ROPE)
    wqn = wuq[:, :, :B_NOPE].reshape(B_Q_RANK, B_HEADS * B_NOPE)
    wqr = wuq[:, :, B_NOPE:].reshape(B_Q_RANK, B_HEADS * B_ROPE)
    wukv = b_w_ukv_1.astype(BF16).reshape(B_KV_RANK, B_HEADS, B_NOPE + B_V)
    wkn_t = jnp.transpose(wukv[:, :, :B_NOPE], (1, 2, 0)).reshape(B_HEADS // 2, 2, B_NOPE, B_KV_RANK)
    zk = jnp.zeros_like(wkn_t[:, 0])
    wabs = jnp.concatenate([jnp.concatenate([wkn_t[:, 0], zk], axis=2),
                            jnp.concatenate([zk, wkn_t[:, 1]], axis=2)], axis=1)
    wvh = jnp.transpose(wukv[:, :, B_NOPE:], (1, 0, 2)).reshape(B_HEADS // 2, 2, B_KV_RANK, B_V)
    zv = jnp.zeros_like(wvh[:, 0])
    wvup = jnp.concatenate([jnp.concatenate([wvh[:, 0], zv], axis=2),
                            jnp.concatenate([zv, wvh[:, 1]], axis=2)], axis=1)
    tabs_b = _rope_tables(s, B_ROPE, 0)
    qp, kx, gb = _proj_b(x1, mod_x, row(norm_g_1), w1, wgb, row(b_q_norm_1), wqn, wqr, wabs,
                         row(b_kv_norm_1), tabs_b, tm=256)
    kc = _proj_b_ctx(ctx1, mod_c, row(norm_g_1), w1, row(b_kv_norm_1))
    kall = jnp.concatenate([kc, kx], axis=1)
    return _flash_b(qp, kall, gb, x1, mod_x, wvup, b_w_o_1.astype(BF16), row(final_g), tk=768)
```

```python
import functools
import math

import jax
import jax.numpy as jnp
from jax import lax
from jax.experimental import pallas as pl
from jax.experimental.pallas import tpu as pltpu

F32 = jnp.float32
BF16 = jnp.bfloat16

GRID_W = 64
ROPE_BASE = 10000.0
EPS = 1e-6
NEG_INF = -1e30
LOG2E = 1.4426950408889634
LANES = 128
WINDOW = 128

A_HEADS, A_KV_HEADS, A_HEAD_DIM = 16, 4, 64
A_WIDTH = A_HEADS * A_HEAD_DIM
A_KV_WIDTH = A_KV_HEADS * A_HEAD_DIM
B_HEADS, B_NOPE, B_ROPE, B_V = 16, 64, 32, 64
B_Q_RANK, B_KV_RANK = 256, 128
B_WIDTH = B_HEADS * B_V
B_FEAT = 2 * LANES

VMEM_LIMIT = 56 * 1024 * 1024


def _silu(x):
    return x * (1.0 / (1.0 + jnp.exp(-x)))


def _rms(x, g):
    return x * lax.rsqrt(jnp.mean(x * x, axis=-1, keepdims=True) + EPS) * g


def _rope(x, c, sa, sb, quarter):
    return (x * c + pltpu.roll(x, LANES - quarter, 1) * sa + pltpu.roll(x, quarter, 1) * sb)


def _dot(a, b):
    return jnp.dot(a, b, preferred_element_type=F32)


def _dot_nt(a, b):
    return lax.dot_general(a, b, (((1,), (1,)), ((), ())), preferred_element_type=F32)


def _mod_kernel(c_ref, w_ref, b_ref, o_ref):
    a = _silu(c_ref[...])
    o_ref[...] = jnp.dot(a, w_ref[...], precision=lax.Precision.HIGHEST,
                         preferred_element_type=F32) + b_ref[...]


def _modulation(cc, ada_w, ada_b):
    d = ada_w.shape[0]
    n = ada_w.shape[1]
    return pl.pallas_call(
        _mod_kernel,
        grid=(n // d,),
        in_specs=[pl.BlockSpec((8, d), lambda j: (0, 0)),
                  pl.BlockSpec((d, d), lambda j: (0, j)),
                  pl.BlockSpec((1, d), lambda j: (0, j))],
        out_specs=pl.BlockSpec((8, d), lambda j: (0, j)),
        out_shape=jax.ShapeDtypeStruct((8, n), F32),
        name="adaln_mod",
    )(cc, ada_w, ada_b.reshape(1, n))


def _proj_a_kernel(x_ref, mod_ref, ng_ref, wq_ref, wk_ref, wv_ref, wg_ref, c_ref, sa_ref, sb_ref,
                   q_ref, kd_ref, vd_ref, g_ref, *, qscale):
    mod = mod_ref[0]
    h = (_rms(x_ref[0], ng_ref[...]) * (1.0 + mod[1:2]) + mod[0:1]).astype(BF16)
    c, sa, sb = c_ref[...], sa_ref[...], sb_ref[...]
    quarter = A_HEAD_DIM // 4
    q = _dot(h, wq_ref[...])
    for j in range(A_WIDTH // LANES):
        sl = slice(j * LANES, (j + 1) * LANES)
        q_ref[0, :, sl] = (_rope(q[:, sl], c, sa, sb, quarter) * qscale).astype(BF16)
    lo = lax.broadcasted_iota(jnp.int32, c.shape, 1) < A_HEAD_DIM
    k = _dot(h, wk_ref[...])
    v = _dot(h, wv_ref[...])
    for m in range(A_KV_WIDTH // LANES):
        sl = slice(m * LANES, (m + 1) * LANES)
        kp = _rope(k[:, sl], c, sa, sb, quarter)
        vp = v[:, sl]
        ks = pltpu.roll(kp, A_HEAD_DIM, 1)
        vs = pltpu.roll(vp, A_HEAD_DIM, 1)
        kd_ref[0, :, 2 * m * LANES:(2 * m + 1) * LANES] = jnp.where(lo, kp, ks).astype(BF16)
        kd_ref[0, :, (2 * m + 1) * LANES:(2 * m + 2) * LANES] = jnp.where(lo, ks, kp).astype(BF16)
        vd_ref[0, :, 2 * m * LANES:(2 * m + 1) * LANES] = jnp.where(lo, vp, vs).astype(BF16)
        vd_ref[0, :, (2 * m + 1) * LANES:(2 * m + 2) * LANES] = jnp.where(lo, vs, vp).astype(BF16)
    g_ref[0] = _silu(_dot(h, wg_ref[...])).astype(BF16)


def _proj_a(x, mod, norm_g, wq, wk, wv, wg, tabs, tm):
    bsz, s, d = x.shape
    kdw = A_KV_HEADS * LANES
    full = lambda a: pl.BlockSpec(a.shape, lambda b, t: (0,) * a.ndim)
    tab = pl.BlockSpec((tm, LANES), lambda b, t: (t, 0))
    tok = lambda w: pl.BlockSpec((1, tm, w), lambda b, t: (b, t, 0))
    return pl.pallas_call(
        functools.partial(_proj_a_kernel, qscale=A_HEAD_DIM ** -0.5 * LOG2E),
        grid=(bsz, s // tm),
        in_specs=[tok(d), pl.BlockSpec((1, 3, d), lambda b, t: (b, 0, 0)), full(norm_g),
                  full(wq), full(wk), full(wv), full(wg), tab, tab, tab],
        out_specs=[tok(A_WIDTH), tok(kdw), tok(kdw), tok(A_WIDTH)],
        out_shape=[jax.ShapeDtypeStruct((bsz, s, A_WIDTH), BF16),
                   jax.ShapeDtypeStruct((bsz, s, kdw), BF16),
                   jax.ShapeDtypeStruct((bsz, s, kdw), BF16),
                   jax.ShapeDtypeStruct((bsz, s, A_WIDTH), BF16)],
        compiler_params=pltpu.CompilerParams(
            dimension_semantics=("parallel", "parallel"), vmem_limit_bytes=VMEM_LIMIT),
        name="proj_a",
    )(x, mod, norm_g, wq, wk, wv, wg, *tabs)


def _attn_a_core(sink_ref, q, kd_loc, vd_loc, kd_ctx, vd_ctx, bias_ok, tq):
    group = A_HEADS // A_KV_HEADS
    lo = lax.broadcasted_iota(jnp.int32, (tq, LANES), 1) < A_HEAD_DIM
    zero = jnp.zeros((tq, LANES), BF16)
    outs = []
    for g in range(A_KV_HEADS):
        parts = []
        for p in range(group // 2):
            qp = q[:, (2 * g + p) * LANES:(2 * g + p + 1) * LANES]
            parts += [jnp.where(lo, qp, zero), jnp.where(lo, zero, qp)]
        qg = jnp.concatenate(parts, axis=0)
        sink = jnp.concatenate(
            [jnp.full((tq, 1), sink_ref[group * g + j] * LOG2E, F32) for j in range(group)], axis=0)
        ksl = slice(g * LANES, (g + 1) * LANES)
        s_ctx = _dot_nt(qg, kd_ctx[:, ksl])
        m = jnp.maximum(sink, jnp.max(s_ctx, axis=1, keepdims=True))
        if kd_loc is not None:
            s_loc = jnp.where(bias_ok, _dot_nt(qg, kd_loc[:, ksl]), NEG_INF)
            m = jnp.maximum(m, jnp.max(s_loc, axis=1, keepdims=True))
        e_ctx = jnp.exp2(s_ctx - m)
        l = jnp.exp2(sink - m) + jnp.sum(e_ctx, axis=1, keepdims=True)
        o = _dot(e_ctx.astype(BF16), vd_ctx[:, ksl])
        if kd_loc is not None:
            e_loc = jnp.exp2(s_loc - m)
            l = l + jnp.sum(e_loc, axis=1, keepdims=True)
            o = o + _dot(e_loc.astype(BF16), vd_loc[:, ksl])
        o = o * (1.0 / l)
        for p in range(group // 2):
            outs.append(jnp.where(lo, o[(2 * p) * tq:(2 * p + 1) * tq],
                                  o[(2 * p + 1) * tq:(2 * p + 2) * tq]))
    return jnp.concatenate(outs, axis=1)


def _attn_a_finish(o, g_ref, x_ref, mod_ref, wo_ref, out_ref):
    z = (o * g_ref[0].astype(F32)).astype(BF16)
    out_ref[0] = x_ref[0] + mod_ref[0][2:3] * _dot(z, wo_ref[...])


def _attn_a_kernel(sink_ref, q_ref, kp_ref, kc_ref, kn_ref, vp_ref, vc_ref, vn_ref, kx_ref, vx_ref,
                   g_ref, x_ref, mod_ref, wo_ref, out_ref, *, tq, seq):
    i = pl.program_id(1)
    kd_loc = jnp.concatenate([kp_ref[0], kc_ref[0], kn_ref[0]], axis=0)
    vd_loc = jnp.concatenate([vp_ref[0], vc_ref[0], vn_ref[0]], axis=0)
    nk = tq + 2 * WINDOW
    group = A_HEADS // A_KV_HEADS
    row = lax.broadcasted_iota(jnp.int32, (group * tq, nk), 0) & (tq - 1)
    col = lax.broadcasted_iota(jnp.int32, (group * tq, nk), 1)
    kpos = i * tq - WINDOW + col
    ok = (jnp.abs(row + WINDOW - col) <= WINDOW) & (kpos >= 0) & (kpos < seq)
    o = _attn_a_core(sink_ref, q_ref[0], kd_loc, vd_loc, kx_ref[0], vx_ref[0], ok, tq)
    _attn_a_finish(o, g_ref, x_ref, mod_ref, wo_ref, out_ref)


def _attn_a_ctx_kernel(sink_ref, q_ref, kx_ref, vx_ref, g_ref, x_ref, mod_ref, wo_ref, out_ref, *, tq):
    o = _attn_a_core(sink_ref, q_ref[0], None, None, kx_ref[0], vx_ref[0], None, tq)
    _attn_a_finish(o, g_ref, x_ref, mod_ref, wo_ref, out_ref)


def _attn_a(sinks, q, kd, vd, kdc, vdc, gact, x, mod, wo, tq):
    bsz, s, d = x.shape
    c = kdc.shape[1]
    kdw = kd.shape[2]
    assert tq & (tq - 1) == 0 and tq % WINDOW == 0
    r = tq // WINDOW
    nblk = s // WINDOW
    tok = lambda w: pl.BlockSpec((1, tq, w), lambda b, i: (b, i, 0))
    prev = pl.BlockSpec((1, WINDOW, kdw), lambda b, i: (b, jnp.maximum(i * r - 1, 0), 0))
    nxt = pl.BlockSpec((1, WINDOW, kdw), lambda b, i: (b, jnp.minimum((i + 1) * r, nblk - 1), 0))
    ctx = pl.BlockSpec((1, c, kdw), lambda b, i: (b, 0, 0))
    return pl.pallas_call(
        functools.partial(_attn_a_kernel, tq=tq, seq=s),
        grid=(bsz, s // tq),
        in_specs=[pl.BlockSpec(memory_space=pltpu.SMEM), tok(A_WIDTH),
                  prev, tok(kdw), nxt, prev, tok(kdw), nxt, ctx, ctx,
                  tok(A_WIDTH), tok(d), pl.BlockSpec((1, 3, d), lambda b, i: (b, 0, 0)),
                  pl.BlockSpec(wo.shape, lambda b, i: (0, 0))],
        out_specs=tok(d),
        out_shape=jax.ShapeDtypeStruct((bsz, s, d), F32),
        compiler_params=pltpu.CompilerParams(
            dimension_semantics=("parallel", "parallel"), vmem_limit_bytes=VMEM_LIMIT),
        name="attn_a",
    )(sinks, q, kd, kd, kd, vd, vd, vd, kdc, vdc, gact, x, mod, wo)


def _attn_a_ctx(sinks, qc, kdc, vdc, gc, ctx, mod, wo):
    bsz, c, d = ctx.shape
    kdw = kdc.shape[2]
    tok = lambda w: pl.BlockSpec((1, c, w), lambda b: (b, 0, 0))
    return pl.pallas_call(
        functools.partial(_attn_a_ctx_kernel, tq=c),
        grid=(bsz,),
        in_specs=[pl.BlockSpec(memory_space=pltpu.SMEM), tok(A_WIDTH), tok(kdw), tok(kdw),
                  tok(A_WIDTH), tok(d), pl.BlockSpec((1, 3, d), lambda b: (b, 0, 0)),
                  pl.BlockSpec(wo.shape, lambda b: (0, 0))],
        out_specs=tok(d),
        out_shape=jax.ShapeDtypeStruct((bsz, c, d), F32),
        compiler_params=pltpu.CompilerParams(
            dimension_semantics=("parallel",), vmem_limit_bytes=VMEM_LIMIT),
        name="attn_a_ctx",
    )(sinks, qc, kdc, vdc, gc, ctx, mod, wo)


def _kext(a, kvg, c, sa, sb):
    ckv = _rms(a[:, B_Q_RANK:B_Q_RANK + B_KV_RANK], kvg)
    kr = a[:, B_Q_RANK + B_KV_RANK:]
    if c is not None:
        kr = _rope(kr, c, sa, sb, B_ROPE // 4)
    return jnp.concatenate([ckv, kr], axis=1).astype(BF16)


def _proj_b_kernel(x_ref, mod_ref, ng_ref, w1_ref, wg_ref, qg_ref, wqn_ref, wqr_ref, wabs_ref,
                   kvg_ref, c_ref, sa_ref, sb_ref, qp_ref, k_ref, g_ref, *, qscale, tm):
    mod = mod_ref[0]
    h = (_rms(x_ref[0], ng_ref[...]) * (1.0 + mod[1:2]) + mod[0:1]).astype(BF16)
    c, sa, sb = c_ref[...], sa_ref[...], sb_ref[...]
    a = _dot(h, w1_ref[...])
    k_ref[0] = _kext(a, kvg_ref[...], c, sa, sb)
    g_ref[0] = _silu(_dot(h, wg_ref[...])).astype(BF16)
    cq = _rms(a[:, :B_Q_RANK], qg_ref[...]).astype(BF16)
    qn = _dot(cq, wqn_ref[...]).astype(BF16)
    qr = _dot(cq, wqr_ref[...])
    lane = lax.broadcasted_iota(jnp.int32, (tm, LANES), 1)
    heads_per_chunk = LANES // B_ROPE
    for cidx in range(B_HEADS * B_ROPE // LANES):
        qrc = _rope(qr[:, cidx * LANES:(cidx + 1) * LANES], c, sa, sb, B_ROPE // 4) * qscale
        for j in range(heads_per_chunk):
            hd = cidx * heads_per_chunk + j
            keep = (lane >= j * B_ROPE) & (lane < (j + 1) * B_ROPE)
            part = jnp.where(keep, qrc, 0.0).astype(BF16)
            for t in range(tm // LANES):
                qp_ref[0, t, hd, :, LANES:] = part[t * LANES:(t + 1) * LANES]
    for m in range(B_HEADS // 2):
        qa = (_dot(qn[:, m * LANES:(m + 1) * LANES], wabs_ref[m]) * qscale).astype(BF16)
        for half in range(2):
            for t in range(tm // LANES):
                qp_ref[0, t, 2 * m + half, :, :LANES] = (
                    qa[t * LANES:(t + 1) * LANES, half * LANES:(half + 1) * LANES])


def _proj_b(x, mod, norm_g, w1, wg, qg, wqn, wqr, wabs, kvg, tabs, tm):
    bsz, s, d = x.shape
    full = lambda a: pl.BlockSpec(a.shape, lambda b, t: (0,) * a.ndim)
    tab = pl.BlockSpec((tm, LANES), lambda b, t: (t, 0))
    tok = lambda w: pl.BlockSpec((1, tm, w), lambda b, t: (b, t, 0))
    nt = tm // LANES
    return pl.pallas_call(
        functools.partial(_proj_b_kernel, qscale=(B_NOPE + B_ROPE) ** -0.5 * LOG2E, tm=tm),
        grid=(bsz, s // tm),
        in_specs=[tok(d), pl.BlockSpec((1, 3, d), lambda b, t: (b, 0, 0)), full(norm_g),
                  full(w1), full(wg), full(qg), full(wqn), full(wqr), full(wabs), full(kvg),
                  tab, tab, tab],
        out_specs=[pl.BlockSpec((1, nt, B_HEADS, LANES, B_FEAT), lambda b, t: (b, t, 0, 0, 0)),
                   tok(B_FEAT), tok(B_WIDTH)],
        out_shape=[jax.ShapeDtypeStruct((bsz, s // LANES, B_HEADS, LANES, B_FEAT), BF16),
                   jax.ShapeDtypeStruct((bsz, s, B_FEAT), BF16),
                   jax.ShapeDtypeStruct((bsz, s, B_WIDTH), BF16)],
        compiler_params=pltpu.CompilerParams(
            dimension_semantics=("parallel", "parallel"), vmem_limit_bytes=VMEM_LIMIT),
        name="proj_b",
    )(x, mod, norm_g, w1, wg, qg, wqn, wqr, wabs, kvg, *tabs)


def _proj_b_ctx_kernel(x_ref, mod_ref, ng_ref, w1_ref, kvg_ref, k_ref):
    mod = mod_ref[0]
    h = (_rms(x_ref[0], ng_ref[...]) * (1.0 + mod[1:2]) + mod[0:1]).astype(BF16)
    k_ref[0] = _kext(_dot(h, w1_ref[...]), kvg_ref[...], None, None, None)


def _proj_b_ctx(ctx, mod, norm_g, w1, kvg):
    bsz, c, d = ctx.shape
    full = lambda a: pl.BlockSpec(a.shape, lambda b: (0,) * a.ndim)
    return pl.pallas_call(
        _proj_b_ctx_kernel,
        grid=(bsz,),
        in_specs=[pl.BlockSpec((1, c, d), lambda b: (b, 0, 0)),
                  pl.BlockSpec((1, 3, d), lambda b: (b, 0, 0)), full(norm_g), full(w1), full(kvg)],
        out_specs=pl.BlockSpec((1, c, B_FEAT), lambda b: (b, 0, 0)),
        out_shape=jax.ShapeDtypeStruct((bsz, c, B_FEAT), BF16),
        compiler_params=pltpu.CompilerParams(dimension_semantics=("parallel",)),
        name="proj_b_ctx",
    )(ctx, mod, norm_g, w1, kvg)


def _flash_b_kernel(q_ref, k_ref, g_ref, x_ref, mod_ref, wv_ref, wo_ref, fg_ref, out_ref,
                    m_ref, l_ref, acc_ref, *, tq, tk, nk, hb):
    rows = B_HEADS * tq
    rb = hb * tq
    m_ref[...] = jnp.full((rows, LANES), NEG_INF, F32)
    l_ref[...] = jnp.zeros((rows, LANES), F32)
    acc_ref[...] = jnp.zeros((rows, LANES), F32)

    def body(j, carry):
        kc = k_ref[0, pl.ds(pl.multiple_of(j * tk, tk), tk), :]
        vc = kc[:, :LANES]
        for sb in range(B_HEADS // hb):
            rs = slice(sb * rb, (sb + 1) * rb)
            q = q_ref[0, sb * hb:(sb + 1) * hb].reshape(rb, B_FEAT)
            s = _dot_nt(q, kc)
            cols = [s[:, c * LANES:(c + 1) * LANES] for c in range(tk // LANES)]
            mx = functools.reduce(jnp.maximum, cols)
            m_old = m_ref[rs]
            m_new = jnp.maximum(m_old, jnp.max(mx, axis=1, keepdims=True))
            alpha = jnp.exp2(m_old - m_new)
            es = [jnp.exp2(col - m_new) for col in cols]
            l_ref[rs] = alpha * l_ref[rs] + functools.reduce(jnp.add, es)
            p = jnp.concatenate([e.astype(BF16) for e in es], axis=1)
            acc_ref[rs] = alpha * acc_ref[rs] + _dot(p, vc)
            m_ref[rs] = m_new
        return carry

    lax.fori_loop(0, nk // tk, body, 0)
    l = jnp.sum(l_ref[...], axis=1, keepdims=True)
    o_lat = (acc_ref[...] * (1.0 / l)).astype(BF16)
    outs = []
    for m in range(B_HEADS // 2):
        pair = jnp.concatenate([o_lat[(2 * m) * tq:(2 * m + 1) * tq],
                                o_lat[(2 * m + 1) * tq:(2 * m + 2) * tq]], axis=1)
        outs.append(_dot(pair, wv_ref[m]))
    o = jnp.concatenate(outs, axis=1)
    z = (o * g_ref[0].astype(F32)).astype(BF16)
    xo = x_ref[0] + mod_ref[0][2:3] * _dot(z, wo_ref[...])
    out_ref[0] = _rms(xo, fg_ref[...])


def _flash_b(qp, kall, gact, x, mod, wv, wo, final_g, tk, hb):
    bsz, s, d = x.shape
    nk = kall.shape[1]
    tq = LANES
    assert nk % tk == 0 and B_HEADS % hb == 0
    rows = B_HEADS * tq
    tok = lambda w: pl.BlockSpec((1, tq, w), lambda b, i: (b, i, 0))
    full = lambda a: pl.BlockSpec(a.shape, lambda b, i: (0,) * a.ndim)
    return pl.pallas_call(
        functools.partial(_flash_b_kernel, tq=tq, tk=tk, nk=nk, hb=hb),
        grid=(bsz, s // tq),
        in_specs=[pl.BlockSpec((1, None, B_HEADS, tq, B_FEAT), lambda b, i: (b, i, 0, 0, 0)),
                  pl.BlockSpec((1, nk, B_FEAT), lambda b, i: (b, 0, 0)),
                  tok(B_WIDTH), tok(d), pl.BlockSpec((1, 3, d), lambda b, i: (b, 0, 0)),
                  full(wv), full(wo), full(final_g)],
        out_specs=tok(d),
        out_shape=jax.ShapeDtypeStruct((bsz, s, d), F32),
        scratch_shapes=[pltpu.VMEM((rows, LANES), F32)] * 3,
        compiler_params=pltpu.CompilerParams(
            dimension_semantics=("parallel", "parallel"), vmem_limit_bytes=VMEM_LIMIT),
        name="flash_b",
    )(qp, kall, gact, x, mod, wv, wo, final_g)


def _rope_tables(n_tokens, rot_dim, n_ctx):
    n_rows = n_tokens // GRID_W
    row = jnp.broadcast_to(jnp.arange(n_rows)[:, None], (n_rows, GRID_W)).reshape(-1)
    col = jnp.broadcast_to(jnp.arange(GRID_W)[None, :], (n_rows, GRID_W)).reshape(-1)
    nf = rot_dim // 4
    inv = ROPE_BASE ** (-jnp.arange(nf, dtype=F32) / nf)
    ar = row.astype(F32)[:, None] * inv
    ac = col.astype(F32)[:, None] * inv
    ang = jnp.concatenate([ar, ar, ac, ac], axis=-1)
    reps = LANES // rot_dim
    cos = jnp.tile(jnp.cos(ang), (1, reps))
    sin = jnp.tile(jnp.sin(ang), (1, reps))
    first = (jnp.arange(LANES) % (2 * nf)) < nf
    sa = jnp.where(first, -sin, 0.0)
    sb = jnp.where(first, 0.0, sin)
    pad = lambda t, v: jnp.concatenate([jnp.full((n_ctx, LANES), v, F32), t], axis=0)
    return pad(cos, 1.0), pad(sa, 0.0), pad(sb, 0.0)


def _split_mod(modout, bsz):
    d = modout.shape[1] // 3
    m = modout.reshape(8, 3, d)
    lat = m[:bsz]
    ctx = jnp.broadcast_to(m[bsz:bsz + 1], (bsz, 3, d))
    return lat, ctx


def kernel(x, c, ctx, c_ctx, norm_g_0, ada_w_0, ada_b_0, a_w_in_0, a_sinks_0, a_w_o_0, norm_g_1,
           ada_w_1, ada_b_1, b_w_in_1, b_q_norm_1, b_w_uq_1, b_kv_norm_1, b_w_ukv_1, b_w_o_1, final_g):
    bsz, s, d = x.shape
    n_ctx = ctx.shape[1]
    assert bsz + 1 <= 8
    cc = jnp.concatenate([c, c_ctx[None], jnp.zeros((8 - bsz - 1, d), F32)], axis=0)
    row = lambda g: g.reshape(1, -1)

    mod_x, mod_c = _split_mod(_modulation(cc, ada_w_0, ada_b_0), bsz)
    w = a_w_in_0.astype(BF16)
    wq, wk = w[:, :A_WIDTH], w[:, A_WIDTH:A_WIDTH + A_KV_WIDTH]
    wv, wg = w[:, A_WIDTH + A_KV_WIDTH:A_WIDTH + 2 * A_KV_WIDTH], w[:, A_WIDTH + 2 * A_KV_WIDTH:]
    tabs_a = _rope_tables(s, A_HEAD_DIM, n_ctx)
    tabs_ac = tuple(t[:n_ctx] for t in tabs_a)
    tabs_ax = tuple(t[n_ctx:] for t in tabs_a)
    wo_a = a_w_o_0.astype(BF16)
    q, kd, vd, gact = _proj_a(x, mod_x, row(norm_g_0), wq, wk, wv, wg, tabs_ax, tm=512)
    qc, kdc, vdc, gc = _proj_a(ctx, mod_c, row(norm_g_0), wq, wk, wv, wg, tabs_ac, tm=n_ctx)
    x1 = _attn_a(a_sinks_0, q, kd, vd, kdc, vdc, gact, x, mod_x, wo_a, tq=256)
    ctx1 = _attn_a_ctx(a_sinks_0, qc, kdc, vdc, gc, ctx, mod_c, wo_a)

    mod_x, mod_c = _split_mod(_modulation(cc, ada_w_1, ada_b_1), bsz)
    wb = b_w_in_1.astype(BF16)
    n_kv = B_Q_RANK + B_KV_RANK
    w1 = jnp.concatenate([wb[:, :n_kv]] + [wb[:, n_kv:n_kv + B_ROPE]] * (LANES // B_ROPE), axis=1)
    wgb = wb[:, n_kv + B_ROPE:]
    wuq = b_w_uq_1.astype(BF16).reshape(B_Q_RANK, B_HEADS, B_NOPE + B_ROPE)
    wqn = wuq[:, :, :B_NOPE].reshape(B_Q_RANK, B_HEADS * B_NOPE)
    wqr = wuq[:, :, B_NOPE:].reshape(B_Q_RANK, B_HEADS * B_ROPE)
    wukv = b_w_ukv_1.astype(BF16).reshape(B_KV_RANK, B_HEADS, B_NOPE + B_V)
    wkn_t = jnp.transpose(wukv[:, :, :B_NOPE], (1, 2, 0)).reshape(B_HEADS // 2, 2, B_NOPE, B_KV_RANK)
    zk = jnp.zeros_like(wkn_t[:, 0])
    wabs = jnp.concatenate([jnp.concatenate([wkn_t[:, 0], zk], axis=2),
                            jnp.concatenate([zk, wkn_t[:, 1]], axis=2)], axis=1)
    wvh = jnp.transpose(wukv[:, :, B_NOPE:], (1, 0, 2)).reshape(B_HEADS // 2, 2, B_KV_RANK, B_V)
    zv = jnp.zeros_like(wvh[:, 0])
    wvup = jnp.concatenate([jnp.concatenate([wvh[:, 0], zv], axis=2),
                            jnp.concatenate([zv, wvh[:, 1]], axis=2)], axis=1)
    tabs_b = _rope_tables(s, B_ROPE, 0)
    qp, kx, gb = _proj_b(x1, mod_x, row(norm_g_1), w1, wgb, row(b_q_norm_1), wqn, wqr, wabs,
                         row(b_kv_norm_1), tabs_b, tm=256)
    kc = _proj_b_ctx(ctx1, mod_c, row(norm_g_1), w1, row(b_kv_norm_1))
    kall = jnp.concatenate([kc, kx], axis=1)
    return _flash_b(qp, kall, gb, x1, mod_x, wvup, b_w_o_1.astype(BF16), row(final_g), tk=768, hb=4)
```

```python
import functools
import math

import jax
import jax.numpy as jnp
from jax import lax
from jax.experimental import pallas as pl
from jax.experimental.pallas import tpu as pltpu

F32 = jnp.float32
BF16 = jnp.bfloat16

GRID_W = 64
ROPE_BASE = 10000.0
EPS = 1e-6
NEG_INF = -1e30
LOG2E = 1.4426950408889634
LANES = 128
WINDOW = 128

A_HEADS, A_KV_HEADS, A_HEAD_DIM = 16, 4, 64
A_WIDTH = A_HEADS * A_HEAD_DIM
A_KV_WIDTH = A_KV_HEADS * A_HEAD_DIM
B_HEADS, B_NOPE, B_ROPE, B_V = 16, 64, 32, 64
B_Q_RANK, B_KV_RANK = 256, 128
B_WIDTH = B_HEADS * B_V
B_FEAT = 2 * LANES

VMEM_LIMIT = 56 * 1024 * 1024


def _silu(x):
    return x * (1.0 / (1.0 + jnp.exp(-x)))


def _rms(x, g):
    return x * lax.rsqrt(jnp.mean(x * x, axis=-1, keepdims=True) + EPS) * g


def _rope(x, c, sa, sb, quarter):
    return (x * c + pltpu.roll(x, LANES - quarter, 1) * sa + pltpu.roll(x, quarter, 1) * sb)


def _dot(a, b):
    return jnp.dot(a, b, preferred_element_type=F32)


def _dot_nt(a, b):
    return lax.dot_general(a, b, (((1,), (1,)), ((), ())), preferred_element_type=F32)


def _mod_kernel(c_ref, w_ref, b_ref, o_ref):
    a = _silu(c_ref[...])
    o_ref[...] = jnp.dot(a, w_ref[...], precision=lax.Precision.HIGHEST,
                         preferred_element_type=F32) + b_ref[...]


def _modulation(cc, ada_w, ada_b):
    d = ada_w.shape[0]
    n = ada_w.shape[1]
    return pl.pallas_call(
        _mod_kernel,
        grid=(n // d,),
        in_specs=[pl.BlockSpec((8, d), lambda j: (0, 0)),
                  pl.BlockSpec((d, d), lambda j: (0, j)),
                  pl.BlockSpec((1, d), lambda j: (0, j))],
        out_specs=pl.BlockSpec((8, d), lambda j: (0, j)),
        out_shape=jax.ShapeDtypeStruct((8, n), F32),
        name="adaln_mod",
    )(cc, ada_w, ada_b.reshape(1, n))


def _proj_a_kernel(x_ref, mod_ref, ng_ref, wq_ref, wk_ref, wv_ref, wg_ref, c_ref, sa_ref, sb_ref,
                   q_ref, kd_ref, vd_ref, g_ref, *, qscale):
    mod = mod_ref[0]
    h = (_rms(x_ref[0], ng_ref[...]) * (1.0 + mod[1:2]) + mod[0:1]).astype(BF16)
    c, sa, sb = c_ref[...], sa_ref[...], sb_ref[...]
    quarter = A_HEAD_DIM // 4
    q = _dot(h, wq_ref[...])
    for j in range(A_WIDTH // LANES):
        sl = slice(j * LANES, (j + 1) * LANES)
        q_ref[0, :, sl] = (_rope(q[:, sl], c, sa, sb, quarter) * qscale).astype(BF16)
    lo = lax.broadcasted_iota(jnp.int32, c.shape, 1) < A_HEAD_DIM
    k = _dot(h, wk_ref[...])
    v = _dot(h, wv_ref[...])
    for m in range(A_KV_WIDTH // LANES):
        sl = slice(m * LANES, (m + 1) * LANES)
        kp = _rope(k[:, sl], c, sa, sb, quarter)
        vp = v[:, sl]
        ks = pltpu.roll(kp, A_HEAD_DIM, 1)
        vs = pltpu.roll(vp, A_HEAD_DIM, 1)
        kd_ref[0, :, 2 * m * LANES:(2 * m + 1) * LANES] = jnp.where(lo, kp, ks).astype(BF16)
        kd_ref[0, :, (2 * m + 1) * LANES:(2 * m + 2) * LANES] = jnp.where(lo, ks, kp).astype(BF16)
        vd_ref[0, :, 2 * m * LANES:(2 * m + 1) * LANES] = jnp.where(lo, vp, vs).astype(BF16)
        vd_ref[0, :, (2 * m + 1) * LANES:(2 * m + 2) * LANES] = jnp.where(lo, vs, vp).astype(BF16)
    g_ref[0] = _silu(_dot(h, wg_ref[...])).astype(BF16)


def _proj_a(x, mod, norm_g, wq, wk, wv, wg, tabs, tm):
    bsz, s, d = x.shape
    kdw = A_KV_HEADS * LANES
    full = lambda a: pl.BlockSpec(a.shape, lambda b, t: (0,) * a.ndim)
    tab = pl.BlockSpec((tm, LANES), lambda b, t: (t, 0))
    tok = lambda w: pl.BlockSpec((1, tm, w), lambda b, t: (b, t, 0))
    return pl.pallas_call(
        functools.partial(_proj_a_kernel, qscale=A_HEAD_DIM ** -0.5 * LOG2E),
        grid=(bsz, s // tm),
        in_specs=[tok(d), pl.BlockSpec((1, 3, d), lambda b, t: (b, 0, 0)), full(norm_g),
                  full(wq), full(wk), full(wv), full(wg), tab, tab, tab],
        out_specs=[tok(A_WIDTH), tok(kdw), tok(kdw), tok(A_WIDTH)],
        out_shape=[jax.ShapeDtypeStruct((bsz, s, A_WIDTH), BF16),
                   jax.ShapeDtypeStruct((bsz, s, kdw), BF16),
                   jax.ShapeDtypeStruct((bsz, s, kdw), BF16),
                   jax.ShapeDtypeStruct((bsz, s, A_WIDTH), BF16)],
        compiler_params=pltpu.CompilerParams(
            dimension_semantics=("parallel", "parallel"), vmem_limit_bytes=VMEM_LIMIT),
        name="proj_a",
    )(x, mod, norm_g, wq, wk, wv, wg, *tabs)


def _attn_a_core(sink_ref, q, kd_loc, vd_loc, kd_ctx, vd_ctx, bias_ok, tq):
    group = A_HEADS // A_KV_HEADS
    lo = lax.broadcasted_iota(jnp.int32, (tq, LANES), 1) < A_HEAD_DIM
    zero = jnp.zeros((tq, LANES), BF16)
    outs = []
    for g in range(A_KV_HEADS):
        parts = []
        for p in range(group // 2):
            qp = q[:, (2 * g + p) * LANES:(2 * g + p + 1) * LANES]
            parts += [jnp.where(lo, qp, zero), jnp.where(lo, zero, qp)]
        qg = jnp.concatenate(parts, axis=0)
        sink = jnp.concatenate(
            [jnp.full((tq, 1), sink_ref[group * g + j] * LOG2E, F32) for j in range(group)], axis=0)
        ksl = slice(g * LANES, (g + 1) * LANES)
        s_ctx = _dot_nt(qg, kd_ctx[:, ksl])
        m = jnp.maximum(sink, jnp.max(s_ctx, axis=1, keepdims=True))
        if kd_loc is not None:
            s_loc = jnp.where(bias_ok, _dot_nt(qg, kd_loc[:, ksl]), NEG_INF)
            m = jnp.maximum(m, jnp.max(s_loc, axis=1, keepdims=True))
        e_ctx = jnp.exp2(s_ctx - m)
        l = jnp.exp2(sink - m) + jnp.sum(e_ctx, axis=1, keepdims=True)
        o = _dot(e_ctx.astype(BF16), vd_ctx[:, ksl])
        if kd_loc is not None:
            e_loc = jnp.exp2(s_loc - m)
            l = l + jnp.sum(e_loc, axis=1, keepdims=True)
            o = o + _dot(e_loc.astype(BF16), vd_loc[:, ksl])
        o = o * (1.0 / l)
        for p in range(group // 2):
            outs.append(jnp.where(lo, o[(2 * p) * tq:(2 * p + 1) * tq],
                                  o[(2 * p + 1) * tq:(2 * p + 2) * tq]))
    return jnp.concatenate(outs, axis=1)


def _attn_a_finish(o, g_ref, x_ref, mod_ref, wo_ref, out_ref):
    z = (o * g_ref[0].astype(F32)).astype(BF16)
    out_ref[0] = x_ref[0] + mod_ref[0][2:3] * _dot(z, wo_ref[...])


def _attn_a_kernel(sink_ref, q_ref, kp_ref, kc_ref, kn_ref, vp_ref, vc_ref, vn_ref, kx_ref, vx_ref,
                   g_ref, x_ref, mod_ref, wo_ref, out_ref, *, tq, seq):
    i = pl.program_id(1)
    kd_loc = jnp.concatenate([kp_ref[0], kc_ref[0], kn_ref[0]], axis=0)
    vd_loc = jnp.concatenate([vp_ref[0], vc_ref[0], vn_ref[0]], axis=0)
    nk = tq + 2 * WINDOW
    group = A_HEADS // A_KV_HEADS
    row = lax.broadcasted_iota(jnp.int32, (group * tq, nk), 0) & (tq - 1)
    col = lax.broadcasted_iota(jnp.int32, (group * tq, nk), 1)
    kpos = i * tq - WINDOW + col
    ok = (jnp.abs(row + WINDOW - col) <= WINDOW) & (kpos >= 0) & (kpos < seq)
    o = _attn_a_core(sink_ref, q_ref[0], kd_loc, vd_loc, kx_ref[0], vx_ref[0], ok, tq)
    _attn_a_finish(o, g_ref, x_ref, mod_ref, wo_ref, out_ref)


def _attn_a_ctx_kernel(sink_ref, q_ref, kx_ref, vx_ref, g_ref, x_ref, mod_ref, wo_ref, out_ref, *, tq):
    o = _attn_a_core(sink_ref, q_ref[0], None, None, kx_ref[0], vx_ref[0], None, tq)
    _attn_a_finish(o, g_ref, x_ref, mod_ref, wo_ref, out_ref)


def _attn_a(sinks, q, kd, vd, kdc, vdc, gact, x, mod, wo, tq):
    bsz, s, d = x.shape
    c = kdc.shape[1]
    kdw = kd.shape[2]
    assert tq & (tq - 1) == 0 and tq % WINDOW == 0
    r = tq // WINDOW
    nblk = s // WINDOW
    tok = lambda w: pl.BlockSpec((1, tq, w), lambda b, i: (b, i, 0))
    prev = pl.BlockSpec((1, WINDOW, kdw), lambda b, i: (b, jnp.maximum(i * r - 1, 0), 0))
    nxt = pl.BlockSpec((1, WINDOW, kdw), lambda b, i: (b, jnp.minimum((i + 1) * r, nblk - 1), 0))
    ctx = pl.BlockSpec((1, c, kdw), lambda b, i: (b, 0, 0))
    return pl.pallas_call(
        functools.partial(_attn_a_kernel, tq=tq, seq=s),
        grid=(bsz, s // tq),
        in_specs=[pl.BlockSpec(memory_space=pltpu.SMEM), tok(A_WIDTH),
                  prev, tok(kdw), nxt, prev, tok(kdw), nxt, ctx, ctx,
                  tok(A_WIDTH), tok(d), pl.BlockSpec((1, 3, d), lambda b, i: (b, 0, 0)),
                  pl.BlockSpec(wo.shape, lambda b, i: (0, 0))],
        out_specs=tok(d),
        out_shape=jax.ShapeDtypeStruct((bsz, s, d), F32),
        compiler_params=pltpu.CompilerParams(
            dimension_semantics=("parallel", "parallel"), vmem_limit_bytes=VMEM_LIMIT),
        name="attn_a",
    )(sinks, q, kd, kd, kd, vd, vd, vd, kdc, vdc, gact, x, mod, wo)


def _attn_a_ctx(sinks, qc, kdc, vdc, gc, ctx, mod, wo):
    bsz, c, d = ctx.shape
    kdw = kdc.shape[2]
    tok = lambda w: pl.BlockSpec((1, c, w), lambda b: (b, 0, 0))
    return pl.pallas_call(
        functools.partial(_attn_a_ctx_kernel, tq=c),
        grid=(bsz,),
        in_specs=[pl.BlockSpec(memory_space=pltpu.SMEM), tok(A_WIDTH), tok(kdw), tok(kdw),
                  tok(A_WIDTH), tok(d), pl.BlockSpec((1, 3, d), lambda b: (b, 0, 0)),
                  pl.BlockSpec(wo.shape, lambda b: (0, 0))],
        out_specs=tok(d),
        out_shape=jax.ShapeDtypeStruct((bsz, c, d), F32),
        compiler_params=pltpu.CompilerParams(
            dimension_semantics=("parallel",), vmem_limit_bytes=VMEM_LIMIT),
        name="attn_a_ctx",
    )(sinks, qc, kdc, vdc, gc, ctx, mod, wo)


def _kext(a, kvg, c, sa, sb):
    ckv = _rms(a[:, B_Q_RANK:B_Q_RANK + B_KV_RANK], kvg)
    kr = a[:, B_Q_RANK + B_KV_RANK:]
    if c is not None:
        kr = _rope(kr, c, sa, sb, B_ROPE // 4)
    return jnp.concatenate([ckv, kr], axis=1).astype(BF16), ckv.T.astype(BF16)


def _proj_b_kernel(x_ref, mod_ref, ng_ref, w1_ref, wg_ref, qg_ref, wqn_ref, wqr_ref, wabs_ref,
                   kvg_ref, c_ref, sa_ref, sb_ref, qp_ref, k_ref, vt_ref, g_ref, *, qscale, tm):
    mod = mod_ref[0]
    h = (_rms(x_ref[0], ng_ref[...]) * (1.0 + mod[1:2]) + mod[0:1]).astype(BF16)
    c, sa, sb = c_ref[...], sa_ref[...], sb_ref[...]
    a = _dot(h, w1_ref[...])
    k_ref[0], vt_ref[0] = _kext(a, kvg_ref[...], c, sa, sb)
    g_ref[0] = _silu(_dot(h, wg_ref[...])).astype(BF16)
    cq = _rms(a[:, :B_Q_RANK], qg_ref[...]).astype(BF16)
    qn = _dot(cq, wqn_ref[...]).astype(BF16)
    qr = _dot(cq, wqr_ref[...])
    lane = lax.broadcasted_iota(jnp.int32, (tm, LANES), 1)
    heads_per_chunk = LANES // B_ROPE
    for cidx in range(B_HEADS * B_ROPE // LANES):
        qrc = _rope(qr[:, cidx * LANES:(cidx + 1) * LANES], c, sa, sb, B_ROPE // 4) * qscale
        for j in range(heads_per_chunk):
            hd = cidx * heads_per_chunk + j
            keep = (lane >= j * B_ROPE) & (lane < (j + 1) * B_ROPE)
            part = jnp.where(keep, qrc, 0.0).astype(BF16)
            for t in range(tm // LANES):
                qp_ref[0, t, hd, :, LANES:] = part[t * LANES:(t + 1) * LANES]
    for m in range(B_HEADS // 2):
        qa = (_dot(qn[:, m * LANES:(m + 1) * LANES], wabs_ref[m]) * qscale).astype(BF16)
        for half in range(2):
            for t in range(tm // LANES):
                qp_ref[0, t, 2 * m + half, :, :LANES] = (
                    qa[t * LANES:(t + 1) * LANES, half * LANES:(half + 1) * LANES])


def _proj_b(x, mod, norm_g, w1, wg, qg, wqn, wqr, wabs, kvg, tabs, tm):
    bsz, s, d = x.shape
    full = lambda a: pl.BlockSpec(a.shape, lambda b, t: (0,) * a.ndim)
    tab = pl.BlockSpec((tm, LANES), lambda b, t: (t, 0))
    tok = lambda w: pl.BlockSpec((1, tm, w), lambda b, t: (b, t, 0))
    nt = tm // LANES
    return pl.pallas_call(
        functools.partial(_proj_b_kernel, qscale=(B_NOPE + B_ROPE) ** -0.5 * LOG2E, tm=tm),
        grid=(bsz, s // tm),
        in_specs=[tok(d), pl.BlockSpec((1, 3, d), lambda b, t: (b, 0, 0)), full(norm_g),
                  full(w1), full(wg), full(qg), full(wqn), full(wqr), full(wabs), full(kvg),
                  tab, tab, tab],
        out_specs=[pl.BlockSpec((1, nt, B_HEADS, LANES, B_FEAT), lambda b, t: (b, t, 0, 0, 0)),
                   tok(B_FEAT), pl.BlockSpec((1, B_KV_RANK, tm), lambda b, t: (b, 0, t)),
                   tok(B_WIDTH)],
        out_shape=[jax.ShapeDtypeStruct((bsz, s // LANES, B_HEADS, LANES, B_FEAT), BF16),
                   jax.ShapeDtypeStruct((bsz, s, B_FEAT), BF16),
                   jax.ShapeDtypeStruct((bsz, B_KV_RANK, s), BF16),
                   jax.ShapeDtypeStruct((bsz, s, B_WIDTH), BF16)],
        compiler_params=pltpu.CompilerParams(
            dimension_semantics=("parallel", "parallel"), vmem_limit_bytes=VMEM_LIMIT),
        name="proj_b",
    )(x, mod, norm_g, w1, wg, qg, wqn, wqr, wabs, kvg, *tabs)


def _proj_b_ctx_kernel(x_ref, mod_ref, ng_ref, w1_ref, kvg_ref, k_ref, vt_ref):
    mod = mod_ref[0]
    h = (_rms(x_ref[0], ng_ref[...]) * (1.0 + mod[1:2]) + mod[0:1]).astype(BF16)
    k_ref[0], vt_ref[0] = _kext(_dot(h, w1_ref[...]), kvg_ref[...], None, None, None)


def _proj_b_ctx(ctx, mod, norm_g, w1, kvg):
    bsz, c, d = ctx.shape
    full = lambda a: pl.BlockSpec(a.shape, lambda b: (0,) * a.ndim)
    return pl.pallas_call(
        _proj_b_ctx_kernel,
        grid=(bsz,),
        in_specs=[pl.BlockSpec((1, c, d), lambda b: (b, 0, 0)),
                  pl.BlockSpec((1, 3, d), lambda b: (b, 0, 0)), full(norm_g), full(w1), full(kvg)],
        out_specs=[pl.BlockSpec((1, c, B_FEAT), lambda b: (b, 0, 0)),
                   pl.BlockSpec((1, B_KV_RANK, c), lambda b: (b, 0, 0))],
        out_shape=[jax.ShapeDtypeStruct((bsz, c, B_FEAT), BF16),
                   jax.ShapeDtypeStruct((bsz, B_KV_RANK, c), BF16)],
        compiler_params=pltpu.CompilerParams(dimension_semantics=("parallel",)),
        name="proj_b_ctx",
    )(ctx, mod, norm_g, w1, kvg)


def _flash_b_kernel(q_ref, k_ref, vt_ref, g_ref, x_ref, mod_ref, wvt_ref, wo_ref, fg_ref, out_ref,
                    m_ref, l_ref, acc_ref, s_ref, mx_ref, *, tq, tk, nk, hb):
    rt = hb * tq
    m_ref[...] = jnp.full(m_ref.shape, NEG_INF, F32)
    l_ref[...] = jnp.zeros(l_ref.shape, F32)
    acc_ref[...] = jnp.zeros(acc_ref.shape, F32)

    nsb = B_HEADS // hb
    nsteps = (nk // tk) * nsb
    unroll = 4
    assert nsb & (nsb - 1) == 0 and nsteps % unroll == 0
    shift = nsb.bit_length() - 1

    def split(k):
        return lax.shift_right_logical(k, shift), lax.bitwise_and(k, nsb - 1)

    def score_stage(k, slot):
        j, t = split(k)
        kc = k_ref[0, pl.ds(pl.multiple_of(j * tk, tk), tk), :]
        st = _dot_nt(kc, q_ref[0, pl.ds(t * hb, hb)].reshape(rt, B_FEAT))
        s_ref[slot] = st
        mx_ref[slot] = jnp.max(st.reshape(tk // 8, 8, rt), axis=0)

    def update_stage(k, slot):
        j, t = split(k)
        vt = vt_ref[0, :, pl.ds(pl.multiple_of(j * tk, tk), tk)]
        m_old = m_ref[t]
        m_new = jnp.maximum(m_old, jnp.max(mx_ref[slot], axis=0, keepdims=True))
        alpha = jnp.exp2(m_old - m_new)
        e = jnp.exp2(s_ref[slot] - m_new[0:1])
        l_ref[t] = alpha * l_ref[t] + jnp.sum(e.reshape(tk // 8, 8, rt), axis=0)
        acc_ref[t] = alpha[0:1] * acc_ref[t] + _dot(vt, e.astype(BF16))
        m_ref[t] = m_new

    score_stage(0, 0)

    def body(i, carry):
        for u in range(unroll):
            k = unroll * i + u
            score_stage(jnp.minimum(k + 1, nsteps - 1), (u + 1) % 2)
            update_stage(k, u % 2)
        return carry

    lax.fori_loop(0, nsteps // unroll, body, 0)
    outs = []
    for t in range(B_HEADS // hb):
        linv = 1.0 / jnp.sum(l_ref[t], axis=0, keepdims=True)
        ot = (acc_ref[t] * linv).astype(BF16)
        for p in range(hb // 2):
            pair = jnp.concatenate([ot[:, (2 * p) * tq:(2 * p + 1) * tq],
                                    ot[:, (2 * p + 1) * tq:(2 * p + 2) * tq]], axis=0)
            outs.append(_dot(wvt_ref[t * (hb // 2) + p], pair))
    o = jnp.concatenate(outs, axis=0).T
    z = (o * g_ref[0].astype(F32)).astype(BF16)
    xo = x_ref[0] + mod_ref[0][2:3] * _dot(z, wo_ref[...])
    out_ref[0] = _rms(xo, fg_ref[...])


def _flash_b(qp, kall, vtall, gact, x, mod, wvt, wo, final_g, tk, hb):
    bsz, s, d = x.shape
    nk = kall.shape[1]
    tq = LANES
    assert nk % tk == 0 and tk % LANES == 0 and hb % 2 == 0 and B_HEADS % hb == 0
    nt = B_HEADS // hb
    tok = lambda w: pl.BlockSpec((1, tq, w), lambda b, i: (b, i, 0))
    full = lambda a: pl.BlockSpec(a.shape, lambda b, i: (0,) * a.ndim)
    return pl.pallas_call(
        functools.partial(_flash_b_kernel, tq=tq, tk=tk, nk=nk, hb=hb),
        grid=(bsz, s // tq),
        in_specs=[pl.BlockSpec((1, None, B_HEADS, tq, B_FEAT), lambda b, i: (b, i, 0, 0, 0)),
                  pl.BlockSpec((1, nk, B_FEAT), lambda b, i: (b, 0, 0)),
                  pl.BlockSpec((1, B_KV_RANK, nk), lambda b, i: (b, 0, 0)),
                  tok(B_WIDTH), tok(d), pl.BlockSpec((1, 3, d), lambda b, i: (b, 0, 0)),
                  full(wvt), full(wo), full(final_g)],
        out_specs=tok(d),
        out_shape=jax.ShapeDtypeStruct((bsz, s, d), F32),
        scratch_shapes=[pltpu.VMEM((nt, 8, hb * tq), F32), pltpu.VMEM((nt, 8, hb * tq), F32),
                        pltpu.VMEM((nt, B_KV_RANK, hb * tq), F32),
                        pltpu.VMEM((2, tk, hb * tq), F32), pltpu.VMEM((2, 8, hb * tq), F32)],
        compiler_params=pltpu.CompilerParams(
            dimension_semantics=("parallel", "parallel"), vmem_limit_bytes=VMEM_LIMIT),
        name="flash_b",
    )(qp, kall, vtall, gact, x, mod, wvt, wo, final_g)


def _rope_tables(n_tokens, rot_dim, n_ctx):
    n_rows = n_tokens // GRID_W
    row = jnp.broadcast_to(jnp.arange(n_rows)[:, None], (n_rows, GRID_W)).reshape(-1)
    col = jnp.broadcast_to(jnp.arange(GRID_W)[None, :], (n_rows, GRID_W)).reshape(-1)
    nf = rot_dim // 4
    inv = ROPE_BASE ** (-jnp.arange(nf, dtype=F32) / nf)
    ar = row.astype(F32)[:, None] * inv
    ac = col.astype(F32)[:, None] * inv
    ang = jnp.concatenate([ar, ar, ac, ac], axis=-1)
    reps = LANES // rot_dim
    cos = jnp.tile(jnp.cos(ang), (1, reps))
    sin = jnp.tile(jnp.sin(ang), (1, reps))
    first = (jnp.arange(LANES) % (2 * nf)) < nf
    sa = jnp.where(first, -sin, 0.0)
    sb = jnp.where(first, 0.0, sin)
    pad = lambda t, v: jnp.concatenate([jnp.full((n_ctx, LANES), v, F32), t], axis=0)
    return pad(cos, 1.0), pad(sa, 0.0), pad(sb, 0.0)


def _split_mod(modout, bsz):
    d = modout.shape[1] // 3
    m = modout.reshape(8, 3, d)
    lat = m[:bsz]
    ctx = jnp.broadcast_to(m[bsz:bsz + 1], (bsz, 3, d))
    return lat, ctx


def kernel(x, c, ctx, c_ctx, norm_g_0, ada_w_0, ada_b_0, a_w_in_0, a_sinks_0, a_w_o_0, norm_g_1,
           ada_w_1, ada_b_1, b_w_in_1, b_q_norm_1, b_w_uq_1, b_kv_norm_1, b_w_ukv_1, b_w_o_1, final_g):
    bsz, s, d = x.shape
    n_ctx = ctx.shape[1]
    assert bsz + 1 <= 8
    cc = jnp.concatenate([c, c_ctx[None], jnp.zeros((8 - bsz - 1, d), F32)], axis=0)
    row = lambda g: g.reshape(1, -1)

    mod_x, mod_c = _split_mod(_modulation(cc, ada_w_0, ada_b_0), bsz)
    w = a_w_in_0.astype(BF16)
    wq, wk = w[:, :A_WIDTH], w[:, A_WIDTH:A_WIDTH + A_KV_WIDTH]
    wv, wg = w[:, A_WIDTH + A_KV_WIDTH:A_WIDTH + 2 * A_KV_WIDTH], w[:, A_WIDTH + 2 * A_KV_WIDTH:]
    tabs_a = _rope_tables(s, A_HEAD_DIM, n_ctx)
    tabs_ac = tuple(t[:n_ctx] for t in tabs_a)
    tabs_ax = tuple(t[n_ctx:] for t in tabs_a)
    wo_a = a_w_o_0.astype(BF16)
    q, kd, vd, gact = _proj_a(x, mod_x, row(norm_g_0), wq, wk, wv, wg, tabs_ax, tm=512)
    qc, kdc, vdc, gc = _proj_a(ctx, mod_c, row(norm_g_0), wq, wk, wv, wg, tabs_ac, tm=n_ctx)
    x1 = _attn_a(a_sinks_0, q, kd, vd, kdc, vdc, gact, x, mod_x, wo_a, tq=256)
    ctx1 = _attn_a_ctx(a_sinks_0, qc, kdc, vdc, gc, ctx, mod_c, wo_a)

    mod_x, mod_c = _split_mod(_modulation(cc, ada_w_1, ada_b_1), bsz)
    wb = b_w_in_1.astype(BF16)
    n_kv = B_Q_RANK + B_KV_RANK
    w1 = jnp.concatenate([wb[:, :n_kv]] + [wb[:, n_kv:n_kv + B_ROPE]] * (LANES // B_ROPE), axis=1)
    wgb = wb[:, n_kv + B_ROPE:]
    wuq = b_w_uq_1.astype(BF16).reshape(B_Q_RANK, B_HEADS, B_NOPE + B_ROPE)
    wqn = wuq[:, :, :B_NOPE].reshape(B_Q_RANK, B_HEADS * B_NOPE)
    wqr = wuq[:, :, B_NOPE:].reshape(B_Q_RANK, B_HEADS * B_ROPE)
    wukv = b_w_ukv_1.astype(BF16).reshape(B_KV_RANK, B_HEADS, B_NOPE + B_V)
    wkn_t = jnp.transpose(wukv[:, :, :B_NOPE], (1, 2, 0)).reshape(B_HEADS // 2, 2, B_NOPE, B_KV_RANK)
    zk = jnp.zeros_like(wkn_t[:, 0])
    wabs = jnp.concatenate([jnp.concatenate([wkn_t[:, 0], zk], axis=2),
                            jnp.concatenate([zk, wkn_t[:, 1]], axis=2)], axis=1)
    wvh = jnp.transpose(wukv[:, :, B_NOPE:], (1, 0, 2)).reshape(B_HEADS // 2, 2, B_KV_RANK, B_V)
    zv = jnp.zeros_like(wvh[:, 0])
    wvup_t = jnp.transpose(
        jnp.concatenate([jnp.concatenate([wvh[:, 0], zv], axis=2),
                         jnp.concatenate([zv, wvh[:, 1]], axis=2)], axis=1), (0, 2, 1))
    tabs_b = _rope_tables(s, B_ROPE, 0)
    qp, kx, vtx, gb = _proj_b(x1, mod_x, row(norm_g_1), w1, wgb, row(b_q_norm_1), wqn, wqr, wabs,
                              row(b_kv_norm_1), tabs_b, tm=256)
    kc, vtc = _proj_b_ctx(ctx1, mod_c, row(norm_g_1), w1, row(b_kv_norm_1))
    kall = jnp.concatenate([kc, kx], axis=1)
    vtall = jnp.concatenate([vtc, vtx], axis=2)
    return _flash_b(qp, kall, vtall, gb, x1, mod_x, wvup_t, b_w_o_1.astype(BF16), row(final_g),
                    tk=768, hb=4)
```

```python
import functools
import math

import jax
import jax.numpy as jnp
from jax import lax
from jax.experimental import pallas as pl
from jax.experimental.pallas import tpu as pltpu

F32 = jnp.float32
BF16 = jnp.bfloat16

GRID_W = 64
ROPE_BASE = 10000.0
EPS = 1e-6
NEG_INF = -1e30
LOG2E = 1.4426950408889634
LANES = 128
WINDOW = 128

A_HEADS, A_KV_HEADS, A_HEAD_DIM = 16, 4, 64
A_WIDTH = A_HEADS * A_HEAD_DIM
A_KV_WIDTH = A_KV_HEADS * A_HEAD_DIM
B_HEADS, B_NOPE, B_ROPE, B_V = 16, 64, 32, 64
B_Q_RANK, B_KV_RANK = 256, 128
B_WIDTH = B_HEADS * B_V
B_FEAT = 2 * LANES

VMEM_LIMIT = 56 * 1024 * 1024
MAX_UNROLL = 22


def _silu(x):
    return x * (1.0 / (1.0 + jnp.exp(-x)))


def _rms(x, g):
    return x * lax.rsqrt(jnp.mean(x * x, axis=-1, keepdims=True) + EPS) * g


def _rope(x, c, sa, sb, quarter):
    return (x * c + pltpu.roll(x, LANES - quarter, 1) * sa + pltpu.roll(x, quarter, 1) * sb)


def _dot(a, b):
    return jnp.dot(a, b, preferred_element_type=F32)


def _dot_nt(a, b):
    return lax.dot_general(a, b, (((1,), (1,)), ((), ())), preferred_element_type=F32)


def _mod_kernel(c_ref, w_ref, b_ref, o_ref):
    a = _silu(c_ref[...])
    o_ref[...] = jnp.dot(a, w_ref[...], precision=lax.Precision.HIGHEST,
                         preferred_element_type=F32) + b_ref[...]


def _modulation(cc, ada_w, ada_b):
    d = ada_w.shape[0]
    n = ada_w.shape[1]
    return pl.pallas_call(
        _mod_kernel,
        grid=(n // d,),
        in_specs=[pl.BlockSpec((8, d), lambda j: (0, 0)),
                  pl.BlockSpec((d, d), lambda j: (0, j)),
                  pl.BlockSpec((1, d), lambda j: (0, j))],
        out_specs=pl.BlockSpec((8, d), lambda j: (0, j)),
        out_shape=jax.ShapeDtypeStruct((8, n), F32),
        name="adaln_mod",
    )(cc, ada_w, ada_b.reshape(1, n))


def _proj_a_kernel(x_ref, mod_ref, ng_ref, wq_ref, wk_ref, wv_ref, wg_ref, c_ref, sa_ref, sb_ref,
                   q_ref, kd_ref, vt_ref, g_ref, *, qscale):
    mod = mod_ref[0]
    h = (_rms(x_ref[0], ng_ref[...]) * (1.0 + mod[1:2]) + mod[0:1]).astype(BF16)
    c, sa, sb = c_ref[...], sa_ref[...], sb_ref[...]
    quarter = A_HEAD_DIM // 4
    q = _dot(h, wq_ref[...])
    for j in range(A_WIDTH // LANES):
        sl = slice(j * LANES, (j + 1) * LANES)
        q_ref[0, :, sl] = (_rope(q[:, sl], c, sa, sb, quarter) * qscale).astype(BF16)
    lo = lax.broadcasted_iota(jnp.int32, c.shape, 1) < A_HEAD_DIM
    k = _dot(h, wk_ref[...])
    for m in range(A_KV_WIDTH // LANES):
        sl = slice(m * LANES, (m + 1) * LANES)
        kp = _rope(k[:, sl], c, sa, sb, quarter)
        ks = pltpu.roll(kp, A_HEAD_DIM, 1)
        kd_ref[0, :, 2 * m * LANES:(2 * m + 1) * LANES] = jnp.where(lo, kp, ks).astype(BF16)
        kd_ref[0, :, (2 * m + 1) * LANES:(2 * m + 2) * LANES] = jnp.where(lo, ks, kp).astype(BF16)
    vt_ref[0] = _dot(h, wv_ref[...]).astype(BF16).T
    g_ref[0] = _silu(_dot(h, wg_ref[...])).astype(BF16)


def _proj_a(x, mod, norm_g, wq, wk, wv, wg, tabs, tm):
    bsz, s, d = x.shape
    kdw = A_KV_HEADS * LANES
    full = lambda a: pl.BlockSpec(a.shape, lambda b, t: (0,) * a.ndim)
    tab = pl.BlockSpec((tm, LANES), lambda b, t: (t, 0))
    tok = lambda w: pl.BlockSpec((1, tm, w), lambda b, t: (b, t, 0))
    return pl.pallas_call(
        functools.partial(_proj_a_kernel, qscale=A_HEAD_DIM ** -0.5 * LOG2E),
        grid=(bsz, s // tm),
        in_specs=[tok(d), pl.BlockSpec((1, 3, d), lambda b, t: (b, 0, 0)), full(norm_g),
                  full(wq), full(wk), full(wv), full(wg), tab, tab, tab],
        out_specs=[tok(A_WIDTH), tok(kdw),
                   pl.BlockSpec((1, A_KV_WIDTH, tm), lambda b, t: (b, 0, t)), tok(A_WIDTH)],
        out_shape=[jax.ShapeDtypeStruct((bsz, s, A_WIDTH), BF16),
                   jax.ShapeDtypeStruct((bsz, s, kdw), BF16),
                   jax.ShapeDtypeStruct((bsz, A_KV_WIDTH, s), BF16),
                   jax.ShapeDtypeStruct((bsz, s, A_WIDTH), BF16)],
        compiler_params=pltpu.CompilerParams(
            dimension_semantics=("parallel", "parallel"), vmem_limit_bytes=VMEM_LIMIT),
        name="proj_a",
    )(x, mod, norm_g, wq, wk, wv, wg, *tabs)


def _attn_a_core(sink_ref, q, kd, vt, cap_loc, tq):
    group = A_HEADS // A_KV_HEADS
    nk = kd.shape[0]
    lane = lax.broadcasted_iota(jnp.int32, (1, 2 * tq), 1)
    klane = lax.broadcasted_iota(jnp.int32, (nk, LANES), 1)
    steps = [(g, par) for g in range(A_KV_HEADS) for par in range(2)]

    def scores(g, par):
        qrows = jnp.concatenate([q[:, (2 * g) * LANES:(2 * g + 1) * LANES],
                                 q[:, (2 * g + 1) * LANES:(2 * g + 2) * LANES]], axis=0)
        kg = kd[:, g * LANES:(g + 1) * LANES]
        keep = (klane < A_HEAD_DIM) if par == 0 else (klane >= A_HEAD_DIM)
        return _dot_nt(jnp.where(keep, kg, jnp.zeros_like(kg)), qrows)

    def finish(g, par, st):
        h0 = group * g + par
        sink = jnp.where(lane < tq, sink_ref[h0] * LOG2E, sink_ref[h0 + 2] * LOG2E)
        if cap_loc is not None:
            nl = cap_loc.shape[0]
            st = jnp.concatenate([jnp.minimum(st[:nl], cap_loc), st[nl:]], axis=0)
        m = jnp.maximum(sink, jnp.max(st, axis=0, keepdims=True))
        e = jnp.exp2(st - m)
        l = jnp.exp2(sink - m) + jnp.sum(e, axis=0, keepdims=True)
        ot = _dot(vt[g * A_HEAD_DIM:(g + 1) * A_HEAD_DIM], e.astype(BF16)) * (1.0 / l)
        return {h0: ot[:, :tq], h0 + 2: ot[:, tq:]}

    heads = {}
    cur = scores(*steps[0])
    for i, (g, par) in enumerate(steps):
        nxt = scores(*steps[i + 1]) if i + 1 < len(steps) else None
        heads.update(finish(g, par, cur))
        cur = nxt
    return jnp.concatenate([heads[h] for h in range(A_HEADS)], axis=0).T


def _attn_a_finish(o, g_ref, x_ref, mod_ref, wo_ref, out_ref):
    z = (o * g_ref[0].astype(F32)).astype(BF16)
    out_ref[0] = x_ref[0] + mod_ref[0][2:3] * _dot(z, wo_ref[...])


def _attn_a_kernel(sink_ref, q_ref, kp_ref, kc_ref, kn_ref, kx_ref, vp_ref, vc_ref, vn_ref, vx_ref,
                   g_ref, x_ref, mod_ref, wo_ref, out_ref, *, tq, seq):
    i = pl.program_id(1)
    kd = jnp.concatenate([kp_ref[0], kc_ref[0], kn_ref[0], kx_ref[0]], axis=0)
    vt = jnp.concatenate([vp_ref[0], vc_ref[0], vn_ref[0], vx_ref[0]], axis=1)
    nl = tq + 2 * WINDOW
    krow = lax.broadcasted_iota(jnp.int32, (nl, 2 * tq), 0)
    qcol = lax.broadcasted_iota(jnp.int32, (nl, 2 * tq), 1) & (tq - 1)
    kpos = i * tq - WINDOW + krow
    ok = (jnp.abs(qcol + WINDOW - krow) <= WINDOW) & (kpos >= 0) & (kpos < seq)
    o = _attn_a_core(sink_ref, q_ref[0], kd, vt, jnp.where(ok, jnp.inf, NEG_INF), tq)
    _attn_a_finish(o, g_ref, x_ref, mod_ref, wo_ref, out_ref)


def _attn_a_ctx_kernel(sink_ref, q_ref, kx_ref, vx_ref, g_ref, x_ref, mod_ref, wo_ref, out_ref, *, tq):
    o = _attn_a_core(sink_ref, q_ref[0], kx_ref[0], vx_ref[0], None, tq)
    _attn_a_finish(o, g_ref, x_ref, mod_ref, wo_ref, out_ref)


def _attn_a(sinks, q, kd, vt, kdc, vtc, gact, x, mod, wo, tq):
    bsz, s, d = x.shape
    c = kdc.shape[1]
    kdw = kd.shape[2]
    vw = vt.shape[1]
    assert tq & (tq - 1) == 0 and tq % WINDOW == 0
    r = tq // WINDOW
    nblk = s // WINDOW
    tok = lambda w: pl.BlockSpec((1, tq, w), lambda b, i: (b, i, 0))
    prev_i = lambda i: jnp.maximum(i * r - 1, 0)
    next_i = lambda i: jnp.minimum((i + 1) * r, nblk - 1)
    return pl.pallas_call(
        functools.partial(_attn_a_kernel, tq=tq, seq=s),
        grid=(bsz, s // tq),
        in_specs=[pl.BlockSpec(memory_space=pltpu.SMEM), tok(A_WIDTH),
                  pl.BlockSpec((1, WINDOW, kdw), lambda b, i: (b, prev_i(i), 0)), tok(kdw),
                  pl.BlockSpec((1, WINDOW, kdw), lambda b, i: (b, next_i(i), 0)),
                  pl.BlockSpec((1, c, kdw), lambda b, i: (b, 0, 0)),
                  pl.BlockSpec((1, vw, WINDOW), lambda b, i: (b, 0, prev_i(i))),
                  pl.BlockSpec((1, vw, tq), lambda b, i: (b, 0, i)),
                  pl.BlockSpec((1, vw, WINDOW), lambda b, i: (b, 0, next_i(i))),
                  pl.BlockSpec((1, vw, c), lambda b, i: (b, 0, 0)),
                  tok(A_WIDTH), tok(d), pl.BlockSpec((1, 3, d), lambda b, i: (b, 0, 0)),
                  pl.BlockSpec(wo.shape, lambda b, i: (0, 0))],
        out_specs=tok(d),
        out_shape=jax.ShapeDtypeStruct((bsz, s, d), F32),
        compiler_params=pltpu.CompilerParams(
            dimension_semantics=("parallel", "parallel"), vmem_limit_bytes=VMEM_LIMIT),
        name="attn_a",
    )(sinks, q, kd, kd, kd, kdc, vt, vt, vt, vtc, gact, x, mod, wo)


def _attn_a_ctx(sinks, qc, kdc, vtc, gc, ctx, mod, wo):
    bsz, c, d = ctx.shape
    tok = lambda w: pl.BlockSpec((1, c, w), lambda b: (b, 0, 0))
    return pl.pallas_call(
        functools.partial(_attn_a_ctx_kernel, tq=c),
        grid=(bsz,),
        in_specs=[pl.BlockSpec(memory_space=pltpu.SMEM), tok(A_WIDTH), tok(kdc.shape[2]),
                  pl.BlockSpec((1, vtc.shape[1], c), lambda b: (b, 0, 0)),
                  tok(A_WIDTH), tok(d), pl.BlockSpec((1, 3, d), lambda b: (b, 0, 0)),
                  pl.BlockSpec(wo.shape, lambda b: (0, 0))],
        out_specs=tok(d),
        out_shape=jax.ShapeDtypeStruct((bsz, c, d), F32),
        compiler_params=pltpu.CompilerParams(
            dimension_semantics=("parallel",), vmem_limit_bytes=VMEM_LIMIT),
        name="attn_a_ctx",
    )(sinks, qc, kdc, vtc, gc, ctx, mod, wo)


def _kext(a, kvg, c, sa, sb):
    ckv = _rms(a[:, B_Q_RANK:B_Q_RANK + B_KV_RANK], kvg)
    kr = a[:, B_Q_RANK + B_KV_RANK:]
    if c is not None:
        kr = _rope(kr, c, sa, sb, B_ROPE // 4)
    return jnp.concatenate([ckv, kr], axis=1).astype(BF16), ckv.T.astype(BF16)


def _proj_b_kernel(x_ref, mod_ref, ng_ref, w1_ref, wg_ref, qg_ref, wqn_ref, wqr_ref, wabs_ref,
                   kvg_ref, c_ref, sa_ref, sb_ref, qp_ref, k_ref, vt_ref, g_ref, *, qscale, tm):
    mod = mod_ref[0]
    h = (_rms(x_ref[0], ng_ref[...]) * (1.0 + mod[1:2]) + mod[0:1]).astype(BF16)
    c, sa, sb = c_ref[...], sa_ref[...], sb_ref[...]
    a = _dot(h, w1_ref[...])
    k_ref[0], vt_ref[0] = _kext(a, kvg_ref[...], c, sa, sb)
    g_ref[0] = _silu(_dot(h, wg_ref[...])).astype(BF16)
    cq = _rms(a[:, :B_Q_RANK], qg_ref[...]).astype(BF16)
    qn = _dot(cq, wqn_ref[...]).astype(BF16)
    qr = _dot(cq, wqr_ref[...])
    lane = lax.broadcasted_iota(jnp.int32, (tm, LANES), 1)
    heads_per_chunk = LANES // B_ROPE
    for cidx in range(B_HEADS * B_ROPE // LANES):
        qrc = _rope(qr[:, cidx * LANES:(cidx + 1) * LANES], c, sa, sb, B_ROPE // 4) * qscale
        for j in range(heads_per_chunk):
            hd = cidx * heads_per_chunk + j
            keep = (lane >= j * B_ROPE) & (lane < (j + 1) * B_ROPE)
            part = jnp.where(keep, qrc, 0.0).astype(BF16)
            for t in range(tm // LANES):
                qp_ref[0, t, hd, :, LANES:] = part[t * LANES:(t + 1) * LANES]
    for m in range(B_HEADS // 2):
        qa = (_dot(qn[:, m * LANES:(m + 1) * LANES], wabs_ref[m]) * qscale).astype(BF16)
        for half in range(2):
            for t in range(tm // LANES):
                qp_ref[0, t, 2 * m + half, :, :LANES] = (
                    qa[t * LANES:(t + 1) * LANES, half * LANES:(half + 1) * LANES])


def _proj_b(x, mod, norm_g, w1, wg, qg, wqn, wqr, wabs, kvg, tabs, tm):
    bsz, s, d = x.shape
    full = lambda a: pl.BlockSpec(a.shape, lambda b, t: (0,) * a.ndim)
    tab = pl.BlockSpec((tm, LANES), lambda b, t: (t, 0))
    tok = lambda w: pl.BlockSpec((1, tm, w), lambda b, t: (b, t, 0))
    nt = tm // LANES
    return pl.pallas_call(
        functools.partial(_proj_b_kernel, qscale=(B_NOPE + B_ROPE) ** -0.5 * LOG2E, tm=tm),
        grid=(bsz, s // tm),
        in_specs=[tok(d), pl.BlockSpec((1, 3, d), lambda b, t: (b, 0, 0)), full(norm_g),
                  full(w1), full(wg), full(qg), full(wqn), full(wqr), full(wabs), full(kvg),
                  tab, tab, tab],
        out_specs=[pl.BlockSpec((1, nt, B_HEADS, LANES, B_FEAT), lambda b, t: (b, t, 0, 0, 0)),
                   tok(B_FEAT), pl.BlockSpec((1, B_KV_RANK, tm), lambda b, t: (b, 0, t)),
                   tok(B_WIDTH)],
        out_shape=[jax.ShapeDtypeStruct((bsz, s // LANES, B_HEADS, LANES, B_FEAT), BF16),
                   jax.ShapeDtypeStruct((bsz, s, B_FEAT), BF16),
                   jax.ShapeDtypeStruct((bsz, B_KV_RANK, s), BF16),
                   jax.ShapeDtypeStruct((bsz, s, B_WIDTH), BF16)],
        compiler_params=pltpu.CompilerParams(
            dimension_semantics=("parallel", "parallel"), vmem_limit_bytes=VMEM_LIMIT),
        name="proj_b",
    )(x, mod, norm_g, w1, wg, qg, wqn, wqr, wabs, kvg, *tabs)


def _proj_b_ctx_kernel(x_ref, mod_ref, ng_ref, w1_ref, kvg_ref, k_ref, vt_ref):
    mod = mod_ref[0]
    h = (_rms(x_ref[0], ng_ref[...]) * (1.0 + mod[1:2]) + mod[0:1]).astype(BF16)
    k_ref[0], vt_ref[0] = _kext(_dot(h, w1_ref[...]), kvg_ref[...], None, None, None)


def _proj_b_ctx(ctx, mod, norm_g, w1, kvg):
    bsz, c, d = ctx.shape
    full = lambda a: pl.BlockSpec(a.shape, lambda b: (0,) * a.ndim)
    return pl.pallas_call(
        _proj_b_ctx_kernel,
        grid=(bsz,),
        in_specs=[pl.BlockSpec((1, c, d), lambda b: (b, 0, 0)),
                  pl.BlockSpec((1, 3, d), lambda b: (b, 0, 0)), full(norm_g), full(w1), full(kvg)],
        out_specs=[pl.BlockSpec((1, c, B_FEAT), lambda b: (b, 0, 0)),
                   pl.BlockSpec((1, B_KV_RANK, c), lambda b: (b, 0, 0))],
        out_shape=[jax.ShapeDtypeStruct((bsz, c, B_FEAT), BF16),
                   jax.ShapeDtypeStruct((bsz, B_KV_RANK, c), BF16)],
        compiler_params=pltpu.CompilerParams(dimension_semantics=("parallel",)),
        name="proj_b_ctx",
    )(ctx, mod, norm_g, w1, kvg)


def _flash_b_kernel(q_ref, k_ref, vt_ref, g_ref, x_ref, mod_ref, wvt_ref, wo_ref, fg_ref, out_ref,
                    m_ref, l_ref, acc_ref, s_ref, mx_ref, *, tq, tk, nk, hb):
    rt = hb * tq
    m_ref[...] = jnp.full(m_ref.shape, NEG_INF, F32)
    l_ref[...] = jnp.zeros(l_ref.shape, F32)
    acc_ref[...] = jnp.zeros(acc_ref.shape, F32)

    nsb = B_HEADS // hb
    nsteps = (nk // tk) * nsb
    unroll = max(u for u in range(2, MAX_UNROLL + 1, 2) if nsteps % u == 0)
    assert nsb & (nsb - 1) == 0
    shift = nsb.bit_length() - 1

    def split(k):
        return lax.shift_right_logical(k, shift), lax.bitwise_and(k, nsb - 1)

    def score_stage(k, slot):
        j, t = split(k)
        kc = k_ref[0, pl.ds(pl.multiple_of(j * tk, tk), tk), :]
        st = _dot_nt(kc, q_ref[0, pl.ds(t * hb, hb)].reshape(rt, B_FEAT))
        s_ref[slot] = st
        mx_ref[slot] = jnp.max(st.reshape(tk // 8, 8, rt), axis=0)

    def update_stage(k, slot):
        j, t = split(k)
        vt = vt_ref[0, :, pl.ds(pl.multiple_of(j * tk, tk), tk)]
        m_old = m_ref[t]
        m_new = jnp.maximum(m_old, jnp.max(mx_ref[slot], axis=0, keepdims=True))
        alpha = jnp.exp2(m_old - m_new)
        e = jnp.exp2(s_ref[slot] - m_new[0:1])
        l_ref[t] = alpha * l_ref[t] + jnp.sum(e.reshape(tk // 8, 8, rt), axis=0)
        acc_ref[t] = alpha[0:1] * acc_ref[t] + _dot(vt, e.astype(BF16))
        m_ref[t] = m_new

    score_stage(0, 0)

    def body(i, carry):
        for u in range(unroll):
            k = unroll * i + u
            score_stage(jnp.minimum(k + 1, nsteps - 1), (u + 1) % 2)
            update_stage(k, u % 2)
        return carry

    lax.fori_loop(0, nsteps // unroll, body, 0)
    outs = []
    for t in range(B_HEADS // hb):
        linv = 1.0 / jnp.sum(l_ref[t], axis=0, keepdims=True)
        ot = (acc_ref[t] * linv).astype(BF16)
        for p in range(hb // 2):
            pair = jnp.concatenate([ot[:, (2 * p) * tq:(2 * p + 1) * tq],
                                    ot[:, (2 * p + 1) * tq:(2 * p + 2) * tq]], axis=0)
            outs.append(_dot(wvt_ref[t * (hb // 2) + p], pair))
    o = jnp.concatenate(outs, axis=0).T
    z = (o * g_ref[0].astype(F32)).astype(BF16)
    xo = x_ref[0] + mod_ref[0][2:3] * _dot(z, wo_ref[...])
    out_ref[0] = _rms(xo, fg_ref[...])


def _flash_b(qp, kall, vtall, gact, x, mod, wvt, wo, final_g, tk, hb):
    bsz, s, d = x.shape
    nk = kall.shape[1]
    tq = LANES
    assert nk % tk == 0 and tk % LANES == 0 and hb % 2 == 0 and B_HEADS % hb == 0
    nt = B_HEADS // hb
    tok = lambda w: pl.BlockSpec((1, tq, w), lambda b, i: (b, i, 0))
    full = lambda a: pl.BlockSpec(a.shape, lambda b, i: (0,) * a.ndim)
    return pl.pallas_call(
        functools.partial(_flash_b_kernel, tq=tq, tk=tk, nk=nk, hb=hb),
        grid=(bsz, s // tq),
        in_specs=[pl.BlockSpec((1, None, B_HEADS, tq, B_FEAT), lambda b, i: (b, i, 0, 0, 0)),
                  pl.BlockSpec((1, nk, B_FEAT), lambda b, i: (b, 0, 0)),
                  pl.BlockSpec((1, B_KV_RANK, nk), lambda b, i: (b, 0, 0)),
                  tok(B_WIDTH), tok(d), pl.BlockSpec((1, 3, d), lambda b, i: (b, 0, 0)),
                  full(wvt), full(wo), full(final_g)],
        out_specs=tok(d),
        out_shape=jax.ShapeDtypeStruct((bsz, s, d), F32),
        scratch_shapes=[pltpu.VMEM((nt, 8, hb * tq), F32), pltpu.VMEM((nt, 8, hb * tq), F32),
                        pltpu.VMEM((nt, B_KV_RANK, hb * tq), F32),
                        pltpu.VMEM((2, tk, hb * tq), F32), pltpu.VMEM((2, 8, hb * tq), F32)],
        compiler_params=pltpu.CompilerParams(
            dimension_semantics=("parallel", "parallel"), vmem_limit_bytes=VMEM_LIMIT),
        name="flash_b",
    )(qp, kall, vtall, gact, x, mod, wvt, wo, final_g)


def _rope_tables(n_tokens, rot_dim, n_ctx):
    n_rows = n_tokens // GRID_W
    row = jnp.broadcast_to(jnp.arange(n_rows)[:, None], (n_rows, GRID_W)).reshape(-1)
    col = jnp.broadcast_to(jnp.arange(GRID_W)[None, :], (n_rows, GRID_W)).reshape(-1)
    nf = rot_dim // 4
    inv = ROPE_BASE ** (-jnp.arange(nf, dtype=F32) / nf)
    ar = row.astype(F32)[:, None] * inv
    ac = col.astype(F32)[:, None] * inv
    ang = jnp.concatenate([ar, ar, ac, ac], axis=-1)
    reps = LANES // rot_dim
    cos = jnp.tile(jnp.cos(ang), (1, reps))
    sin = jnp.tile(jnp.sin(ang), (1, reps))
    first = (jnp.arange(LANES) % (2 * nf)) < nf
    sa = jnp.where(first, -sin, 0.0)
    sb = jnp.where(first, 0.0, sin)
    pad = lambda t, v: jnp.concatenate([jnp.full((n_ctx, LANES), v, F32), t], axis=0)
    return pad(cos, 1.0), pad(sa, 0.0), pad(sb, 0.0)


def _split_mod(modout, bsz):
    d = modout.shape[1] // 3
    m = modout.reshape(8, 3, d)
    lat = m[:bsz]
    ctx = jnp.broadcast_to(m[bsz:bsz + 1], (bsz, 3, d))
    return lat, ctx


def kernel(x, c, ctx, c_ctx, norm_g_0, ada_w_0, ada_b_0, a_w_in_0, a_sinks_0, a_w_o_0, norm_g_1,
           ada_w_1, ada_b_1, b_w_in_1, b_q_norm_1, b_w_uq_1, b_kv_norm_1, b_w_ukv_1, b_w_o_1, final_g):
    bsz, s, d = x.shape
    n_ctx = ctx.shape[1]
    assert bsz + 1 <= 8
    cc = jnp.concatenate([c, c_ctx[None], jnp.zeros((8 - bsz - 1, d), F32)], axis=0)
    row = lambda g: g.reshape(1, -1)

    mod_x, mod_c = _split_mod(_modulation(cc, ada_w_0, ada_b_0), bsz)
    w = a_w_in_0.astype(BF16)
    wq, wk = w[:, :A_WIDTH], w[:, A_WIDTH:A_WIDTH + A_KV_WIDTH]
    wv, wg = w[:, A_WIDTH + A_KV_WIDTH:A_WIDTH + 2 * A_KV_WIDTH], w[:, A_WIDTH + 2 * A_KV_WIDTH:]
    tabs_a = _rope_tables(s, A_HEAD_DIM, n_ctx)
    tabs_ac = tuple(t[:n_ctx] for t in tabs_a)
    tabs_ax = tuple(t[n_ctx:] for t in tabs_a)
    wo_a = a_w_o_0.astype(BF16)
    q, kd, vt, gact = _proj_a(x, mod_x, row(norm_g_0), wq, wk, wv, wg, tabs_ax, tm=512)
    qc, kdc, vtc, gc = _proj_a(ctx, mod_c, row(norm_g_0), wq, wk, wv, wg, tabs_ac, tm=n_ctx)
    x1 = _attn_a(a_sinks_0, q, kd, vt, kdc, vtc, gact, x, mod_x, wo_a, tq=256)
    ctx1 = _attn_a_ctx(a_sinks_0, qc, kdc, vtc, gc, ctx, mod_c, wo_a)

    mod_x, mod_c = _split_mod(_modulation(cc, ada_w_1, ada_b_1), bsz)
    wb = b_w_in_1.astype(BF16)
    n_kv = B_Q_RANK + B_KV_RANK
    w1 = jnp.concatenate([wb[:, :n_kv]] + [wb[:, n_kv:n_kv + B_ROPE]] * (LANES // B_ROPE), axis=1)
    wgb = wb[:, n_kv + B_ROPE:]
    wuq = b_w_uq_1.astype(BF16).reshape(B_Q_RANK, B_HEADS, B_NOPE + B_ROPE)
    wqn = wuq[:, :, :B_NOPE].reshape(B_Q_RANK, B_HEADS * B_NOPE)
    wqr = wuq[:, :, B_NOPE:].reshape(B_Q_RANK, B_HEADS * B_ROPE)
    wukv = b_w_ukv_1.astype(BF16).reshape(B_KV_RANK, B_HEADS, B_NOPE + B_V)
    wkn_t = jnp.transpose(wukv[:, :, :B_NOPE], (1, 2, 0)).reshape(B_HEADS // 2, 2, B_NOPE, B_KV_RANK)
    zk = jnp.zeros_like(wkn_t[:, 0])
    wabs = jnp.concatenate([jnp.concatenate([wkn_t[:, 0], zk], axis=2),
                            jnp.concatenate([zk, wkn_t[:, 1]], axis=2)], axis=1)
    wvh = jnp.transpose(wukv[:, :, B_NOPE:], (1, 0, 2)).reshape(B_HEADS // 2, 2, B_KV_RANK, B_V)
    zv = jnp.zeros_like(wvh[:, 0])
    wvup_t = jnp.transpose(
        jnp.concatenate([jnp.concatenate([wvh[:, 0], zv], axis=2),
                         jnp.concatenate([zv, wvh[:, 1]], axis=2)], axis=1), (0, 2, 1))
    tabs_b = _rope_tables(s, B_ROPE, 0)
    qp, kx, vtx, gb = _proj_b(x1, mod_x, row(norm_g_1), w1, wgb, row(b_q_norm_1), wqn, wqr, wabs,
                              row(b_kv_norm_1), tabs_b, tm=256)
    kc, vtc = _proj_b_ctx(ctx1, mod_c, row(norm_g_1), w1, row(b_kv_norm_1))
    kall = jnp.concatenate([kc, kx], axis=1)
    vtall = jnp.concatenate([vtc, vtx], axis=2)
    return _flash_b(qp, kall, vtall, gb, x1, mod_x, wvup_t, b_w_o_1.astype(BF16), row(final_g),
                    tk=768, hb=4)
```

```python
import functools
import math

import jax
import jax.numpy as jnp
from jax import lax
from jax.experimental import pallas as pl
from jax.experimental.pallas import tpu as pltpu

F32 = jnp.float32
BF16 = jnp.bfloat16

GRID_W = 64
ROPE_BASE = 10000.0
EPS = 1e-6
NEG_INF = -1e30
LOG2E = 1.4426950408889634
LANES = 128
WINDOW = 128

A_HEADS, A_KV_HEADS, A_HEAD_DIM = 16, 4, 64
A_WIDTH = A_HEADS * A_HEAD_DIM
A_KV_WIDTH = A_KV_HEADS * A_HEAD_DIM
B_HEADS, B_NOPE, B_ROPE, B_V = 16, 64, 32, 64
B_Q_RANK, B_KV_RANK = 256, 128
B_WIDTH = B_HEADS * B_V
B_FEAT = 2 * LANES

VMEM_LIMIT = 56 * 1024 * 1024
MAX_UNROLL = 22


def _silu(x):
    return x * (1.0 / (1.0 + jnp.exp(-x)))


def _rms(x, g):
    return x * lax.rsqrt(jnp.mean(x * x, axis=-1, keepdims=True) + EPS) * g


def _rope(x, c, sa, sb, quarter):
    return (x * c + pltpu.roll(x, LANES - quarter, 1) * sa + pltpu.roll(x, quarter, 1) * sb)


def _dot(a, b):
    return jnp.dot(a, b, preferred_element_type=F32)


def _dot_nt(a, b):
    return lax.dot_general(a, b, (((1,), (1,)), ((), ())), preferred_element_type=F32)


def _mod_kernel(c_ref, w_ref, b_ref, o_ref):
    a = _silu(c_ref[...])
    o_ref[...] = jnp.dot(a, w_ref[...], precision=lax.Precision.HIGHEST,
                         preferred_element_type=F32) + b_ref[...]


def _modulation(cc, ada_w, ada_b):
    d = ada_w.shape[0]
    n = ada_w.shape[1]
    return pl.pallas_call(
        _mod_kernel,
        grid=(n // d,),
        in_specs=[pl.BlockSpec((8, d), lambda j: (0, 0)),
                  pl.BlockSpec((d, d), lambda j: (0, j)),
                  pl.BlockSpec((1, d), lambda j: (0, j))],
        out_specs=pl.BlockSpec((8, d), lambda j: (0, j)),
        out_shape=jax.ShapeDtypeStruct((8, n), F32),
        name="adaln_mod",
    )(cc, ada_w, ada_b.reshape(1, n))


def _proj_a_kernel(x_ref, mod_ref, ng_ref, wq_ref, wk_ref, wv_ref, wg_ref, c_ref, sa_ref, sb_ref,
                   q_ref, kd_ref, vt_ref, g_ref, *, qscale):
    mod = mod_ref[0]
    h = (_rms(x_ref[0], ng_ref[...]) * (1.0 + mod[1:2]) + mod[0:1]).astype(BF16)
    c, sa, sb = c_ref[...], sa_ref[...], sb_ref[...]
    quarter = A_HEAD_DIM // 4
    q = _dot(h, wq_ref[...])
    for j in range(A_WIDTH // LANES):
        sl = slice(j * LANES, (j + 1) * LANES)
        q_ref[0, :, sl] = (_rope(q[:, sl], c, sa, sb, quarter) * qscale).astype(BF16)
    lo = lax.broadcasted_iota(jnp.int32, c.shape, 1) < A_HEAD_DIM
    k = _dot(h, wk_ref[...])
    for m in range(A_KV_WIDTH // LANES):
        sl = slice(m * LANES, (m + 1) * LANES)
        kp = _rope(k[:, sl], c, sa, sb, quarter)
        ks = pltpu.roll(kp, A_HEAD_DIM, 1)
        kd_ref[0, :, 2 * m * LANES:(2 * m + 1) * LANES] = jnp.where(lo, kp, ks).astype(BF16)
        kd_ref[0, :, (2 * m + 1) * LANES:(2 * m + 2) * LANES] = jnp.where(lo, ks, kp).astype(BF16)
    vt_ref[0] = _dot(h, wv_ref[...]).astype(BF16).T
    g_ref[0] = _silu(_dot(h, wg_ref[...])).astype(BF16)


def _proj_a(x, mod, norm_g, wq, wk, wv, wg, tabs, tm):
    bsz, s, d = x.shape
    kdw = A_KV_HEADS * LANES
    full = lambda a: pl.BlockSpec(a.shape, lambda b, t: (0,) * a.ndim)
    tab = pl.BlockSpec((tm, LANES), lambda b, t: (t, 0))
    tok = lambda w: pl.BlockSpec((1, tm, w), lambda b, t: (b, t, 0))
    return pl.pallas_call(
        functools.partial(_proj_a_kernel, qscale=A_HEAD_DIM ** -0.5 * LOG2E),
        grid=(bsz, s // tm),
        in_specs=[tok(d), pl.BlockSpec((1, 3, d), lambda b, t: (b, 0, 0)), full(norm_g),
                  full(wq), full(wk), full(wv), full(wg), tab, tab, tab],
        out_specs=[tok(A_WIDTH), tok(kdw),
                   pl.BlockSpec((1, A_KV_WIDTH, tm), lambda b, t: (b, 0, t)), tok(A_WIDTH)],
        out_shape=[jax.ShapeDtypeStruct((bsz, s, A_WIDTH), BF16),
                   jax.ShapeDtypeStruct((bsz, s, kdw), BF16),
                   jax.ShapeDtypeStruct((bsz, A_KV_WIDTH, s), BF16),
                   jax.ShapeDtypeStruct((bsz, s, A_WIDTH), BF16)],
        compiler_params=pltpu.CompilerParams(
            dimension_semantics=("parallel", "parallel"), vmem_limit_bytes=VMEM_LIMIT),
        name="proj_a",
    )(x, mod, norm_g, wq, wk, wv, wg, *tabs)


def _attn_a_core(sink_ref, q, kd, vt, cap_loc, tq):
    group = A_HEADS // A_KV_HEADS
    nk = kd.shape[0]
    lane = lax.broadcasted_iota(jnp.int32, (1, 2 * tq), 1)
    klane = lax.broadcasted_iota(jnp.int32, (nk, LANES), 1)
    steps = [(g, par) for g in range(A_KV_HEADS) for par in range(2)]

    def scores(g, par):
        qrows = jnp.concatenate([q[:, (2 * g) * LANES:(2 * g + 1) * LANES],
                                 q[:, (2 * g + 1) * LANES:(2 * g + 2) * LANES]], axis=0)
        kg = kd[:, g * LANES:(g + 1) * LANES]
        keep = (klane < A_HEAD_DIM) if par == 0 else (klane >= A_HEAD_DIM)
        return _dot_nt(jnp.where(keep, kg, jnp.zeros_like(kg)), qrows)

    def finish(g, par, st):
        h0 = group * g + par
        sink = jnp.where(lane < tq, sink_ref[h0] * LOG2E, sink_ref[h0 + 2] * LOG2E)
        if cap_loc is not None:
            nl = cap_loc.shape[0]
            st = jnp.concatenate([jnp.minimum(st[:nl], cap_loc), st[nl:]], axis=0)
        m = jnp.maximum(sink, jnp.max(st, axis=0, keepdims=True))
        e = jnp.exp2(st - m)
        l = jnp.exp2(sink - m) + jnp.sum(e, axis=0, keepdims=True)
        ot = _dot(vt[g * A_HEAD_DIM:(g + 1) * A_HEAD_DIM], e.astype(BF16)) * (1.0 / l)
        return {h0: ot[:, :tq], h0 + 2: ot[:, tq:]}

    heads = {}
    cur = scores(*steps[0])
    for i, (g, par) in enumerate(steps):
        nxt = scores(*steps[i + 1]) if i + 1 < len(steps) else None
        heads.update(finish(g, par, cur))
        cur = nxt
    return jnp.concatenate([heads[h] for h in range(A_HEADS)], axis=0).T


def _attn_a_finish(o, g_ref, x_ref, mod_ref, wo_ref, out_ref):
    z = (o * g_ref[0].astype(F32)).astype(BF16)
    out_ref[0] = x_ref[0] + mod_ref[0][2:3] * _dot(z, wo_ref[...])


def _attn_a_kernel(sink_ref, q_ref, kp_ref, kc_ref, kn_ref, kx_ref, vp_ref, vc_ref, vn_ref, vx_ref,
                   g_ref, x_ref, mod_ref, wo_ref, out_ref, *, tq, seq):
    i = pl.program_id(1)
    kd = jnp.concatenate([kp_ref[0], kc_ref[0], kn_ref[0], kx_ref[0]], axis=0)
    vt = jnp.concatenate([vp_ref[0], vc_ref[0], vn_ref[0], vx_ref[0]], axis=1)
    nl = tq + 2 * WINDOW
    krow = lax.broadcasted_iota(jnp.int32, (nl, 2 * tq), 0)
    qcol = lax.broadcasted_iota(jnp.int32, (nl, 2 * tq), 1) & (tq - 1)
    kpos = i * tq - WINDOW + krow
    ok = (jnp.abs(qcol + WINDOW - krow) <= WINDOW) & (kpos >= 0) & (kpos < seq)
    o = _attn_a_core(sink_ref, q_ref[0], kd, vt, jnp.where(ok, jnp.inf, NEG_INF), tq)
    _attn_a_finish(o, g_ref, x_ref, mod_ref, wo_ref, out_ref)


def _attn_a_ctx_kernel(sink_ref, q_ref, kx_ref, vx_ref, g_ref, x_ref, mod_ref, wo_ref, out_ref, *, tq):
    o = _attn_a_core(sink_ref, q_ref[0], kx_ref[0], vx_ref[0], None, tq)
    _attn_a_finish(o, g_ref, x_ref, mod_ref, wo_ref, out_ref)


def _attn_a(sinks, q, kd, vt, kdc, vtc, gact, x, mod, wo, tq):
    bsz, s, d = x.shape
    c = kdc.shape[1]
    kdw = kd.shape[2]
    vw = vt.shape[1]
    assert tq & (tq - 1) == 0 and tq % WINDOW == 0
    r = tq // WINDOW
    nblk = s // WINDOW
    tok = lambda w: pl.BlockSpec((1, tq, w), lambda b, i: (b, i, 0))
    prev_i = lambda i: jnp.maximum(i * r - 1, 0)
    next_i = lambda i: jnp.minimum((i + 1) * r, nblk - 1)
    return pl.pallas_call(
        functools.partial(_attn_a_kernel, tq=tq, seq=s),
        grid=(bsz, s // tq),
        in_specs=[pl.BlockSpec(memory_space=pltpu.SMEM), tok(A_WIDTH),
                  pl.BlockSpec((1, WINDOW, kdw), lambda b, i: (b, prev_i(i), 0)), tok(kdw),
                  pl.BlockSpec((1, WINDOW, kdw), lambda b, i: (b, next_i(i), 0)),
                  pl.BlockSpec((1, c, kdw), lambda b, i: (b, 0, 0)),
                  pl.BlockSpec((1, vw, WINDOW), lambda b, i: (b, 0, prev_i(i))),
                  pl.BlockSpec((1, vw, tq), lambda b, i: (b, 0, i)),
                  pl.BlockSpec((1, vw, WINDOW), lambda b, i: (b, 0, next_i(i))),
                  pl.BlockSpec((1, vw, c), lambda b, i: (b, 0, 0)),
                  tok(A_WIDTH), tok(d), pl.BlockSpec((1, 3, d), lambda b, i: (b, 0, 0)),
                  pl.BlockSpec(wo.shape, lambda b, i: (0, 0))],
        out_specs=tok(d),
        out_shape=jax.ShapeDtypeStruct((bsz, s, d), F32),
        compiler_params=pltpu.CompilerParams(
            dimension_semantics=("parallel", "parallel"), vmem_limit_bytes=VMEM_LIMIT),
        name="attn_a",
    )(sinks, q, kd, kd, kd, kdc, vt, vt, vt, vtc, gact, x, mod, wo)


def _attn_a_ctx(sinks, qc, kdc, vtc, gc, ctx, mod, wo):
    bsz, c, d = ctx.shape
    tok = lambda w: pl.BlockSpec((1, c, w), lambda b: (b, 0, 0))
    return pl.pallas_call(
        functools.partial(_attn_a_ctx_kernel, tq=c),
        grid=(bsz,),
        in_specs=[pl.BlockSpec(memory_space=pltpu.SMEM), tok(A_WIDTH), tok(kdc.shape[2]),
                  pl.BlockSpec((1, vtc.shape[1], c), lambda b: (b, 0, 0)),
                  tok(A_WIDTH), tok(d), pl.BlockSpec((1, 3, d), lambda b: (b, 0, 0)),
                  pl.BlockSpec(wo.shape, lambda b: (0, 0))],
        out_specs=tok(d),
        out_shape=jax.ShapeDtypeStruct((bsz, c, d), F32),
        compiler_params=pltpu.CompilerParams(
            dimension_semantics=("parallel",), vmem_limit_bytes=VMEM_LIMIT),
        name="attn_a_ctx",
    )(sinks, qc, kdc, vtc, gc, ctx, mod, wo)


def _kext(a, kvg, c, sa, sb):
    ckv = _rms(a[:, B_Q_RANK:B_Q_RANK + B_KV_RANK], kvg)
    kr = a[:, B_Q_RANK + B_KV_RANK:]
    if c is not None:
        kr = _rope(kr, c, sa, sb, B_ROPE // 4)
    return jnp.concatenate([ckv, kr], axis=1).astype(BF16), ckv.T.astype(BF16)


def _proj_b_kernel(x_ref, mod_ref, ng_ref, w1_ref, wg_ref, qg_ref, wqn_ref, wqr_ref, wabs_ref,
                   kvg_ref, c_ref, sa_ref, sb_ref, qt_ref, k_ref, vt_ref, g_ref, *, qscale, tm):
    mod = mod_ref[0]
    h = (_rms(x_ref[0], ng_ref[...]) * (1.0 + mod[1:2]) + mod[0:1]).astype(BF16)
    c, sa, sb = c_ref[...], sa_ref[...], sb_ref[...]
    a = _dot(h, w1_ref[...])
    k_ref[0], vt_ref[0] = _kext(a, kvg_ref[...], c, sa, sb)
    g_ref[0] = _silu(_dot(h, wg_ref[...])).astype(BF16)
    cq = _rms(a[:, :B_Q_RANK], qg_ref[...]).astype(BF16)
    qn = _dot(cq, wqn_ref[...]).astype(BF16)
    qr = _dot(cq, wqr_ref[...])
    lane = lax.broadcasted_iota(jnp.int32, (tm, LANES), 1)
    heads_per_chunk = LANES // B_ROPE
    for cidx in range(B_HEADS * B_ROPE // LANES):
        qrc = _rope(qr[:, cidx * LANES:(cidx + 1) * LANES], c, sa, sb, B_ROPE // 4) * qscale
        for j in range(heads_per_chunk):
            hd = cidx * heads_per_chunk + j
            keep = (lane >= j * B_ROPE) & (lane < (j + 1) * B_ROPE)
            part_t = jnp.where(keep, qrc, 0.0).astype(BF16).T
            for t in range(tm // LANES):
                qt_ref[0, t, LANES:, hd * LANES:(hd + 1) * LANES] = part_t[:, t * LANES:(t + 1) * LANES]
    for m in range(B_HEADS // 2):
        qa = (_dot(qn[:, m * LANES:(m + 1) * LANES], wabs_ref[m]) * qscale).astype(BF16)
        for half in range(2):
            hd = 2 * m + half
            qa_t = qa[:, half * LANES:(half + 1) * LANES].T
            for t in range(tm // LANES):
                qt_ref[0, t, :LANES, hd * LANES:(hd + 1) * LANES] = qa_t[:, t * LANES:(t + 1) * LANES]


def _proj_b(x, mod, norm_g, w1, wg, qg, wqn, wqr, wabs, kvg, tabs, tm):
    bsz, s, d = x.shape
    full = lambda a: pl.BlockSpec(a.shape, lambda b, t: (0,) * a.ndim)
    tab = pl.BlockSpec((tm, LANES), lambda b, t: (t, 0))
    tok = lambda w: pl.BlockSpec((1, tm, w), lambda b, t: (b, t, 0))
    nt = tm // LANES
    return pl.pallas_call(
        functools.partial(_proj_b_kernel, qscale=(B_NOPE + B_ROPE) ** -0.5 * LOG2E, tm=tm),
        grid=(bsz, s // tm),
        in_specs=[tok(d), pl.BlockSpec((1, 3, d), lambda b, t: (b, 0, 0)), full(norm_g),
                  full(w1), full(wg), full(qg), full(wqn), full(wqr), full(wabs), full(kvg),
                  tab, tab, tab],
        out_specs=[pl.BlockSpec((1, nt, B_FEAT, B_HEADS * LANES), lambda b, t: (b, t, 0, 0)),
                   tok(B_FEAT), pl.BlockSpec((1, B_KV_RANK, tm), lambda b, t: (b, 0, t)),
                   tok(B_WIDTH)],
        out_shape=[jax.ShapeDtypeStruct((bsz, s // LANES, B_FEAT, B_HEADS * LANES), BF16),
                   jax.ShapeDtypeStruct((bsz, s, B_FEAT), BF16),
                   jax.ShapeDtypeStruct((bsz, B_KV_RANK, s), BF16),
                   jax.ShapeDtypeStruct((bsz, s, B_WIDTH), BF16)],
        compiler_params=pltpu.CompilerParams(
            dimension_semantics=("parallel", "parallel"), vmem_limit_bytes=VMEM_LIMIT),
        name="proj_b",
    )(x, mod, norm_g, w1, wg, qg, wqn, wqr, wabs, kvg, *tabs)


def _proj_b_ctx_kernel(x_ref, mod_ref, ng_ref, w1_ref, kvg_ref, k_ref, vt_ref):
    mod = mod_ref[0]
    h = (_rms(x_ref[0], ng_ref[...]) * (1.0 + mod[1:2]) + mod[0:1]).astype(BF16)
    k_ref[0], vt_ref[0] = _kext(_dot(h, w1_ref[...]), kvg_ref[...], None, None, None)


def _proj_b_ctx(ctx, mod, norm_g, w1, kvg):
    bsz, c, d = ctx.shape
    full = lambda a: pl.BlockSpec(a.shape, lambda b: (0,) * a.ndim)
    return pl.pallas_call(
        _proj_b_ctx_kernel,
        grid=(bsz,),
        in_specs=[pl.BlockSpec((1, c, d), lambda b: (b, 0, 0)),
                  pl.BlockSpec((1, 3, d), lambda b: (b, 0, 0)), full(norm_g), full(w1), full(kvg)],
        out_specs=[pl.BlockSpec((1, c, B_FEAT), lambda b: (b, 0, 0)),
                   pl.BlockSpec((1, B_KV_RANK, c), lambda b: (b, 0, 0))],
        out_shape=[jax.ShapeDtypeStruct((bsz, c, B_FEAT), BF16),
                   jax.ShapeDtypeStruct((bsz, B_KV_RANK, c), BF16)],
        compiler_params=pltpu.CompilerParams(dimension_semantics=("parallel",)),
        name="proj_b_ctx",
    )(ctx, mod, norm_g, w1, kvg)


def _flash_b_kernel(q_ref, k_ref, vt_ref, g_ref, x_ref, mod_ref, wvt_ref, wo_ref, fg_ref, out_ref,
                    m_ref, l_ref, acc_ref, s_ref, mx_ref, *, tq, tk, nk, hb):
    rt = hb * tq
    m_ref[...] = jnp.full(m_ref.shape, NEG_INF, F32)
    l_ref[...] = jnp.zeros(l_ref.shape, F32)
    acc_ref[...] = jnp.zeros(acc_ref.shape, F32)

    nsb = B_HEADS // hb
    nsteps = (nk // tk) * nsb
    unroll = max(u for u in range(2, MAX_UNROLL + 1, 2) if nsteps % u == 0)
    assert nsb & (nsb - 1) == 0
    shift = nsb.bit_length() - 1

    def split(k):
        return lax.shift_right_logical(k, shift), lax.bitwise_and(k, nsb - 1)

    def score_stage(k, slot):
        j, t = split(k)
        kc = k_ref[0, pl.ds(pl.multiple_of(j * tk, tk), tk), :]
        st = _dot(kc, q_ref[0, :, pl.ds(pl.multiple_of(t * rt, rt), rt)])
        s_ref[slot] = st
        mx_ref[slot] = jnp.max(st.reshape(tk // 8, 8, rt), axis=0)

    def update_stage(k, slot):
        j, t = split(k)
        vt = vt_ref[0, :, pl.ds(pl.multiple_of(j * tk, tk), tk)]
        m_old = m_ref[t]
        m_new = jnp.maximum(m_old, jnp.max(mx_ref[slot], axis=0, keepdims=True))
        alpha = jnp.exp2(m_old - m_new)
        e = jnp.exp2(s_ref[slot] - m_new[0:1])
        l_ref[t] = alpha * l_ref[t] + jnp.sum(e.reshape(tk // 8, 8, rt), axis=0)
        acc_ref[t] = alpha[0:1] * acc_ref[t] + _dot(vt, e.astype(BF16))
        m_ref[t] = m_new

    score_stage(0, 0)

    def body(i, carry):
        for u in range(unroll):
            k = unroll * i + u
            score_stage(jnp.minimum(k + 1, nsteps - 1), (u + 1) % 2)
            update_stage(k, u % 2)
        return carry

    lax.fori_loop(0, nsteps // unroll, body, 0)
    outs = []
    for t in range(B_HEADS // hb):
        linv = 1.0 / jnp.sum(l_ref[t], axis=0, keepdims=True)
        ot = (acc_ref[t] * linv).astype(BF16)
        for p in range(hb // 2):
            pair = jnp.concatenate([ot[:, (2 * p) * tq:(2 * p + 1) * tq],
                                    ot[:, (2 * p + 1) * tq:(2 * p + 2) * tq]], axis=0)
            outs.append(_dot(wvt_ref[t * (hb // 2) + p], pair))
    o = jnp.concatenate(outs, axis=0).T
    z = (o * g_ref[0].astype(F32)).astype(BF16)
    xo = x_ref[0] + mod_ref[0][2:3] * _dot(z, wo_ref[...])
    out_ref[0] = _rms(xo, fg_ref[...])


def _flash_b(qp, kall, vtall, gact, x, mod, wvt, wo, final_g, max_tk, hb):
    bsz, s, d = x.shape
    nk = kall.shape[1]
    tq = LANES
    tk = max(t for t in range(LANES, max_tk + 1, LANES) if nk % t == 0)
    assert hb % 2 == 0 and B_HEADS % hb == 0
    nt = B_HEADS // hb
    tok = lambda w: pl.BlockSpec((1, tq, w), lambda b, i: (b, i, 0))
    full = lambda a: pl.BlockSpec(a.shape, lambda b, i: (0,) * a.ndim)
    return pl.pallas_call(
        functools.partial(_flash_b_kernel, tq=tq, tk=tk, nk=nk, hb=hb),
        grid=(bsz, s // tq),
        in_specs=[pl.BlockSpec((1, None, B_FEAT, B_HEADS * tq), lambda b, i: (b, i, 0, 0)),
                  pl.BlockSpec((1, nk, B_FEAT), lambda b, i: (b, 0, 0)),
                  pl.BlockSpec((1, B_KV_RANK, nk), lambda b, i: (b, 0, 0)),
                  tok(B_WIDTH), tok(d), pl.BlockSpec((1, 3, d), lambda b, i: (b, 0, 0)),
                  full(wvt), full(wo), full(final_g)],
        out_specs=tok(d),
        out_shape=jax.ShapeDtypeStruct((bsz, s, d), F32),
        scratch_shapes=[pltpu.VMEM((nt, 8, hb * tq), F32), pltpu.VMEM((nt, 8, hb * tq), F32),
                        pltpu.VMEM((nt, B_KV_RANK, hb * tq), F32),
                        pltpu.VMEM((2, tk, hb * tq), F32), pltpu.VMEM((2, 8, hb * tq), F32)],
        compiler_params=pltpu.CompilerParams(
            dimension_semantics=("parallel", "parallel"), vmem_limit_bytes=VMEM_LIMIT),
        name="flash_b",
    )(qp, kall, vtall, gact, x, mod, wvt, wo, final_g)


def _rope_tables(n_tokens, rot_dim, n_ctx):
    n_rows = n_tokens // GRID_W
    row = jnp.broadcast_to(jnp.arange(n_rows)[:, None], (n_rows, GRID_W)).reshape(-1)
    col = jnp.broadcast_to(jnp.arange(GRID_W)[None, :], (n_rows, GRID_W)).reshape(-1)
    nf = rot_dim // 4
    inv = ROPE_BASE ** (-jnp.arange(nf, dtype=F32) / nf)
    ar = row.astype(F32)[:, None] * inv
    ac = col.astype(F32)[:, None] * inv
    ang = jnp.concatenate([ar, ar, ac, ac], axis=-1)
    reps = LANES // rot_dim
    cos = jnp.tile(jnp.cos(ang), (1, reps))
    sin = jnp.tile(jnp.sin(ang), (1, reps))
    first = (jnp.arange(LANES) % (2 * nf)) < nf
    sa = jnp.where(first, -sin, 0.0)
    sb = jnp.where(first, 0.0, sin)
    pad = lambda t, v: jnp.concatenate([jnp.full((n_ctx, LANES), v, F32), t], axis=0)
    return pad(cos, 1.0), pad(sa, 0.0), pad(sb, 0.0)


def _split_mod(modout, bsz):
    d = modout.shape[1] // 3
    m = modout.reshape(8, 3, d)
    lat = m[:bsz]
    ctx = jnp.broadcast_to(m[bsz:bsz + 1], (bsz, 3, d))
    return lat, ctx


def kernel(x, c, ctx, c_ctx, norm_g_0, ada_w_0, ada_b_0, a_w_in_0, a_sinks_0, a_w_o_0, norm_g_1,
           ada_w_1, ada_b_1, b_w_in_1, b_q_norm_1, b_w_uq_1, b_kv_norm_1, b_w_ukv_1, b_w_o_1, final_g):
    bsz, s, d = x.shape
    n_ctx = ctx.shape[1]
    assert bsz + 1 <= 8
    cc = jnp.concatenate([c, c_ctx[None], jnp.zeros((8 - bsz - 1, d), F32)], axis=0)
    row = lambda g: g.reshape(1, -1)

    mod_x, mod_c = _split_mod(_modulation(cc, ada_w_0, ada_b_0), bsz)
    w = a_w_in_0.astype(BF16)
    wq, wk = w[:, :A_WIDTH], w[:, A_WIDTH:A_WIDTH + A_KV_WIDTH]
    wv, wg = w[:, A_WIDTH + A_KV_WIDTH:A_WIDTH + 2 * A_KV_WIDTH], w[:, A_WIDTH + 2 * A_KV_WIDTH:]
    tabs_a = _rope_tables(s, A_HEAD_DIM, n_ctx)
    tabs_ac = tuple(t[:n_ctx] for t in tabs_a)
    tabs_ax = tuple(t[n_ctx:] for t in tabs_a)
    wo_a = a_w_o_0.astype(BF16)
    q, kd, vt, gact = _proj_a(x, mod_x, row(norm_g_0), wq, wk, wv, wg, tabs_ax, tm=512)
    qc, kdc, vtc, gc = _proj_a(ctx, mod_c, row(norm_g_0), wq, wk, wv, wg, tabs_ac, tm=n_ctx)
    x1 = _attn_a(a_sinks_0, q, kd, vt, kdc, vtc, gact, x, mod_x, wo_a, tq=256)
    ctx1 = _attn_a_ctx(a_sinks_0, qc, kdc, vtc, gc, ctx, mod_c, wo_a)

    mod_x, mod_c = _split_mod(_modulation(cc, ada_w_1, ada_b_1), bsz)
    wb = b_w_in_1.astype(BF16)
    n_kv = B_Q_RANK + B_KV_RANK
    w1 = jnp.concatenate([wb[:, :n_kv]] + [wb[:, n_kv:n_kv + B_ROPE]] * (LANES // B_ROPE), axis=1)
    wgb = wb[:, n_kv + B_ROPE:]
    wuq = b_w_uq_1.astype(BF16).reshape(B_Q_RANK, B_HEADS, B_NOPE + B_ROPE)
    wqn = wuq[:, :, :B_NOPE].reshape(B_Q_RANK, B_HEADS * B_NOPE)
    wqr = wuq[:, :, B_NOPE:].reshape(B_Q_RANK, B_HEADS * B_ROPE)
    wukv = b_w_ukv_1.astype(BF16).reshape(B_KV_RANK, B_HEADS, B_NOPE + B_V)
    wkn_t = jnp.transpose(wukv[:, :, :B_NOPE], (1, 2, 0)).reshape(B_HEADS // 2, 2, B_NOPE, B_KV_RANK)
    zk = jnp.zeros_like(wkn_t[:, 0])
    wabs = jnp.concatenate([jnp.concatenate([wkn_t[:, 0], zk], axis=2),
                            jnp.concatenate([zk, wkn_t[:, 1]], axis=2)], axis=1)
    wvh = jnp.transpose(wukv[:, :, B_NOPE:], (1, 0, 2)).reshape(B_HEADS // 2, 2, B_KV_RANK, B_V)
    zv = jnp.zeros_like(wvh[:, 0])
    wvup_t = jnp.transpose(
        jnp.concatenate([jnp.concatenate([wvh[:, 0], zv], axis=2),
                         jnp.concatenate([zv, wvh[:, 1]], axis=2)], axis=1), (0, 2, 1))
    tabs_b = _rope_tables(s, B_ROPE, 0)
    qp, kx, vtx, gb = _proj_b(x1, mod_x, row(norm_g_1), w1, wgb, row(b_q_norm_1), wqn, wqr, wabs,
                              row(b_kv_norm_1), tabs_b, tm=256)
    kc, vtc = _proj_b_ctx(ctx1, mod_c, row(norm_g_1), w1, row(b_kv_norm_1))
    kall = jnp.concatenate([kc, kx], axis=1)
    vtall = jnp.concatenate([vtc, vtx], axis=2)
    return _flash_b(qp, kall, vtall, gb, x1, mod_x, wvup_t, b_w_o_1.astype(BF16), row(final_g),
                    max_tk=1408, hb=4)
```

```python
import functools
import math

import jax
import jax.numpy as jnp
from jax import lax
from jax.experimental import pallas as pl
from jax.experimental.pallas import tpu as pltpu

F32 = jnp.float32
BF16 = jnp.bfloat16

GRID_W = 64
ROPE_BASE = 10000.0
EPS = 1e-6
NEG_INF = -1e30
LOG2E = 1.4426950408889634
LANES = 128
WINDOW = 128

A_HEADS, A_KV_HEADS, A_HEAD_DIM = 16, 4, 64
A_WIDTH = A_HEADS * A_HEAD_DIM
A_KV_WIDTH = A_KV_HEADS * A_HEAD_DIM
B_HEADS, B_NOPE, B_ROPE, B_V = 16, 64, 32, 64
B_Q_RANK, B_KV_RANK = 256, 128
B_WIDTH = B_HEADS * B_V
B_FEAT = 2 * LANES

VMEM_LIMIT = 56 * 1024 * 1024
MAX_UNROLL = 24


def _silu(x):
    return x * (1.0 / (1.0 + jnp.exp(-x)))


def _rms(x, g):
    return x * lax.rsqrt(jnp.mean(x * x, axis=-1, keepdims=True) + EPS) * g


def _rope(x, c, sa, sb, quarter):
    return (x * c + pltpu.roll(x, LANES - quarter, 1) * sa + pltpu.roll(x, quarter, 1) * sb)


def _dot(a, b):
    return jnp.dot(a, b, preferred_element_type=F32)


def _dot_nt(a, b):
    return lax.dot_general(a, b, (((1,), (1,)), ((), ())), preferred_element_type=F32)


def _mod_kernel(c_ref, w_ref, b_ref, o_ref):
    a = _silu(c_ref[...])
    o_ref[...] = jnp.dot(a, w_ref[...], precision=lax.Precision.HIGHEST,
                         preferred_element_type=F32) + b_ref[...]


def _modulation(cc, ada_w, ada_b):
    d = ada_w.shape[0]
    n = ada_w.shape[1]
    return pl.pallas_call(
        _mod_kernel,
        grid=(n // d,),
        in_specs=[pl.BlockSpec((8, d), lambda j: (0, 0)),
                  pl.BlockSpec((d, d), lambda j: (0, j)),
                  pl.BlockSpec((1, d), lambda j: (0, j))],
        out_specs=pl.BlockSpec((8, d), lambda j: (0, j)),
        out_shape=jax.ShapeDtypeStruct((8, n), F32),
        name="adaln_mod",
    )(cc, ada_w, ada_b.reshape(1, n))


def _proj_a_kernel(x_ref, mod_ref, ng_ref, wq_ref, wk_ref, wv_ref, wg_ref, c_ref, sa_ref, sb_ref,
                   q_ref, kd_ref, vt_ref, g_ref, *, qscale):
    mod = mod_ref[0]
    h = (_rms(x_ref[0], ng_ref[...]) * (1.0 + mod[1:2]) + mod[0:1]).astype(BF16)
    c, sa, sb = c_ref[...], sa_ref[...], sb_ref[...]
    quarter = A_HEAD_DIM // 4
    q = _dot(h, wq_ref[...])
    for j in range(A_WIDTH // LANES):
        sl = slice(j * LANES, (j + 1) * LANES)
        q_ref[0, :, sl] = (_rope(q[:, sl], c, sa, sb, quarter) * qscale).astype(BF16)
    lo = lax.broadcasted_iota(jnp.int32, c.shape, 1) < A_HEAD_DIM
    k = _dot(h, wk_ref[...])
    for m in range(A_KV_WIDTH // LANES):
        sl = slice(m * LANES, (m + 1) * LANES)
        kp = _rope(k[:, sl], c, sa, sb, quarter)
        ks = pltpu.roll(kp, A_HEAD_DIM, 1)
        kd_ref[0, :, 2 * m * LANES:(2 * m + 1) * LANES] = jnp.where(lo, kp, ks).astype(BF16)
        kd_ref[0, :, (2 * m + 1) * LANES:(2 * m + 2) * LANES] = jnp.where(lo, ks, kp).astype(BF16)
    vt_ref[0] = _dot(h, wv_ref[...]).astype(BF16).T
    g_ref[0] = _silu(_dot(h, wg_ref[...])).astype(BF16)


def _proj_a(x, mod, norm_g, wq, wk, wv, wg, tabs, tm):
    bsz, s, d = x.shape
    kdw = A_KV_HEADS * LANES
    full = lambda a: pl.BlockSpec(a.shape, lambda b, t: (0,) * a.ndim)
    tab = pl.BlockSpec((tm, LANES), lambda b, t: (t, 0))
    tok = lambda w: pl.BlockSpec((1, tm, w), lambda b, t: (b, t, 0))
    return pl.pallas_call(
        functools.partial(_proj_a_kernel, qscale=A_HEAD_DIM ** -0.5 * LOG2E),
        grid=(bsz, s // tm),
        in_specs=[tok(d), pl.BlockSpec((1, 3, d), lambda b, t: (b, 0, 0)), full(norm_g),
                  full(wq), full(wk), full(wv), full(wg), tab, tab, tab],
        out_specs=[tok(A_WIDTH), tok(kdw),
                   pl.BlockSpec((1, A_KV_WIDTH, tm), lambda b, t: (b, 0, t)), tok(A_WIDTH)],
        out_shape=[jax.ShapeDtypeStruct((bsz, s, A_WIDTH), BF16),
                   jax.ShapeDtypeStruct((bsz, s, kdw), BF16),
                   jax.ShapeDtypeStruct((bsz, A_KV_WIDTH, s), BF16),
                   jax.ShapeDtypeStruct((bsz, s, A_WIDTH), BF16)],
        compiler_params=pltpu.CompilerParams(
            dimension_semantics=("parallel", "parallel"), vmem_limit_bytes=VMEM_LIMIT),
        name="proj_a",
    )(x, mod, norm_g, wq, wk, wv, wg, *tabs)


def _attn_a_core(sink_ref, q, kd, vt, cap_loc, tq):
    group = A_HEADS // A_KV_HEADS
    nk = kd.shape[0]
    lane = lax.broadcasted_iota(jnp.int32, (1, 2 * tq), 1)
    klane = lax.broadcasted_iota(jnp.int32, (nk, LANES), 1)
    steps = [(g, par) for g in range(A_KV_HEADS) for par in range(2)]

    def scores(g, par):
        qrows = jnp.concatenate([q[:, (2 * g) * LANES:(2 * g + 1) * LANES],
                                 q[:, (2 * g + 1) * LANES:(2 * g + 2) * LANES]], axis=0)
        kg = kd[:, g * LANES:(g + 1) * LANES]
        keep = (klane < A_HEAD_DIM) if par == 0 else (klane >= A_HEAD_DIM)
        return _dot_nt(jnp.where(keep, kg, jnp.zeros_like(kg)), qrows)

    def finish(g, par, st):
        h0 = group * g + par
        sink = jnp.where(lane < tq, sink_ref[h0] * LOG2E, sink_ref[h0 + 2] * LOG2E)
        if cap_loc is not None:
            nl = cap_loc.shape[0]
            st = jnp.concatenate([jnp.minimum(st[:nl], cap_loc), st[nl:]], axis=0)
        m = jnp.maximum(sink, jnp.max(st, axis=0, keepdims=True))
        e = jnp.exp2(st - m)
        l = jnp.exp2(sink - m) + jnp.sum(e, axis=0, keepdims=True)
        ot = _dot(vt[g * A_HEAD_DIM:(g + 1) * A_HEAD_DIM], e.astype(BF16)) * (1.0 / l)
        return {h0: ot[:, :tq], h0 + 2: ot[:, tq:]}

    heads = {}
    cur = scores(*steps[0])
    for i, (g, par) in enumerate(steps):
        nxt = scores(*steps[i + 1]) if i + 1 < len(steps) else None
        heads.update(finish(g, par, cur))
        cur = nxt
    return jnp.concatenate([heads[h] for h in range(A_HEADS)], axis=0).T


def _attn_a_finish(o, g_ref, x_ref, mod_ref, wo_ref, out_ref):
    z = (o * g_ref[0].astype(F32)).astype(BF16)
    out_ref[0] = x_ref[0] + mod_ref[0][2:3] * _dot(z, wo_ref[...])


def _attn_a_kernel(sink_ref, q_ref, kp_ref, kc_ref, kn_ref, kx_ref, vp_ref, vc_ref, vn_ref, vx_ref,
                   g_ref, x_ref, mod_ref, wo_ref, out_ref, *, tq, seq):
    i = pl.program_id(1)
    kd = jnp.concatenate([kp_ref[0], kc_ref[0], kn_ref[0], kx_ref[0]], axis=0)
    vt = jnp.concatenate([vp_ref[0], vc_ref[0], vn_ref[0], vx_ref[0]], axis=1)
    nl = tq + 2 * WINDOW
    krow = lax.broadcasted_iota(jnp.int32, (nl, 2 * tq), 0)
    qcol = lax.broadcasted_iota(jnp.int32, (nl, 2 * tq), 1) & (tq - 1)
    kpos = i * tq - WINDOW + krow
    ok = (jnp.abs(qcol + WINDOW - krow) <= WINDOW) & (kpos >= 0) & (kpos < seq)
    o = _attn_a_core(sink_ref, q_ref[0], kd, vt, jnp.where(ok, jnp.inf, NEG_INF), tq)
    _attn_a_finish(o, g_ref, x_ref, mod_ref, wo_ref, out_ref)


def _attn_a_ctx_kernel(sink_ref, q_ref, kx_ref, vx_ref, g_ref, x_ref, mod_ref, wo_ref, out_ref, *, tq):
    o = _attn_a_core(sink_ref, q_ref[0], kx_ref[0], vx_ref[0], None, tq)
    _attn_a_finish(o, g_ref, x_ref, mod_ref, wo_ref, out_ref)


def _attn_a(sinks, q, kd, vt, kdc, vtc, gact, x, mod, wo, tq):
    bsz, s, d = x.shape
    c = kdc.shape[1]
    kdw = kd.shape[2]
    vw = vt.shape[1]
    assert tq & (tq - 1) == 0 and tq % WINDOW == 0
    r = tq // WINDOW
    nblk = s // WINDOW
    tok = lambda w: pl.BlockSpec((1, tq, w), lambda b, i: (b, i, 0))
    prev_i = lambda i: jnp.maximum(i * r - 1, 0)
    next_i = lambda i: jnp.minimum((i + 1) * r, nblk - 1)
    return pl.pallas_call(
        functools.partial(_attn_a_kernel, tq=tq, seq=s),
        grid=(bsz, s // tq),
        in_specs=[pl.BlockSpec(memory_space=pltpu.SMEM), tok(A_WIDTH),
                  pl.BlockSpec((1, WINDOW, kdw), lambda b, i: (b, prev_i(i), 0)), tok(kdw),
                  pl.BlockSpec((1, WINDOW, kdw), lambda b, i: (b, next_i(i), 0)),
                  pl.BlockSpec((1, c, kdw), lambda b, i: (b, 0, 0)),
                  pl.BlockSpec((1, vw, WINDOW), lambda b, i: (b, 0, prev_i(i))),
                  pl.BlockSpec((1, vw, tq), lambda b, i: (b, 0, i)),
                  pl.BlockSpec((1, vw, WINDOW), lambda b, i: (b, 0, next_i(i))),
                  pl.BlockSpec((1, vw, c), lambda b, i: (b, 0, 0)),
                  tok(A_WIDTH), tok(d), pl.BlockSpec((1, 3, d), lambda b, i: (b, 0, 0)),
                  pl.BlockSpec(wo.shape, lambda b, i: (0, 0))],
        out_specs=tok(d),
        out_shape=jax.ShapeDtypeStruct((bsz, s, d), F32),
        compiler_params=pltpu.CompilerParams(
            dimension_semantics=("parallel", "parallel"), vmem_limit_bytes=VMEM_LIMIT),
        name="attn_a",
    )(sinks, q, kd, kd, kd, kdc, vt, vt, vt, vtc, gact, x, mod, wo)


def _attn_a_ctx(sinks, qc, kdc, vtc, gc, ctx, mod, wo):
    bsz, c, d = ctx.shape
    tok = lambda w: pl.BlockSpec((1, c, w), lambda b: (b, 0, 0))
    return pl.pallas_call(
        functools.partial(_attn_a_ctx_kernel, tq=c),
        grid=(bsz,),
        in_specs=[pl.BlockSpec(memory_space=pltpu.SMEM), tok(A_WIDTH), tok(kdc.shape[2]),
                  pl.BlockSpec((1, vtc.shape[1], c), lambda b: (b, 0, 0)),
                  tok(A_WIDTH), tok(d), pl.BlockSpec((1, 3, d), lambda b: (b, 0, 0)),
                  pl.BlockSpec(wo.shape, lambda b: (0, 0))],
        out_specs=tok(d),
        out_shape=jax.ShapeDtypeStruct((bsz, c, d), F32),
        compiler_params=pltpu.CompilerParams(
            dimension_semantics=("parallel",), vmem_limit_bytes=VMEM_LIMIT),
        name="attn_a_ctx",
    )(sinks, qc, kdc, vtc, gc, ctx, mod, wo)


def _kext(a, kvg, c, sa, sb):
    ckv = _rms(a[:, B_Q_RANK:B_Q_RANK + B_KV_RANK], kvg)
    kr = a[:, B_Q_RANK + B_KV_RANK:]
    if c is not None:
        kr = _rope(kr, c, sa, sb, B_ROPE // 4)
    return jnp.concatenate([ckv, kr], axis=1).astype(BF16), ckv.T.astype(BF16)


def _proj_b_kernel(x_ref, mod_ref, ng_ref, w1_ref, wg_ref, qg_ref, wqn_ref, wqr_ref, wabs_ref,
                   kvg_ref, c_ref, sa_ref, sb_ref, qt_ref, k_ref, vt_ref, g_ref, *, qscale, tm):
    mod = mod_ref[0]
    h = (_rms(x_ref[0], ng_ref[...]) * (1.0 + mod[1:2]) + mod[0:1]).astype(BF16)
    c, sa, sb = c_ref[...], sa_ref[...], sb_ref[...]
    a = _dot(h, w1_ref[...])
    k_ref[0], vt_ref[0] = _kext(a, kvg_ref[...], c, sa, sb)
    g_ref[0] = _silu(_dot(h, wg_ref[...])).astype(BF16)
    cq = _rms(a[:, :B_Q_RANK], qg_ref[...]).astype(BF16)
    qn = _dot(cq, wqn_ref[...]).astype(BF16)
    qr = _dot(cq, wqr_ref[...])
    lane = lax.broadcasted_iota(jnp.int32, (tm, LANES), 1)
    heads_per_chunk = LANES // B_ROPE
    for cidx in range(B_HEADS * B_ROPE // LANES):
        qrc = _rope(qr[:, cidx * LANES:(cidx + 1) * LANES], c, sa, sb, B_ROPE // 4) * qscale
        for j in range(heads_per_chunk):
            hd = cidx * heads_per_chunk + j
            keep = (lane >= j * B_ROPE) & (lane < (j + 1) * B_ROPE)
            part_t = jnp.where(keep, qrc, 0.0).astype(BF16).T
            for t in range(tm // LANES):
                qt_ref[0, t, LANES:, hd * LANES:(hd + 1) * LANES] = part_t[:, t * LANES:(t + 1) * LANES]
    for m in range(B_HEADS // 2):
        qa = (_dot(qn[:, m * LANES:(m + 1) * LANES], wabs_ref[m]) * qscale).astype(BF16)
        for half in range(2):
            hd = 2 * m + half
            qa_t = qa[:, half * LANES:(half + 1) * LANES].T
            for t in range(tm // LANES):
                qt_ref[0, t, :LANES, hd * LANES:(hd + 1) * LANES] = qa_t[:, t * LANES:(t + 1) * LANES]


def _proj_b(x, mod, norm_g, w1, wg, qg, wqn, wqr, wabs, kvg, tabs, tm):
    bsz, s, d = x.shape
    full = lambda a: pl.BlockSpec(a.shape, lambda b, t: (0,) * a.ndim)
    tab = pl.BlockSpec((tm, LANES), lambda b, t: (t, 0))
    tok = lambda w: pl.BlockSpec((1, tm, w), lambda b, t: (b, t, 0))
    nt = tm // LANES
    return pl.pallas_call(
        functools.partial(_proj_b_kernel, qscale=(B_NOPE + B_ROPE) ** -0.5 * LOG2E, tm=tm),
        grid=(bsz, s // tm),
        in_specs=[tok(d), pl.BlockSpec((1, 3, d), lambda b, t: (b, 0, 0)), full(norm_g),
                  full(w1), full(wg), full(qg), full(wqn), full(wqr), full(wabs), full(kvg),
                  tab, tab, tab],
        out_specs=[pl.BlockSpec((1, nt, B_FEAT, B_HEADS * LANES), lambda b, t: (b, t, 0, 0)),
                   tok(B_FEAT), pl.BlockSpec((1, B_KV_RANK, tm), lambda b, t: (b, 0, t)),
                   tok(B_WIDTH)],
        out_shape=[jax.ShapeDtypeStruct((bsz, s // LANES, B_FEAT, B_HEADS * LANES), BF16),
                   jax.ShapeDtypeStruct((bsz, s, B_FEAT), BF16),
                   jax.ShapeDtypeStruct((bsz, B_KV_RANK, s), BF16),
                   jax.ShapeDtypeStruct((bsz, s, B_WIDTH), BF16)],
        compiler_params=pltpu.CompilerParams(
            dimension_semantics=("parallel", "parallel"), vmem_limit_bytes=VMEM_LIMIT),
        name="proj_b",
    )(x, mod, norm_g, w1, wg, qg, wqn, wqr, wabs, kvg, *tabs)


def _proj_b_ctx_kernel(x_ref, mod_ref, ng_ref, w1_ref, kvg_ref, k_ref, vt_ref):
    mod = mod_ref[0]
    h = (_rms(x_ref[0], ng_ref[...]) * (1.0 + mod[1:2]) + mod[0:1]).astype(BF16)
    k_ref[0], vt_ref[0] = _kext(_dot(h, w1_ref[...]), kvg_ref[...], None, None, None)


def _proj_b_ctx(ctx, mod, norm_g, w1, kvg):
    bsz, c, d = ctx.shape
    full = lambda a: pl.BlockSpec(a.shape, lambda b: (0,) * a.ndim)
    return pl.pallas_call(
        _proj_b_ctx_kernel,
        grid=(bsz,),
        in_specs=[pl.BlockSpec((1, c, d), lambda b: (b, 0, 0)),
                  pl.BlockSpec((1, 3, d), lambda b: (b, 0, 0)), full(norm_g), full(w1), full(kvg)],
        out_specs=[pl.BlockSpec((1, c, B_FEAT), lambda b: (b, 0, 0)),
                   pl.BlockSpec((1, B_KV_RANK, c), lambda b: (b, 0, 0))],
        out_shape=[jax.ShapeDtypeStruct((bsz, c, B_FEAT), BF16),
                   jax.ShapeDtypeStruct((bsz, B_KV_RANK, c), BF16)],
        compiler_params=pltpu.CompilerParams(dimension_semantics=("parallel",)),
        name="proj_b_ctx",
    )(ctx, mod, norm_g, w1, kvg)


def _flash_b_kernel(q_ref, k_ref, vt_ref, g_ref, x_ref, mod_ref, wvt_ref, wo_ref, fg_ref, out_ref,
                    m_ref, l_ref, acc_ref, s_ref, mx_ref, *, tiles, tk, nk, hb):
    tq = LANES
    rt = hb * tq
    m_ref[...] = jnp.full(m_ref.shape, NEG_INF, F32)
    l_ref[...] = jnp.zeros(l_ref.shape, F32)
    acc_ref[...] = jnp.zeros(acc_ref.shape, F32)

    hgroups = B_HEADS // hb
    nsb = tiles * hgroups
    nsteps = (nk // tk) * nsb
    unroll = max(u for u in range(2, MAX_UNROLL + 1, 2) if nsteps % u == 0)
    assert nsb & (nsb - 1) == 0 and hgroups & (hgroups - 1) == 0
    shift = nsb.bit_length() - 1
    hshift = hgroups.bit_length() - 1

    def split(k):
        return lax.shift_right_logical(k, shift), lax.bitwise_and(k, nsb - 1)

    def score_stage(k, slot):
        j, t = split(k)
        kc = k_ref[0, pl.ds(pl.multiple_of(j * tk, tk), tk), :]
        tile, hg = lax.shift_right_logical(t, hshift), lax.bitwise_and(t, hgroups - 1)
        st = _dot(kc, q_ref[0, tile, :, pl.ds(pl.multiple_of(hg * rt, rt), rt)])
        s_ref[slot] = st
        mx_ref[slot] = jnp.max(st.reshape(tk // 8, 8, rt), axis=0)

    def update_stage(k, slot):
        j, t = split(k)
        vt = vt_ref[0, :, pl.ds(pl.multiple_of(j * tk, tk), tk)]
        m_old = m_ref[t]
        m_new = jnp.maximum(m_old, jnp.max(mx_ref[slot], axis=0, keepdims=True))
        alpha = jnp.exp2(m_old - m_new)
        e = jnp.exp2(s_ref[slot] - m_new[0:1])
        l_ref[t] = alpha * l_ref[t] + jnp.sum(e.reshape(tk // 8, 8, rt), axis=0)
        acc_ref[t] = alpha[0:1] * acc_ref[t] + _dot(vt, e.astype(BF16))
        m_ref[t] = m_new

    score_stage(0, 0)

    def body(i, carry):
        for u in range(unroll):
            k = unroll * i + u
            score_stage(jnp.minimum(k + 1, nsteps - 1), (u + 1) % 2)
            update_stage(k, u % 2)
        return carry

    lax.fori_loop(0, nsteps // unroll, body, 0)
    rows = []
    for tile in range(tiles):
        outs = []
        for hg in range(hgroups):
            t = tile * hgroups + hg
            linv = 1.0 / jnp.sum(l_ref[t], axis=0, keepdims=True)
            ot = (acc_ref[t] * linv).astype(BF16)
            for p in range(hb // 2):
                pair = jnp.concatenate([ot[:, (2 * p) * tq:(2 * p + 1) * tq],
                                        ot[:, (2 * p + 1) * tq:(2 * p + 2) * tq]], axis=0)
                outs.append(_dot(wvt_ref[hg * (hb // 2) + p], pair))
        rows.append(jnp.concatenate(outs, axis=0).T)
    o = jnp.concatenate(rows, axis=0)
    z = (o * g_ref[0].astype(F32)).astype(BF16)
    xo = x_ref[0] + mod_ref[0][2:3] * _dot(z, wo_ref[...])
    out_ref[0] = _rms(xo, fg_ref[...])


def _flash_b(qp, kall, vtall, gact, x, mod, wvt, wo, final_g, max_tk, hb, tiles):
    bsz, s, d = x.shape
    nk = kall.shape[1]
    tq = tiles * LANES
    tk = max(t for t in range(LANES, max_tk + 1, LANES) if nk % t == 0)
    assert hb % 2 == 0 and B_HEADS % hb == 0
    nt = tiles * (B_HEADS // hb)
    tok = lambda w: pl.BlockSpec((1, tq, w), lambda b, i: (b, i, 0))
    full = lambda a: pl.BlockSpec(a.shape, lambda b, i: (0,) * a.ndim)
    return pl.pallas_call(
        functools.partial(_flash_b_kernel, tiles=tiles, tk=tk, nk=nk, hb=hb),
        grid=(bsz, s // tq),
        in_specs=[pl.BlockSpec((1, tiles, B_FEAT, B_HEADS * LANES), lambda b, i: (b, i, 0, 0)),
                  pl.BlockSpec((1, nk, B_FEAT), lambda b, i: (b, 0, 0)),
                  pl.BlockSpec((1, B_KV_RANK, nk), lambda b, i: (b, 0, 0)),
                  tok(B_WIDTH), tok(d), pl.BlockSpec((1, 3, d), lambda b, i: (b, 0, 0)),
                  full(wvt), full(wo), full(final_g)],
        out_specs=tok(d),
        out_shape=jax.ShapeDtypeStruct((bsz, s, d), F32),
        scratch_shapes=[pltpu.VMEM((nt, 8, hb * LANES), F32), pltpu.VMEM((nt, 8, hb * LANES), F32),
                        pltpu.VMEM((nt, B_KV_RANK, hb * LANES), F32),
                        pltpu.VMEM((2, tk, hb * LANES), F32), pltpu.VMEM((2, 8, hb * LANES), F32)],
        compiler_params=pltpu.CompilerParams(
            dimension_semantics=("parallel", "parallel"), vmem_limit_bytes=VMEM_LIMIT),
        name="flash_b",
    )(qp, kall, vtall, gact, x, mod, wvt, wo, final_g)


def _rope_tables(n_tokens, rot_dim, n_ctx):
    n_rows = n_tokens // GRID_W
    row = jnp.broadcast_to(jnp.arange(n_rows)[:, None], (n_rows, GRID_W)).reshape(-1)
    col = jnp.broadcast_to(jnp.arange(GRID_W)[None, :], (n_rows, GRID_W)).reshape(-1)
    nf = rot_dim // 4
    inv = ROPE_BASE ** (-jnp.arange(nf, dtype=F32) / nf)
    ar = row.astype(F32)[:, None] * inv
    ac = col.astype(F32)[:, None] * inv
    ang = jnp.concatenate([ar, ar, ac, ac], axis=-1)
    reps = LANES // rot_dim
    cos = jnp.tile(jnp.cos(ang), (1, reps))
    sin = jnp.tile(jnp.sin(ang), (1, reps))
    first = (jnp.arange(LANES) % (2 * nf)) < nf
    sa = jnp.where(first, -sin, 0.0)
    sb = jnp.where(first, 0.0, sin)
    pad = lambda t, v: jnp.concatenate([jnp.full((n_ctx, LANES), v, F32), t], axis=0)
    return pad(cos, 1.0), pad(sa, 0.0), pad(sb, 0.0)


def _split_mod(modout, bsz):
    d = modout.shape[1] // 3
    m = modout.reshape(8, 3, d)
    lat = m[:bsz]
    ctx = jnp.broadcast_to(m[bsz:bsz + 1], (bsz, 3, d))
    return lat, ctx


def kernel(x, c, ctx, c_ctx, norm_g_0, ada_w_0, ada_b_0, a_w_in_0, a_sinks_0, a_w_o_0, norm_g_1,
           ada_w_1, ada_b_1, b_w_in_1, b_q_norm_1, b_w_uq_1, b_kv_norm_1, b_w_ukv_1, b_w_o_1, final_g):
    bsz, s, d = x.shape
    n_ctx = ctx.shape[1]
    assert bsz + 1 <= 8
    cc = jnp.concatenate([c, c_ctx[None], jnp.zeros((8 - bsz - 1, d), F32)], axis=0)
    row = lambda g: g.reshape(1, -1)

    mod_x, mod_c = _split_mod(_modulation(cc, ada_w_0, ada_b_0), bsz)
    w = a_w_in_0.astype(BF16)
    wq, wk = w[:, :A_WIDTH], w[:, A_WIDTH:A_WIDTH + A_KV_WIDTH]
    wv, wg = w[:, A_WIDTH + A_KV_WIDTH:A_WIDTH + 2 * A_KV_WIDTH], w[:, A_WIDTH + 2 * A_KV_WIDTH:]
    tabs_a = _rope_tables(s, A_HEAD_DIM, n_ctx)
    tabs_ac = tuple(t[:n_ctx] for t in tabs_a)
    tabs_ax = tuple(t[n_ctx:] for t in tabs_a)
    wo_a = a_w_o_0.astype(BF16)
    q, kd, vt, gact = _proj_a(x, mod_x, row(norm_g_0), wq, wk, wv, wg, tabs_ax, tm=512)
    qc, kdc, vtc, gc = _proj_a(ctx, mod_c, row(norm_g_0), wq, wk, wv, wg, tabs_ac, tm=n_ctx)
    x1 = _attn_a(a_sinks_0, q, kd, vt, kdc, vtc, gact, x, mod_x, wo_a, tq=256)
    ctx1 = _attn_a_ctx(a_sinks_0, qc, kdc, vtc, gc, ctx, mod_c, wo_a)

    mod_x, mod_c = _split_mod(_modulation(cc, ada_w_1, ada_b_1), bsz)
    wb = b_w_in_1.astype(BF16)
    n_kv = B_Q_RANK + B_KV_RANK
    w1 = jnp.concatenate([wb[:, :n_kv]] + [wb[:, n_kv:n_kv + B_ROPE]] * (LANES // B_ROPE), axis=1)
    wgb = wb[:, n_kv + B_ROPE:]
    wuq = b_w_uq_1.astype(BF16).reshape(B_Q_RANK, B_HEADS, B_NOPE + B_ROPE)
    wqn = wuq[:, :, :B_NOPE].reshape(B_Q_RANK, B_HEADS * B_NOPE)
    wqr = wuq[:, :, B_NOPE:].reshape(B_Q_RANK, B_HEADS * B_ROPE)
    wukv = b_w_ukv_1.astype(BF16).reshape(B_KV_RANK, B_HEADS, B_NOPE + B_V)
    wkn_t = jnp.transpose(wukv[:, :, :B_NOPE], (1, 2, 0)).reshape(B_HEADS // 2, 2, B_NOPE, B_KV_RANK)
    zk = jnp.zeros_like(wkn_t[:, 0])
    wabs = jnp.concatenate([jnp.concatenate([wkn_t[:, 0], zk], axis=2),
                            jnp.concatenate([zk, wkn_t[:, 1]], axis=2)], axis=1)
    wvh = jnp.transpose(wukv[:, :, B_NOPE:], (1, 0, 2)).reshape(B_HEADS // 2, 2, B_KV_RANK, B_V)
    zv = jnp.zeros_like(wvh[:, 0])
    wvup_t = jnp.transpose(
        jnp.concatenate([jnp.concatenate([wvh[:, 0], zv], axis=2),
                         jnp.concatenate([zv, wvh[:, 1]], axis=2)], axis=1), (0, 2, 1))
    tabs_b = _rope_tables(s, B_ROPE, 0)
    qp, kx, vtx, gb = _proj_b(x1, mod_x, row(norm_g_1), w1, wgb, row(b_q_norm_1), wqn, wqr, wabs,
                              row(b_kv_norm_1), tabs_b, tm=256)
    kc, vtc = _proj_b_ctx(ctx1, mod_c, row(norm_g_1), w1, row(b_kv_norm_1))
    kall = jnp.concatenate([kc, kx], axis=1)
    vtall = jnp.concatenate([vtc, vtx], axis=2)
    return _flash_b(qp, kall, vtall, gb, x1, mod_x, wvup_t, b_w_o_1.astype(BF16), row(final_g),
                    max_tk=1408, hb=4, tiles=2)
```

```python
import functools
import math

import jax
import jax.numpy as jnp
from jax import lax
from jax.experimental import pallas as pl
from jax.experimental.pallas import tpu as pltpu

F32 = jnp.float32
BF16 = jnp.bfloat16

GRID_W = 64
ROPE_BASE = 10000.0
EPS = 1e-6
NEG_INF = -1e30
LOG2E = 1.4426950408889634
LANES = 128
WINDOW = 128

A_HEADS, A_KV_HEADS, A_HEAD_DIM = 16, 4, 64
A_WIDTH = A_HEADS * A_HEAD_DIM
A_KV_WIDTH = A_KV_HEADS * A_HEAD_DIM
B_HEADS, B_NOPE, B_ROPE, B_V = 16, 64, 32, 64
B_Q_RANK, B_KV_RANK = 256, 128
B_WIDTH = B_HEADS * B_V
B_FEAT = 2 * LANES

VMEM_LIMIT = 56 * 1024 * 1024
MAX_UNROLL = 24


def _silu(x):
    return x * (1.0 / (1.0 + jnp.exp(-x)))


def _rms(x, g):
    return x * lax.rsqrt(jnp.mean(x * x, axis=-1, keepdims=True) + EPS) * g


def _rope(x, cos, sin, quarter):
    if cos is None:
        return x
    first = (lax.broadcasted_iota(jnp.int32, x.shape, 1) & (2 * quarter - 1)) < quarter
    rot = jnp.where(first, -pltpu.roll(x, LANES - quarter, 1), pltpu.roll(x, quarter, 1))
    return x * cos + rot * sin


def _dot(a, b):
    return jnp.dot(a, b, preferred_element_type=F32)


def _dot_nt(a, b):
    return lax.dot_general(a, b, (((1,), (1,)), ((), ())), preferred_element_type=F32)


def _mod_kernel(c_ref, w_ref, b_ref, o_ref):
    a = _silu(c_ref[...])
    o_ref[...] = jnp.dot(a, w_ref[...], precision=lax.Precision.HIGHEST,
                         preferred_element_type=F32) + b_ref[...]


def _modulation(cc, ada_w, ada_b):
    d = ada_w.shape[0]
    n = ada_w.shape[1]
    return pl.pallas_call(
        _mod_kernel,
        grid=(n // d,),
        in_specs=[pl.BlockSpec((8, d), lambda j: (0, 0)),
                  pl.BlockSpec((d, d), lambda j: (0, j)),
                  pl.BlockSpec((1, d), lambda j: (0, j))],
        out_specs=pl.BlockSpec((8, d), lambda j: (0, j)),
        out_shape=jax.ShapeDtypeStruct((8, n), F32),
        name="adaln_mod",
    )(cc, ada_w, ada_b.reshape(1, n))


def _proj_a_kernel(x_ref, mod_ref, ng_ref, wq_ref, wk_ref, wv_ref, wg_ref, *rest, qscale, rope):
    cos, sin = (rest[0][...], rest[1][...]) if rope else (None, None)
    q_ref, kd_ref, vt_ref, g_ref = rest[-4:]
    mod = mod_ref[0]
    h = (_rms(x_ref[0], ng_ref[...]) * (1.0 + mod[1:2]) + mod[0:1]).astype(BF16)
    quarter = A_HEAD_DIM // 4
    q = _dot(h, wq_ref[...])
    for j in range(A_WIDTH // LANES):
        sl = slice(j * LANES, (j + 1) * LANES)
        q_ref[0, :, sl] = (_rope(q[:, sl], cos, sin, quarter) * qscale).astype(BF16)
    lo = lax.broadcasted_iota(jnp.int32, (q.shape[0], LANES), 1) < A_HEAD_DIM
    k = _dot(h, wk_ref[...])
    for m in range(A_KV_WIDTH // LANES):
        sl = slice(m * LANES, (m + 1) * LANES)
        kp = _rope(k[:, sl], cos, sin, quarter)
        ks = pltpu.roll(kp, A_HEAD_DIM, 1)
        kd_ref[0, :, 2 * m * LANES:(2 * m + 1) * LANES] = jnp.where(lo, kp, ks).astype(BF16)
        kd_ref[0, :, (2 * m + 1) * LANES:(2 * m + 2) * LANES] = jnp.where(lo, ks, kp).astype(BF16)
    vt_ref[0] = _dot(h, wv_ref[...]).astype(BF16).T
    g_ref[0] = _silu(_dot(h, wg_ref[...])).astype(BF16)


def _proj_a(x, mod, norm_g, wq, wk, wv, wg, tabs, tm):
    bsz, s, d = x.shape
    kdw = A_KV_HEADS * LANES
    full = lambda a: pl.BlockSpec(a.shape, lambda b, t: (0,) * a.ndim)
    tab = pl.BlockSpec((tm, LANES), lambda b, t: (t, 0))
    tok = lambda w: pl.BlockSpec((1, tm, w), lambda b, t: (b, t, 0))
    return pl.pallas_call(
        functools.partial(_proj_a_kernel, qscale=A_HEAD_DIM ** -0.5 * LOG2E, rope=bool(tabs)),
        grid=(bsz, s // tm),
        in_specs=[tok(d), pl.BlockSpec((1, 3, d), lambda b, t: (b, 0, 0)), full(norm_g),
                  full(wq), full(wk), full(wv), full(wg)] + [tab] * len(tabs),
        out_specs=[tok(A_WIDTH), tok(kdw),
                   pl.BlockSpec((1, A_KV_WIDTH, tm), lambda b, t: (b, 0, t)), tok(A_WIDTH)],
        out_shape=[jax.ShapeDtypeStruct((bsz, s, A_WIDTH), BF16),
                   jax.ShapeDtypeStruct((bsz, s, kdw), BF16),
                   jax.ShapeDtypeStruct((bsz, A_KV_WIDTH, s), BF16),
                   jax.ShapeDtypeStruct((bsz, s, A_WIDTH), BF16)],
        compiler_params=pltpu.CompilerParams(
            dimension_semantics=("parallel", "parallel"), vmem_limit_bytes=VMEM_LIMIT),
        name="proj_a",
    )(x, mod, norm_g, wq, wk, wv, wg, *tabs)


def _attn_a_core(sink_ref, q, kd, vt, cap_loc, tq):
    group = A_HEADS // A_KV_HEADS
    nk = kd.shape[0]
    lane = lax.broadcasted_iota(jnp.int32, (1, 2 * tq), 1)
    klane = lax.broadcasted_iota(jnp.int32, (nk, LANES), 1)
    steps = [(g, par) for g in range(A_KV_HEADS) for par in range(2)]

    def scores(g, par):
        qrows = jnp.concatenate([q[:, (2 * g) * LANES:(2 * g + 1) * LANES],
                                 q[:, (2 * g + 1) * LANES:(2 * g + 2) * LANES]], axis=0)
        kg = kd[:, g * LANES:(g + 1) * LANES]
        keep = (klane < A_HEAD_DIM) if par == 0 else (klane >= A_HEAD_DIM)
        return _dot_nt(jnp.where(keep, kg, jnp.zeros_like(kg)), qrows)

    def finish(g, par, st):
        h0 = group * g + par
        sink = jnp.where(lane < tq, sink_ref[h0] * LOG2E, sink_ref[h0 + 2] * LOG2E)
        if cap_loc is not None:
            nl = cap_loc.shape[0]
            st = jnp.concatenate([jnp.minimum(st[:nl], cap_loc), st[nl:]], axis=0)
        m = jnp.maximum(sink, jnp.max(st, axis=0, keepdims=True))
        e = jnp.exp2(st - m)
        l = jnp.exp2(sink - m) + jnp.sum(e, axis=0, keepdims=True)
        ot = _dot(vt[g * A_HEAD_DIM:(g + 1) * A_HEAD_DIM], e.astype(BF16)) * (1.0 / l)
        return {h0: ot[:, :tq], h0 + 2: ot[:, tq:]}

    heads = {}
    cur = scores(*steps[0])
    for i, (g, par) in enumerate(steps):
        nxt = scores(*steps[i + 1]) if i + 1 < len(steps) else None
        heads.update(finish(g, par, cur))
        cur = nxt
    return jnp.concatenate([heads[h] for h in range(A_HEADS)], axis=0).T


def _attn_a_finish(o, g_ref, x_ref, mod_ref, wo_ref, out_ref):
    z = (o * g_ref[0].astype(F32)).astype(BF16)
    out_ref[0] = x_ref[0] + mod_ref[0][2:3] * _dot(z, wo_ref[...])


def _attn_a_kernel(sink_ref, q_ref, kp_ref, kc_ref, kn_ref, kx_ref, vp_ref, vc_ref, vn_ref, vx_ref,
                   g_ref, x_ref, mod_ref, wo_ref, out_ref, *, tq, seq):
    i = pl.program_id(1)
    kd = jnp.concatenate([kp_ref[0], kc_ref[0], kn_ref[0], kx_ref[0]], axis=0)
    vt = jnp.concatenate([vp_ref[0], vc_ref[0], vn_ref[0], vx_ref[0]], axis=1)
    nl = tq + 2 * WINDOW
    krow = lax.broadcasted_iota(jnp.int32, (nl, 2 * tq), 0)
    qcol = lax.broadcasted_iota(jnp.int32, (nl, 2 * tq), 1) & (tq - 1)
    kpos = i * tq - WINDOW + krow
    ok = (jnp.abs(qcol + WINDOW - krow) <= WINDOW) & (kpos >= 0) & (kpos < seq)
    o = _attn_a_core(sink_ref, q_ref[0], kd, vt, jnp.where(ok, jnp.inf, NEG_INF), tq)
    _attn_a_finish(o, g_ref, x_ref, mod_ref, wo_ref, out_ref)


def _attn_a_ctx_kernel(sink_ref, q_ref, kx_ref, vx_ref, g_ref, x_ref, mod_ref, wo_ref, out_ref, *, tq):
    o = _attn_a_core(sink_ref, q_ref[0], kx_ref[0], vx_ref[0], None, tq)
    _attn_a_finish(o, g_ref, x_ref, mod_ref, wo_ref, out_ref)


def _attn_a(sinks, q, kd, vt, kdc, vtc, gact, x, mod, wo, tq):
    bsz, s, d = x.shape
    c = kdc.shape[1]
    kdw = kd.shape[2]
    vw = vt.shape[1]
    assert tq & (tq - 1) == 0 and tq % WINDOW == 0
    r = tq // WINDOW
    nblk = s // WINDOW
    tok = lambda w: pl.BlockSpec((1, tq, w), lambda b, i: (b, i, 0))
    prev_i = lambda i: jnp.maximum(i * r - 1, 0)
    next_i = lambda i: jnp.minimum((i + 1) * r, nblk - 1)
    return pl.pallas_call(
        functools.partial(_attn_a_kernel, tq=tq, seq=s),
        grid=(bsz, s // tq),
        in_specs=[pl.BlockSpec(memory_space=pltpu.SMEM), tok(A_WIDTH),
                  pl.BlockSpec((1, WINDOW, kdw), lambda b, i: (b, prev_i(i), 0)), tok(kdw),
                  pl.BlockSpec((1, WINDOW, kdw), lambda b, i: (b, next_i(i), 0)),
                  pl.BlockSpec((1, c, kdw), lambda b, i: (b, 0, 0)),
                  pl.BlockSpec((1, vw, WINDOW), lambda b, i: (b, 0, prev_i(i))),
                  pl.BlockSpec((1, vw, tq), lambda b, i: (b, 0, i)),
                  pl.BlockSpec((1, vw, WINDOW), lambda b, i: (b, 0, next_i(i))),
                  pl.BlockSpec((1, vw, c), lambda b, i: (b, 0, 0)),
                  tok(A_WIDTH), tok(d), pl.BlockSpec((1, 3, d), lambda b, i: (b, 0, 0)),
                  pl.BlockSpec(wo.shape, lambda b, i: (0, 0))],
        out_specs=tok(d),
        out_shape=jax.ShapeDtypeStruct((bsz, s, d), F32),
        compiler_params=pltpu.CompilerParams(
            dimension_semantics=("parallel", "parallel"), vmem_limit_bytes=VMEM_LIMIT),
        name="attn_a",
    )(sinks, q, kd, kd, kd, kdc, vt, vt, vt, vtc, gact, x, mod, wo)


def _attn_a_ctx(sinks, qc, kdc, vtc, gc, ctx, mod, wo):
    bsz, c, d = ctx.shape
    tok = lambda w: pl.BlockSpec((1, c, w), lambda b: (b, 0, 0))
    return pl.pallas_call(
        functools.partial(_attn_a_ctx_kernel, tq=c),
        grid=(bsz,),
        in_specs=[pl.BlockSpec(memory_space=pltpu.SMEM), tok(A_WIDTH), tok(kdc.shape[2]),
                  pl.BlockSpec((1, vtc.shape[1], c), lambda b: (b, 0, 0)),
                  tok(A_WIDTH), tok(d), pl.BlockSpec((1, 3, d), lambda b: (b, 0, 0)),
                  pl.BlockSpec(wo.shape, lambda b: (0, 0))],
        out_specs=tok(d),
        out_shape=jax.ShapeDtypeStruct((bsz, c, d), F32),
        compiler_params=pltpu.CompilerParams(
            dimension_semantics=("parallel",), vmem_limit_bytes=VMEM_LIMIT),
        name="attn_a_ctx",
    )(sinks, qc, kdc, vtc, gc, ctx, mod, wo)


def _kext(a, kvg, cos, sin):
    ckv = _rms(a[:, B_Q_RANK:B_Q_RANK + B_KV_RANK], kvg)
    kr = a[:, B_Q_RANK + B_KV_RANK:]
    kr = _rope(kr, cos, sin, B_ROPE // 4)
    return jnp.concatenate([ckv, kr], axis=1).astype(BF16), ckv.T.astype(BF16)


def _proj_b_kernel(x_ref, mod_ref, ng_ref, w1_ref, wg_ref, qg_ref, wqn_ref, wqr_ref, wabs_ref,
                   kvg_ref, c_ref, s_ref, ct_ref, st_ref, qt_ref, k_ref, vt_ref, g_ref,
                   *, qscale, tm):
    mod = mod_ref[0]
    h = (_rms(x_ref[0], ng_ref[...]) * (1.0 + mod[1:2]) + mod[0:1]).astype(BF16)
    a = _dot(h, w1_ref[...])
    k_ref[0], vt_ref[0] = _kext(a, kvg_ref[...], c_ref[...], s_ref[...])
    g_ref[0] = _silu(_dot(h, wg_ref[...])).astype(BF16)
    cq_t = _rms(a[:, :B_Q_RANK], qg_ref[...]).T.astype(BF16)
    qn_t = _dot(wqn_ref[...], cq_t).astype(BF16)
    qr_t = _dot(wqr_ref[...], cq_t)
    ct, st = ct_ref[...], st_ref[...]
    quarter = B_ROPE // 4
    heads_per_chunk = LANES // B_ROPE
    for hd in range(B_HEADS):
        xh = qr_t[hd * B_ROPE:(hd + 1) * B_ROPE]
        rot = jnp.concatenate([-xh[quarter:2 * quarter], xh[:quarter],
                               -xh[3 * quarter:], xh[2 * quarter:3 * quarter]], axis=0)
        rope = ((xh * ct + rot * st) * qscale).astype(BF16)
        j = hd % heads_per_chunk
        for t in range(tm // LANES):
            blk = rope[:, t * LANES:(t + 1) * LANES]
            pieces = [jnp.zeros((j * B_ROPE, LANES), BF16)] if j else []
            pieces.append(blk)
            if j + 1 < heads_per_chunk:
                pieces.append(jnp.zeros(((heads_per_chunk - 1 - j) * B_ROPE, LANES), BF16))
            qt_ref[0, t, LANES:, hd * LANES:(hd + 1) * LANES] = jnp.concatenate(pieces, axis=0)
    for m in range(B_HEADS // 2):
        qa_t = (_dot(wabs_ref[m], qn_t[m * LANES:(m + 1) * LANES]) * qscale).astype(BF16)
        for half in range(2):
            hd = 2 * m + half
            for t in range(tm // LANES):
                qt_ref[0, t, :LANES, hd * LANES:(hd + 1) * LANES] = (
                    qa_t[half * LANES:(half + 1) * LANES, t * LANES:(t + 1) * LANES])


def _proj_b(x, mod, norm_g, w1, wg, qg, wqn_t, wqr_t, wabs_t, kvg, tabs, tabs_t, tm):
    bsz, s, d = x.shape
    full = lambda a: pl.BlockSpec(a.shape, lambda b, t: (0,) * a.ndim)
    tab = pl.BlockSpec((tm, LANES), lambda b, t: (t, 0))
    tab_t = pl.BlockSpec((B_ROPE, tm), lambda b, t: (0, t))
    tok = lambda w: pl.BlockSpec((1, tm, w), lambda b, t: (b, t, 0))
    nt = tm // LANES
    return pl.pallas_call(
        functools.partial(_proj_b_kernel, qscale=(B_NOPE + B_ROPE) ** -0.5 * LOG2E, tm=tm),
        grid=(bsz, s // tm),
        in_specs=[tok(d), pl.BlockSpec((1, 3, d), lambda b, t: (b, 0, 0)), full(norm_g),
                  full(w1), full(wg), full(qg), full(wqn_t), full(wqr_t), full(wabs_t), full(kvg),
                  tab, tab, tab_t, tab_t],
        out_specs=[pl.BlockSpec((1, nt, B_FEAT, B_HEADS * LANES), lambda b, t: (b, t, 0, 0)),
                   tok(B_FEAT), pl.BlockSpec((1, B_KV_RANK, tm), lambda b, t: (b, 0, t)),
                   tok(B_WIDTH)],
        out_shape=[jax.ShapeDtypeStruct((bsz, s // LANES, B_FEAT, B_HEADS * LANES), BF16),
                   jax.ShapeDtypeStruct((bsz, s, B_FEAT), BF16),
                   jax.ShapeDtypeStruct((bsz, B_KV_RANK, s), BF16),
                   jax.ShapeDtypeStruct((bsz, s, B_WIDTH), BF16)],
        compiler_params=pltpu.CompilerParams(
            dimension_semantics=("parallel", "parallel"), vmem_limit_bytes=VMEM_LIMIT),
        name="proj_b",
    )(x, mod, norm_g, w1, wg, qg, wqn_t, wqr_t, wabs_t, kvg, *tabs, *tabs_t)


def _proj_b_ctx_kernel(x_ref, mod_ref, ng_ref, w1_ref, kvg_ref, k_ref, vt_ref):
    mod = mod_ref[0]
    h = (_rms(x_ref[0], ng_ref[...]) * (1.0 + mod[1:2]) + mod[0:1]).astype(BF16)
    k_ref[0], vt_ref[0] = _kext(_dot(h, w1_ref[...]), kvg_ref[...], None, None)


def _proj_b_ctx(ctx, mod, norm_g, w1, kvg):
    bsz, c, d = ctx.shape
    full = lambda a: pl.BlockSpec(a.shape, lambda b: (0,) * a.ndim)
    return pl.pallas_call(
        _proj_b_ctx_kernel,
        grid=(bsz,),
        in_specs=[pl.BlockSpec((1, c, d), lambda b: (b, 0, 0)),
                  pl.BlockSpec((1, 3, d), lambda b: (b, 0, 0)), full(norm_g), full(w1), full(kvg)],
        out_specs=[pl.BlockSpec((1, c, B_FEAT), lambda b: (b, 0, 0)),
                   pl.BlockSpec((1, B_KV_RANK, c), lambda b: (b, 0, 0))],
        out_shape=[jax.ShapeDtypeStruct((bsz, c, B_FEAT), BF16),
                   jax.ShapeDtypeStruct((bsz, B_KV_RANK, c), BF16)],
        compiler_params=pltpu.CompilerParams(dimension_semantics=("parallel",)),
        name="proj_b_ctx",
    )(ctx, mod, norm_g, w1, kvg)


def _flash_b_kernel(q_ref, k_ref, vt_ref, g_ref, x_ref, mod_ref, wvt_ref, wo_ref, fg_ref, out_ref,
                    m_ref, l_ref, acc_ref, s_ref, mx_ref, *, tiles, tk, nk, hb):
    tq = LANES
    rt = hb * tq
    m_ref[...] = jnp.full(m_ref.shape, NEG_INF, F32)
    l_ref[...] = jnp.zeros(l_ref.shape, F32)
    acc_ref[...] = jnp.zeros(acc_ref.shape, F32)

    hgroups = B_HEADS // hb
    nsb = tiles * hgroups
    nsteps = (nk // tk) * nsb
    unroll = max(u for u in range(2, MAX_UNROLL + 1, 2) if nsteps % u == 0)
    assert nsb & (nsb - 1) == 0 and hgroups & (hgroups - 1) == 0
    shift = nsb.bit_length() - 1
    hshift = hgroups.bit_length() - 1

    def split(k):
        return lax.shift_right_logical(k, shift), lax.bitwise_and(k, nsb - 1)

    def score_stage(k, slot):
        j, t = split(k)
        kc = k_ref[0, pl.ds(pl.multiple_of(j * tk, tk), tk), :]
        tile, hg = lax.shift_right_logical(t, hshift), lax.bitwise_and(t, hgroups - 1)
        st = _dot(kc, q_ref[0, tile, :, pl.ds(pl.multiple_of(hg * rt, rt), rt)])
        s_ref[slot] = st
        mx_ref[slot] = jnp.max(st.reshape(tk // 8, 8, rt), axis=0)

    def update_stage(k, slot):
        j, t = split(k)
        vt = vt_ref[0, :, pl.ds(pl.multiple_of(j * tk, tk), tk)]
        m_old = m_ref[t]
        m_new = jnp.maximum(m_old, jnp.max(mx_ref[slot], axis=0, keepdims=True))
        alpha = jnp.exp2(m_old - m_new)
        e = jnp.exp2(s_ref[slot] - m_new[0:1])
        l_ref[t] = alpha * l_ref[t] + jnp.sum(e.reshape(tk // 8, 8, rt), axis=0)
        acc_ref[t] = alpha[0:1] * acc_ref[t] + _dot(vt, e.astype(BF16))
        m_ref[t] = m_new

    score_stage(0, 0)

    def body(i, carry):
        for u in range(unroll):
            k = unroll * i + u
            score_stage(jnp.minimum(k + 1, nsteps - 1), (u + 1) % 2)
            update_stage(k, u % 2)
        return carry

    lax.fori_loop(0, nsteps // unroll, body, 0)
    rows = []
    for tile in range(tiles):
        outs = []
        for hg in range(hgroups):
            t = tile * hgroups + hg
            linv = 1.0 / jnp.sum(l_ref[t], axis=0, keepdims=True)
            ot = (acc_ref[t] * linv).astype(BF16)
            for p in range(hb // 2):
                pair = jnp.concatenate([ot[:, (2 * p) * tq:(2 * p + 1) * tq],
                                        ot[:, (2 * p + 1) * tq:(2 * p + 2) * tq]], axis=0)
                outs.append(_dot(wvt_ref[hg * (hb // 2) + p], pair))
        rows.append(jnp.concatenate(outs, axis=0).T)
    o = jnp.concatenate(rows, axis=0)
    z = (o * g_ref[0].astype(F32)).astype(BF16)
    xo = x_ref[0] + mod_ref[0][2:3] * _dot(z, wo_ref[...])
    out_ref[0] = _rms(xo, fg_ref[...])


def _flash_b(qp, kall, vtall, gact, x, mod, wvt, wo, final_g, max_tk, hb, tiles):
    bsz, s, d = x.shape
    nk = kall.shape[1]
    tq = tiles * LANES
    tk = max(t for t in range(LANES, max_tk + 1, LANES) if nk % t == 0)
    assert hb % 2 == 0 and B_HEADS % hb == 0
    nt = tiles * (B_HEADS // hb)
    tok = lambda w: pl.BlockSpec((1, tq, w), lambda b, i: (b, i, 0))
    full = lambda a: pl.BlockSpec(a.shape, lambda b, i: (0,) * a.ndim)
    return pl.pallas_call(
        functools.partial(_flash_b_kernel, tiles=tiles, tk=tk, nk=nk, hb=hb),
        grid=(bsz, s // tq),
        in_specs=[pl.BlockSpec((1, tiles, B_FEAT, B_HEADS * LANES), lambda b, i: (b, i, 0, 0)),
                  pl.BlockSpec((1, nk, B_FEAT), lambda b, i: (b, 0, 0)),
                  pl.BlockSpec((1, B_KV_RANK, nk), lambda b, i: (b, 0, 0)),
                  tok(B_WIDTH), tok(d), pl.BlockSpec((1, 3, d), lambda b, i: (b, 0, 0)),
                  full(wvt), full(wo), full(final_g)],
        out_specs=tok(d),
        out_shape=jax.ShapeDtypeStruct((bsz, s, d), F32),
        scratch_shapes=[pltpu.VMEM((nt, 8, hb * LANES), F32), pltpu.VMEM((nt, 8, hb * LANES), F32),
                        pltpu.VMEM((nt, B_KV_RANK, hb * LANES), F32),
                        pltpu.VMEM((2, tk, hb * LANES), F32), pltpu.VMEM((2, 8, hb * LANES), F32)],
        compiler_params=pltpu.CompilerParams(
            dimension_semantics=("parallel", "parallel"), vmem_limit_bytes=VMEM_LIMIT),
        name="flash_b",
    )(qp, kall, vtall, gact, x, mod, wvt, wo, final_g)


def _rope_cos_sin(n_tokens, rot_dim):
    n_rows = n_tokens // GRID_W
    row = jnp.broadcast_to(jnp.arange(n_rows)[:, None], (n_rows, GRID_W)).reshape(-1)
    col = jnp.broadcast_to(jnp.arange(GRID_W)[None, :], (n_rows, GRID_W)).reshape(-1)
    nf = rot_dim // 4
    inv = ROPE_BASE ** (-jnp.arange(nf, dtype=F32) / nf)
    ar = row.astype(F32)[:, None] * inv
    ac = col.astype(F32)[:, None] * inv
    ang = jnp.concatenate([ar, ar, ac, ac], axis=-1)
    return jnp.cos(ang), jnp.sin(ang)


def _rope_tables(n_tokens, rot_dim):
    reps = LANES // rot_dim
    cos, sin = _rope_cos_sin(n_tokens, rot_dim)
    return jnp.tile(cos, (1, reps)), jnp.tile(sin, (1, reps))


def _split_mod(modout, bsz):
    d = modout.shape[1] // 3
    m = modout.reshape(8, 3, d)
    lat = m[:bsz]
    ctx = jnp.broadcast_to(m[bsz:bsz + 1], (bsz, 3, d))
    return lat, ctx


def kernel(x, c, ctx, c_ctx, norm_g_0, ada_w_0, ada_b_0, a_w_in_0, a_sinks_0, a_w_o_0, norm_g_1,
           ada_w_1, ada_b_1, b_w_in_1, b_q_norm_1, b_w_uq_1, b_kv_norm_1, b_w_ukv_1, b_w_o_1, final_g):
    bsz, s, d = x.shape
    n_ctx = ctx.shape[1]
    assert bsz + 1 <= 8
    cc = jnp.concatenate([c, c_ctx[None], jnp.zeros((8 - bsz - 1, d), F32)], axis=0)
    row = lambda g: g.reshape(1, -1)

    mod_x, mod_c = _split_mod(_modulation(cc, ada_w_0, ada_b_0), bsz)
    w = a_w_in_0.astype(BF16)
    wq, wk = w[:, :A_WIDTH], w[:, A_WIDTH:A_WIDTH + A_KV_WIDTH]
    wv, wg = w[:, A_WIDTH + A_KV_WIDTH:A_WIDTH + 2 * A_KV_WIDTH], w[:, A_WIDTH + 2 * A_KV_WIDTH:]
    tabs_a = _rope_tables(s, A_HEAD_DIM)
    wo_a = a_w_o_0.astype(BF16)
    q, kd, vt, gact = _proj_a(x, mod_x, row(norm_g_0), wq, wk, wv, wg, tabs_a, tm=512)
    qc, kdc, vtc, gc = _proj_a(ctx, mod_c, row(norm_g_0), wq, wk, wv, wg, (), tm=n_ctx)
    x1 = _attn_a(a_sinks_0, q, kd, vt, kdc, vtc, gact, x, mod_x, wo_a, tq=256)
    ctx1 = _attn_a_ctx(a_sinks_0, qc, kdc, vtc, gc, ctx, mod_c, wo_a)

    mod_x, mod_c = _split_mod(_modulation(cc, ada_w_1, ada_b_1), bsz)
    wb = b_w_in_1.astype(BF16)
    n_kv = B_Q_RANK + B_KV_RANK
    w1 = jnp.concatenate([wb[:, :n_kv]] + [wb[:, n_kv:n_kv + B_ROPE]] * (LANES // B_ROPE), axis=1)
    wgb = wb[:, n_kv + B_ROPE:]
    wuq = b_w_uq_1.astype(BF16).reshape(B_Q_RANK, B_HEADS, B_NOPE + B_ROPE)
    wqn_t = wuq[:, :, :B_NOPE].reshape(B_Q_RANK, B_HEADS * B_NOPE).T
    wqr_t = wuq[:, :, B_NOPE:].reshape(B_Q_RANK, B_HEADS * B_ROPE).T
    wukv = b_w_ukv_1.astype(BF16).reshape(B_KV_RANK, B_HEADS, B_NOPE + B_V)
    wkn_t = jnp.transpose(wukv[:, :, :B_NOPE], (1, 2, 0)).reshape(B_HEADS // 2, 2, B_NOPE, B_KV_RANK)
    zk = jnp.zeros_like(wkn_t[:, 0])
    wabs_t = jnp.transpose(
        jnp.concatenate([jnp.concatenate([wkn_t[:, 0], zk], axis=2),
                         jnp.concatenate([zk, wkn_t[:, 1]], axis=2)], axis=1), (0, 2, 1))
    wvh = jnp.transpose(wukv[:, :, B_NOPE:], (1, 0, 2)).reshape(B_HEADS // 2, 2, B_KV_RANK, B_V)
    zv = jnp.zeros_like(wvh[:, 0])
    wvup_t = jnp.transpose(
        jnp.concatenate([jnp.concatenate([wvh[:, 0], zv], axis=2),
                         jnp.concatenate([zv, wvh[:, 1]], axis=2)], axis=1), (0, 2, 1))
    tabs_b = _rope_tables(s, B_ROPE)
    cos_b, sin_b = _rope_cos_sin(s, B_ROPE)
    qp, kx, vtx, gb = _proj_b(x1, mod_x, row(norm_g_1), w1, wgb, row(b_q_norm_1), wqn_t, wqr_t,
                              wabs_t, row(b_kv_norm_1), tabs_b, (cos_b.T, sin_b.T), tm=512)
    kc, vtc = _proj_b_ctx(ctx1, mod_c, row(norm_g_1), w1, row(b_kv_norm_1))
    kall = jnp.concatenate([kc, kx], axis=1)
    vtall = jnp.concatenate([vtc, vtx], axis=2)
    return _flash_b(qp, kall, vtall, gb, x1, mod_x, wvup_t, b_w_o_1.astype(BF16), row(final_g),
                    max_tk=1408, hb=4, tiles=4)
```

```python
import functools
import math

import jax
import jax.numpy as jnp
from jax import lax
from jax.experimental import pallas as pl
from jax.experimental.pallas import tpu as pltpu

F32 = jnp.float32
BF16 = jnp.bfloat16

GRID_W = 64
ROPE_BASE = 10000.0
EPS = 1e-6
NEG_INF = -1e30
LOG2E = 1.4426950408889634
LANES = 128
WINDOW = 128

A_HEADS, A_KV_HEADS, A_HEAD_DIM = 16, 4, 64
A_WIDTH = A_HEADS * A_HEAD_DIM
A_KV_WIDTH = A_KV_HEADS * A_HEAD_DIM
B_HEADS, B_NOPE, B_ROPE, B_V = 16, 64, 32, 64
B_Q_RANK, B_KV_RANK = 256, 128
B_WIDTH = B_HEADS * B_V
B_FEAT = 2 * LANES

VMEM_LIMIT = 56 * 1024 * 1024
MAX_UNROLL = 24


def _silu(x):
    return x * (1.0 / (1.0 + jnp.exp(-x)))


def _rms(x, g):
    return x * lax.rsqrt(jnp.mean(x * x, axis=-1, keepdims=True) + EPS) * g


def _rope(x, cos, sin, quarter):
    if cos is None:
        return x
    first = (lax.broadcasted_iota(jnp.int32, x.shape, 1) & (2 * quarter - 1)) < quarter
    rot = jnp.where(first, -pltpu.roll(x, LANES - quarter, 1), pltpu.roll(x, quarter, 1))
    return x * cos + rot * sin


def _dot(a, b):
    return jnp.dot(a, b, preferred_element_type=F32)


def _dot_nt(a, b):
    return lax.dot_general(a, b, (((1,), (1,)), ((), ())), preferred_element_type=F32)


def _mod_kernel(c_ref, w_ref, b_ref, o_ref):
    a = _silu(c_ref[...])
    o_ref[...] = jnp.dot(a, w_ref[...], precision=lax.Precision.HIGHEST,
                         preferred_element_type=F32) + b_ref[...]


def _modulation(cc, ada_w, ada_b):
    d = ada_w.shape[0]
    n = ada_w.shape[1]
    return pl.pallas_call(
        _mod_kernel,
        grid=(n // d,),
        in_specs=[pl.BlockSpec((8, d), lambda j: (0, 0)),
                  pl.BlockSpec((d, d), lambda j: (0, j)),
                  pl.BlockSpec((1, d), lambda j: (0, j))],
        out_specs=pl.BlockSpec((8, d), lambda j: (0, j)),
        out_shape=jax.ShapeDtypeStruct((8, n), F32),
        name="adaln_mod",
    )(cc, ada_w, ada_b.reshape(1, n))


def _proj_a_kernel(x_ref, mod_ref, ng_ref, wq_ref, wk_ref, wv_ref, wg_ref, *rest, qscale, rope):
    cos, sin = (rest[0][...], rest[1][...]) if rope else (None, None)
    q_ref, kd_ref, vt_ref, g_ref = rest[-4:]
    mod = mod_ref[0]
    h = (_rms(x_ref[0], ng_ref[...]) * (1.0 + mod[1:2]) + mod[0:1]).astype(BF16)
    quarter = A_HEAD_DIM // 4
    q = _dot(h, wq_ref[...])
    for j in range(A_WIDTH // LANES):
        sl = slice(j * LANES, (j + 1) * LANES)
        q_ref[0, :, sl] = (_rope(q[:, sl], cos, sin, quarter) * qscale).astype(BF16)
    lo = lax.broadcasted_iota(jnp.int32, (q.shape[0], LANES), 1) < A_HEAD_DIM
    k = _dot(h, wk_ref[...])
    for m in range(A_KV_WIDTH // LANES):
        sl = slice(m * LANES, (m + 1) * LANES)
        kp = _rope(k[:, sl], cos, sin, quarter)
        ks = pltpu.roll(kp, A_HEAD_DIM, 1)
        kd_ref[0, :, 2 * m * LANES:(2 * m + 1) * LANES] = jnp.where(lo, kp, ks).astype(BF16)
        kd_ref[0, :, (2 * m + 1) * LANES:(2 * m + 2) * LANES] = jnp.where(lo, ks, kp).astype(BF16)
    vt_ref[0] = _dot(h, wv_ref[...]).astype(BF16).T
    g_ref[0] = _silu(_dot(h, wg_ref[...])).astype(BF16)


def _proj_a(x, mod, norm_g, wq, wk, wv, wg, tabs, tm):
    bsz, s, d = x.shape
    kdw = A_KV_HEADS * LANES
    full = lambda a: pl.BlockSpec(a.shape, lambda b, t: (0,) * a.ndim)
    tab = pl.BlockSpec((tm, LANES), lambda b, t: (t, 0))
    tok = lambda w: pl.BlockSpec((1, tm, w), lambda b, t: (b, t, 0))
    return pl.pallas_call(
        functools.partial(_proj_a_kernel, qscale=A_HEAD_DIM ** -0.5 * LOG2E, rope=bool(tabs)),
        grid=(bsz, s // tm),
        in_specs=[tok(d), pl.BlockSpec((1, 3, d), lambda b, t: (b, 0, 0)), full(norm_g),
                  full(wq), full(wk), full(wv), full(wg)] + [tab] * len(tabs),
        out_specs=[tok(A_WIDTH), tok(kdw),
                   pl.BlockSpec((1, A_KV_WIDTH, tm), lambda b, t: (b, 0, t)), tok(A_WIDTH)],
        out_shape=[jax.ShapeDtypeStruct((bsz, s, A_WIDTH), BF16),
                   jax.ShapeDtypeStruct((bsz, s, kdw), BF16),
                   jax.ShapeDtypeStruct((bsz, A_KV_WIDTH, s), BF16),
                   jax.ShapeDtypeStruct((bsz, s, A_WIDTH), BF16)],
        compiler_params=pltpu.CompilerParams(
            dimension_semantics=("parallel", "parallel"), vmem_limit_bytes=VMEM_LIMIT),
        name="proj_a",
    )(x, mod, norm_g, wq, wk, wv, wg, *tabs)


def _attn_a_core(sink_ref, q_ref, kd_ref, vt_ref, cap_ref, s_ref, mx_ref, ot_ref, tq):
    group = A_HEADS // A_KV_HEADS
    nk = kd_ref.shape[0]
    nsteps = 2 * A_KV_HEADS
    lane = lax.broadcasted_iota(jnp.int32, (1, 2 * tq), 1)
    khalf = lax.shift_right_logical(lax.broadcasted_iota(jnp.int32, (nk, LANES), 1),
                                    A_HEAD_DIM.bit_length() - 1)

    def split(i):
        return lax.shift_right_logical(i, 1), lax.bitwise_and(i, 1)

    def score_stage(i, slot):
        g, par = split(i)
        x = q_ref[0, :, pl.ds(pl.multiple_of(g * 2 * LANES, 2 * LANES), 2 * LANES)]
        qrows = jnp.concatenate([x[:, :LANES], x[:, LANES:]], axis=0)
        kg = kd_ref[:, pl.ds(pl.multiple_of(g * LANES, LANES), LANES)]
        st = _dot_nt(jnp.where(khalf == par, kg, jnp.zeros_like(kg)), qrows)
        if cap_ref is not None:
            nl = cap_ref.shape[0]
            st = jnp.concatenate([jnp.minimum(st[:nl], cap_ref[...]), st[nl:]], axis=0)
        s_ref[slot] = st
        mx_ref[slot] = jnp.max(st.reshape(nk // 8, 8, 2 * tq), axis=0)

    def update_stage(i, slot):
        g, par = split(i)
        h0 = group * g + par
        sink = jnp.where(lane < tq, sink_ref[h0] * LOG2E, sink_ref[h0 + 2] * LOG2E)
        m = jnp.maximum(sink, jnp.max(mx_ref[slot], axis=0, keepdims=True))
        e = jnp.exp2(s_ref[slot] - m)
        l = jnp.exp2(sink - m) + jnp.sum(e, axis=0, keepdims=True)
        vtg = vt_ref[pl.ds(pl.multiple_of(g * A_HEAD_DIM, A_HEAD_DIM), A_HEAD_DIM), :]
        ot = _dot(vtg, e.astype(BF16)) * (1.0 / l)
        ot_ref[pl.ds(pl.multiple_of(h0 * A_HEAD_DIM, A_HEAD_DIM), A_HEAD_DIM), :] = ot[:, :tq]
        ot_ref[pl.ds(pl.multiple_of((h0 + 2) * A_HEAD_DIM, A_HEAD_DIM), A_HEAD_DIM), :] = ot[:, tq:]

    score_stage(0, 0)

    def body(i2, carry):
        for u in range(2):
            i = 2 * i2 + u
            score_stage(jnp.minimum(i + 1, nsteps - 1), (u + 1) % 2)
            update_stage(i, u % 2)
        return carry

    lax.fori_loop(0, nsteps // 2, body, 0)
    return ot_ref[...].T


def _attn_a_finish(o, g_ref, x_ref, mod_ref, wo_ref, out_ref):
    z = (o * g_ref[0].astype(F32)).astype(BF16)
    out_ref[0] = x_ref[0] + mod_ref[0][2:3] * _dot(z, wo_ref[...])


def _attn_a_kernel(sink_ref, q_ref, kp_ref, kc_ref, kn_ref, kx_ref, vp_ref, vc_ref, vn_ref, vx_ref,
                   g_ref, x_ref, mod_ref, wo_ref, out_ref,
                   kd_ref, vt_ref, cap_ref, s_ref, mx_ref, ot_ref, *, tq, seq):
    i = pl.program_id(1)
    kd_ref[...] = jnp.concatenate([kp_ref[0], kc_ref[0], kn_ref[0], kx_ref[0]], axis=0)
    vt_ref[...] = jnp.concatenate([vp_ref[0], vc_ref[0], vn_ref[0], vx_ref[0]], axis=1)
    nl = tq + 2 * WINDOW
    krow = lax.broadcasted_iota(jnp.int32, (nl, 2 * tq), 0)
    qcol = lax.broadcasted_iota(jnp.int32, (nl, 2 * tq), 1) & (tq - 1)
    kpos = i * tq - WINDOW + krow
    ok = (jnp.abs(qcol + WINDOW - krow) <= WINDOW) & (kpos >= 0) & (kpos < seq)
    cap_ref[...] = jnp.where(ok, jnp.inf, NEG_INF)
    o = _attn_a_core(sink_ref, q_ref, kd_ref, vt_ref, cap_ref, s_ref, mx_ref, ot_ref, tq)
    _attn_a_finish(o, g_ref, x_ref, mod_ref, wo_ref, out_ref)


def _attn_a_ctx_kernel(sink_ref, q_ref, kx_ref, vx_ref, g_ref, x_ref, mod_ref, wo_ref, out_ref,
                       s_ref, mx_ref, ot_ref, *, tq):
    o = _attn_a_core(sink_ref, q_ref, kx_ref.at[0], vx_ref.at[0], None, s_ref, mx_ref, ot_ref, tq)
    _attn_a_finish(o, g_ref, x_ref, mod_ref, wo_ref, out_ref)


def _attn_a_scratch(nk, tq):
    return [pltpu.VMEM((2, nk, 2 * tq), F32), pltpu.VMEM((2, 8, 2 * tq), F32),
            pltpu.VMEM((A_WIDTH, tq), F32)]


def _attn_a(sinks, q, kd, vt, kdc, vtc, gact, x, mod, wo, tq):
    bsz, s, d = x.shape
    c = kdc.shape[1]
    kdw = kd.shape[2]
    vw = vt.shape[1]
    assert tq & (tq - 1) == 0 and tq % WINDOW == 0
    r = tq // WINDOW
    nblk = s // WINDOW
    nk = tq + 2 * WINDOW + c
    tok = lambda w: pl.BlockSpec((1, tq, w), lambda b, i: (b, i, 0))
    prev_i = lambda i: jnp.maximum(i * r - 1, 0)
    next_i = lambda i: jnp.minimum((i + 1) * r, nblk - 1)
    return pl.pallas_call(
        functools.partial(_attn_a_kernel, tq=tq, seq=s),
        grid=(bsz, s // tq),
        in_specs=[pl.BlockSpec(memory_space=pltpu.SMEM), tok(A_WIDTH),
                  pl.BlockSpec((1, WINDOW, kdw), lambda b, i: (b, prev_i(i), 0)), tok(kdw),
                  pl.BlockSpec((1, WINDOW, kdw), lambda b, i: (b, next_i(i), 0)),
                  pl.BlockSpec((1, c, kdw), lambda b, i: (b, 0, 0)),
                  pl.BlockSpec((1, vw, WINDOW), lambda b, i: (b, 0, prev_i(i))),
                  pl.BlockSpec((1, vw, tq), lambda b, i: (b, 0, i)),
                  pl.BlockSpec((1, vw, WINDOW), lambda b, i: (b, 0, next_i(i))),
                  pl.BlockSpec((1, vw, c), lambda b, i: (b, 0, 0)),
                  tok(A_WIDTH), tok(d), pl.BlockSpec((1, 3, d), lambda b, i: (b, 0, 0)),
                  pl.BlockSpec(wo.shape, lambda b, i: (0, 0))],
        out_specs=tok(d),
        out_shape=jax.ShapeDtypeStruct((bsz, s, d), F32),
        scratch_shapes=[pltpu.VMEM((nk, kdw), BF16), pltpu.VMEM((vw, nk), BF16),
                        pltpu.VMEM((tq + 2 * WINDOW, 2 * tq), F32)] + _attn_a_scratch(nk, tq),
        compiler_params=pltpu.CompilerParams(
            dimension_semantics=("parallel", "parallel"), vmem_limit_bytes=VMEM_LIMIT),
        name="attn_a",
    )(sinks, q, kd, kd, kd, kdc, vt, vt, vt, vtc, gact, x, mod, wo)


def _attn_a_ctx(sinks, qc, kdc, vtc, gc, ctx, mod, wo):
    bsz, c, d = ctx.shape
    tok = lambda w: pl.BlockSpec((1, c, w), lambda b: (b, 0, 0))
    return pl.pallas_call(
        functools.partial(_attn_a_ctx_kernel, tq=c),
        grid=(bsz,),
        in_specs=[pl.BlockSpec(memory_space=pltpu.SMEM), tok(A_WIDTH), tok(kdc.shape[2]),
                  pl.BlockSpec((1, vtc.shape[1], c), lambda b: (b, 0, 0)),
                  tok(A_WIDTH), tok(d), pl.BlockSpec((1, 3, d), lambda b: (b, 0, 0)),
                  pl.BlockSpec(wo.shape, lambda b: (0, 0))],
        out_specs=tok(d),
        out_shape=jax.ShapeDtypeStruct((bsz, c, d), F32),
        scratch_shapes=_attn_a_scratch(c, c),
        compiler_params=pltpu.CompilerParams(
            dimension_semantics=("parallel",), vmem_limit_bytes=VMEM_LIMIT),
        name="attn_a_ctx",
    )(sinks, qc, kdc, vtc, gc, ctx, mod, wo)


def _kext(a, kvg, cos, sin):
    ckv = _rms(a[:, B_Q_RANK:B_Q_RANK + B_KV_RANK], kvg)
    kr = a[:, B_Q_RANK + B_KV_RANK:]
    kr = _rope(kr, cos, sin, B_ROPE // 4)
    return jnp.concatenate([ckv, kr], axis=1).astype(BF16), ckv.T.astype(BF16)


def _proj_b_kernel(x_ref, mod_ref, ng_ref, w1_ref, wg_ref, qg_ref, wqn_ref, wqr_ref, wabs_ref,
                   kvg_ref, c_ref, s_ref, ct_ref, st_ref, qt_ref, k_ref, vt_ref, g_ref,
                   *, qscale, tm):
    mod = mod_ref[0]
    h = (_rms(x_ref[0], ng_ref[...]) * (1.0 + mod[1:2]) + mod[0:1]).astype(BF16)
    a = _dot(h, w1_ref[...])
    k_ref[0], vt_ref[0] = _kext(a, kvg_ref[...], c_ref[...], s_ref[...])
    g_ref[0] = _silu(_dot(h, wg_ref[...])).astype(BF16)
    cq_t = _rms(a[:, :B_Q_RANK], qg_ref[...]).T.astype(BF16)
    qn_t = _dot(wqn_ref[...], cq_t).astype(BF16)
    qr_t = _dot(wqr_ref[...], cq_t)
    ct, st = ct_ref[...], st_ref[...]
    quarter = B_ROPE // 4
    heads_per_chunk = LANES // B_ROPE
    for hd in range(B_HEADS):
        xh = qr_t[hd * B_ROPE:(hd + 1) * B_ROPE]
        rot = jnp.concatenate([-xh[quarter:2 * quarter], xh[:quarter],
                               -xh[3 * quarter:], xh[2 * quarter:3 * quarter]], axis=0)
        rope = ((xh * ct + rot * st) * qscale).astype(BF16)
        j = hd % heads_per_chunk
        for t in range(tm // LANES):
            blk = rope[:, t * LANES:(t + 1) * LANES]
            pieces = [jnp.zeros((j * B_ROPE, LANES), BF16)] if j else []
            pieces.append(blk)
            if j + 1 < heads_per_chunk:
                pieces.append(jnp.zeros(((heads_per_chunk - 1 - j) * B_ROPE, LANES), BF16))
            qt_ref[0, t, LANES:, hd * LANES:(hd + 1) * LANES] = jnp.concatenate(pieces, axis=0)
    for m in range(B_HEADS // 2):
        qa_t = (_dot(wabs_ref[m], qn_t[m * LANES:(m + 1) * LANES]) * qscale).astype(BF16)
        for half in range(2):
            hd = 2 * m + half
            for t in range(tm // LANES):
                qt_ref[0, t, :LANES, hd * LANES:(hd + 1) * LANES] = (
                    qa_t[half * LANES:(half + 1) * LANES, t * LANES:(t + 1) * LANES])


def _proj_b(x, mod, norm_g, w1, wg, qg, wqn_t, wqr_t, wabs_t, kvg, tabs, tabs_t, tm):
    bsz, s, d = x.shape
    full = lambda a: pl.BlockSpec(a.shape, lambda b, t: (0,) * a.ndim)
    tab = pl.BlockSpec((tm, LANES), lambda b, t: (t, 0))
    tab_t = pl.BlockSpec((B_ROPE, tm), lambda b, t: (0, t))
    tok = lambda w: pl.BlockSpec((1, tm, w), lambda b, t: (b, t, 0))
    nt = tm // LANES
    return pl.pallas_call(
        functools.partial(_proj_b_kernel, qscale=(B_NOPE + B_ROPE) ** -0.5 * LOG2E, tm=tm),
        grid=(bsz, s // tm),
        in_specs=[tok(d), pl.BlockSpec((1, 3, d), lambda b, t: (b, 0, 0)), full(norm_g),
                  full(w1), full(wg), full(qg), full(wqn_t), full(wqr_t), full(wabs_t), full(kvg),
                  tab, tab, tab_t, tab_t],
        out_specs=[pl.BlockSpec((1, nt, B_FEAT, B_HEADS * LANES), lambda b, t: (b, t, 0, 0)),
                   tok(B_FEAT), pl.BlockSpec((1, B_KV_RANK, tm), lambda b, t: (b, 0, t)),
                   tok(B_WIDTH)],
        out_shape=[jax.ShapeDtypeStruct((bsz, s // LANES, B_FEAT, B_HEADS * LANES), BF16),
                   jax.ShapeDtypeStruct((bsz, s, B_FEAT), BF16),
                   jax.ShapeDtypeStruct((bsz, B_KV_RANK, s), BF16),
                   jax.ShapeDtypeStruct((bsz, s, B_WIDTH), BF16)],
        compiler_params=pltpu.CompilerParams(
            dimension_semantics=("parallel", "parallel"), vmem_limit_bytes=VMEM_LIMIT),
        name="proj_b",
    )(x, mod, norm_g, w1, wg, qg, wqn_t, wqr_t, wabs_t, kvg, *tabs, *tabs_t)


def _proj_b_ctx_kernel(x_ref, mod_ref, ng_ref, w1_ref, kvg_ref, k_ref, vt_ref):
    mod = mod_ref[0]
    h = (_rms(x_ref[0], ng_ref[...]) * (1.0 + mod[1:2]) + mod[0:1]).astype(BF16)
    k_ref[0], vt_ref[0] = _kext(_dot(h, w1_ref[...]), kvg_ref[...], None, None)


def _proj_b_ctx(ctx, mod, norm_g, w1, kvg):
    bsz, c, d = ctx.shape
    full = lambda a: pl.BlockSpec(a.shape, lambda b: (0,) * a.ndim)
    return pl.pallas_call(
        _proj_b_ctx_kernel,
        grid=(bsz,),
        in_specs=[pl.BlockSpec((1, c, d), lambda b: (b, 0, 0)),
                  pl.BlockSpec((1, 3, d), lambda b: (b, 0, 0)), full(norm_g), full(w1), full(kvg)],
        out_specs=[pl.BlockSpec((1, c, B_FEAT), lambda b: (b, 0, 0)),
                   pl.BlockSpec((1, B_KV_RANK, c), lambda b: (b, 0, 0))],
        out_shape=[jax.ShapeDtypeStruct((bsz, c, B_FEAT), BF16),
                   jax.ShapeDtypeStruct((bsz, B_KV_RANK, c), BF16)],
        compiler_params=pltpu.CompilerParams(dimension_semantics=("parallel",)),
        name="proj_b_ctx",
    )(ctx, mod, norm_g, w1, kvg)


def _flash_b_kernel(q_ref, k_ref, vt_ref, g_ref, x_ref, mod_ref, wvt_ref, wo_ref, fg_ref, out_ref,
                    m_ref, l_ref, acc_ref, s_ref, mx_ref, *, tiles, tk, nk, hb):
    tq = LANES
    rt = hb * tq
    m_ref[...] = jnp.full(m_ref.shape, NEG_INF, F32)
    l_ref[...] = jnp.zeros(l_ref.shape, F32)
    acc_ref[...] = jnp.zeros(acc_ref.shape, F32)

    hgroups = B_HEADS // hb
    nsb = tiles * hgroups
    nsteps = (nk // tk) * nsb
    unroll = max(u for u in range(2, MAX_UNROLL + 1, 2) if nsteps % u == 0)
    assert nsb & (nsb - 1) == 0 and hgroups & (hgroups - 1) == 0
    shift = nsb.bit_length() - 1
    hshift = hgroups.bit_length() - 1

    def split(k):
        return lax.shift_right_logical(k, shift), lax.bitwise_and(k, nsb - 1)

    def score_stage(k, slot):
        j, t = split(k)
        kc = k_ref[0, pl.ds(pl.multiple_of(j * tk, tk), tk), :]
        tile, hg = lax.shift_right_logical(t, hshift), lax.bitwise_and(t, hgroups - 1)
        st = _dot(kc, q_ref[0, tile, :, pl.ds(pl.multiple_of(hg * rt, rt), rt)])
        s_ref[slot] = st
        mx_ref[slot] = jnp.max(st.reshape(tk // 8, 8, rt), axis=0)

    def update_stage(k, slot):
        j, t = split(k)
        vt = vt_ref[0, :, pl.ds(pl.multiple_of(j * tk, tk), tk)]
        m_old = m_ref[t]
        m_new = jnp.maximum(m_old, jnp.max(mx_ref[slot], axis=0, keepdims=True))
        alpha = jnp.exp2(m_old - m_new)
        e = jnp.exp2(s_ref[slot] - m_new[0:1])
        l_ref[t] = alpha * l_ref[t] + jnp.sum(e.reshape(tk // 8, 8, rt), axis=0)
        acc_ref[t] = alpha[0:1] * acc_ref[t] + _dot(vt, e.astype(BF16))
        m_ref[t] = m_new

    score_stage(0, 0)

    def body(i, carry):
        for u in range(unroll):
            k = unroll * i + u
            score_stage(jnp.minimum(k + 1, nsteps - 1), (u + 1) % 2)
            update_stage(k, u % 2)
        return carry

    lax.fori_loop(0, nsteps // unroll, body, 0)
    rows = []
    for tile in range(tiles):
        outs = []
        for hg in range(hgroups):
            t = tile * hgroups + hg
            linv = 1.0 / jnp.sum(l_ref[t], axis=0, keepdims=True)
            ot = (acc_ref[t] * linv).astype(BF16)
            for p in range(hb // 2):
                pair = jnp.concatenate([ot[:, (2 * p) * tq:(2 * p + 1) * tq],
                                        ot[:, (2 * p + 1) * tq:(2 * p + 2) * tq]], axis=0)
                outs.append(_dot(wvt_ref[hg * (hb // 2) + p], pair))
        rows.append(jnp.concatenate(outs, axis=0).T)
    o = jnp.concatenate(rows, axis=0)
    z = (o * g_ref[0].astype(F32)).astype(BF16)
    xo = x_ref[0] + mod_ref[0][2:3] * _dot(z, wo_ref[...])
    out_ref[0] = _rms(xo, fg_ref[...])


def _flash_b(qp, kall, vtall, gact, x, mod, wvt, wo, final_g, max_tk, hb, tiles):
    bsz, s, d = x.shape
    nk = kall.shape[1]
    tq = tiles * LANES
    tk = max(t for t in range(LANES, max_tk + 1, LANES) if nk % t == 0)
    assert hb % 2 == 0 and B_HEADS % hb == 0
    nt = tiles * (B_HEADS // hb)
    tok = lambda w: pl.BlockSpec((1, tq, w), lambda b, i: (b, i, 0))
    full = lambda a: pl.BlockSpec(a.shape, lambda b, i: (0,) * a.ndim)
    return pl.pallas_call(
        functools.partial(_flash_b_kernel, tiles=tiles, tk=tk, nk=nk, hb=hb),
        grid=(bsz, s // tq),
        in_specs=[pl.BlockSpec((1, tiles, B_FEAT, B_HEADS * LANES), lambda b, i: (b, i, 0, 0)),
                  pl.BlockSpec((1, nk, B_FEAT), lambda b, i: (b, 0, 0)),
                  pl.BlockSpec((1, B_KV_RANK, nk), lambda b, i: (b, 0, 0)),
                  tok(B_WIDTH), tok(d), pl.BlockSpec((1, 3, d), lambda b, i: (b, 0, 0)),
                  full(wvt), full(wo), full(final_g)],
        out_specs=tok(d),
        out_shape=jax.ShapeDtypeStruct((bsz, s, d), F32),
        scratch_shapes=[pltpu.VMEM((nt, 8, hb * LANES), F32), pltpu.VMEM((nt, 8, hb * LANES), F32),
                        pltpu.VMEM((nt, B_KV_RANK, hb * LANES), F32),
                        pltpu.VMEM((2, tk, hb * LANES), F32), pltpu.VMEM((2, 8, hb * LANES), F32)],
        compiler_params=pltpu.CompilerParams(
            dimension_semantics=("parallel", "parallel"), vmem_limit_bytes=VMEM_LIMIT),
        name="flash_b",
    )(qp, kall, vtall, gact, x, mod, wvt, wo, final_g)


def _rope_cos_sin(n_tokens, rot_dim):
    n_rows = n_tokens // GRID_W
    row = jnp.broadcast_to(jnp.arange(n_rows)[:, None], (n_rows, GRID_W)).reshape(-1)
    col = jnp.broadcast_to(jnp.arange(GRID_W)[None, :], (n_rows, GRID_W)).reshape(-1)
    nf = rot_dim // 4
    inv = ROPE_BASE ** (-jnp.arange(nf, dtype=F32) / nf)
    ar = row.astype(F32)[:, None] * inv
    ac = col.astype(F32)[:, None] * inv
    ang = jnp.concatenate([ar, ar, ac, ac], axis=-1)
    return jnp.cos(ang), jnp.sin(ang)


def _rope_tables(n_tokens, rot_dim):
    reps = LANES // rot_dim
    cos, sin = _rope_cos_sin(n_tokens, rot_dim)
    return jnp.tile(cos, (1, reps)), jnp.tile(sin, (1, reps))


def _split_mod(modout, bsz):
    d = modout.shape[1] // 3
    m = modout.reshape(8, 3, d)
    lat = m[:bsz]
    ctx = jnp.broadcast_to(m[bsz:bsz + 1], (bsz, 3, d))
    return lat, ctx


def kernel(x, c, ctx, c_ctx, norm_g_0, ada_w_0, ada_b_0, a_w_in_0, a_sinks_0, a_w_o_0, norm_g_1,
           ada_w_1, ada_b_1, b_w_in_1, b_q_norm_1, b_w_uq_1, b_kv_norm_1, b_w_ukv_1, b_w_o_1, final_g):
    bsz, s, d = x.shape
    n_ctx = ctx.shape[1]
    assert bsz + 1 <= 8
    cc = jnp.concatenate([c, c_ctx[None], jnp.zeros((8 - bsz - 1, d), F32)], axis=0)
    row = lambda g: g.reshape(1, -1)

    mod_x, mod_c = _split_mod(_modulation(cc, ada_w_0, ada_b_0), bsz)
    w = a_w_in_0.astype(BF16)
    wq, wk = w[:, :A_WIDTH], w[:, A_WIDTH:A_WIDTH + A_KV_WIDTH]
    wv, wg = w[:, A_WIDTH + A_KV_WIDTH:A_WIDTH + 2 * A_KV_WIDTH], w[:, A_WIDTH + 2 * A_KV_WIDTH:]
    tabs_a = _rope_tables(s, A_HEAD_DIM)
    wo_a = a_w_o_0.astype(BF16)
    q, kd, vt, gact = _proj_a(x, mod_x, row(norm_g_0), wq, wk, wv, wg, tabs_a, tm=1024)
    qc, kdc, vtc, gc = _proj_a(ctx, mod_c, row(norm_g_0), wq, wk, wv, wg, (), tm=n_ctx)
    x1 = _attn_a(a_sinks_0, q, kd, vt, kdc, vtc, gact, x, mod_x, wo_a, tq=256)
    ctx1 = _attn_a_ctx(a_sinks_0, qc, kdc, vtc, gc, ctx, mod_c, wo_a)

    mod_x, mod_c = _split_mod(_modulation(cc, ada_w_1, ada_b_1), bsz)
    wb = b_w_in_1.astype(BF16)
    n_kv = B_Q_RANK + B_KV_RANK
    w1 = jnp.concatenate([wb[:, :n_kv]] + [wb[:, n_kv:n_kv + B_ROPE]] * (LANES // B_ROPE), axis=1)
    wgb = wb[:, n_kv + B_ROPE:]
    wuq = b_w_uq_1.astype(BF16).reshape(B_Q_RANK, B_HEADS, B_NOPE + B_ROPE)
    wqn_t = wuq[:, :, :B_NOPE].reshape(B_Q_RANK, B_HEADS * B_NOPE).T
    wqr_t = wuq[:, :, B_NOPE:].reshape(B_Q_RANK, B_HEADS * B_ROPE).T
    wukv = b_w_ukv_1.astype(BF16).reshape(B_KV_RANK, B_HEADS, B_NOPE + B_V)
    wkn_t = jnp.transpose(wukv[:, :, :B_NOPE], (1, 2, 0)).reshape(B_HEADS // 2, 2, B_NOPE, B_KV_RANK)
    zk = jnp.zeros_like(wkn_t[:, 0])
    wabs_t = jnp.transpose(
        jnp.concatenate([jnp.concatenate([wkn_t[:, 0], zk], axis=2),
                         jnp.concatenate([zk, wkn_t[:, 1]], axis=2)], axis=1), (0, 2, 1))
    wvh = jnp.transpose(wukv[:, :, B_NOPE:], (1, 0, 2)).reshape(B_HEADS // 2, 2, B_KV_RANK, B_V)
    zv = jnp.zeros_like(wvh[:, 0])
    wvup_t = jnp.transpose(
        jnp.concatenate([jnp.concatenate([wvh[:, 0], zv], axis=2),
                         jnp.concatenate([zv, wvh[:, 1]], axis=2)], axis=1), (0, 2, 1))
    tabs_b = _rope_tables(s, B_ROPE)
    cos_b, sin_b = _rope_cos_sin(s, B_ROPE)
    qp, kx, vtx, gb = _proj_b(x1, mod_x, row(norm_g_1), w1, wgb, row(b_q_norm_1), wqn_t, wqr_t,
                              wabs_t, row(b_kv_norm_1), tabs_b, (cos_b.T, sin_b.T), tm=512)
    kc, vtc = _proj_b_ctx(ctx1, mod_c, row(norm_g_1), w1, row(b_kv_norm_1))
    kall = jnp.concatenate([kc, kx], axis=1)
    vtall = jnp.concatenate([vtc, vtx], axis=2)
    return _flash_b(qp, kall, vtall, gb, x1, mod_x, wvup_t, b_w_o_1.astype(BF16), row(final_g),
                    max_tk=1408, hb=4, tiles=4)
```

```python
import functools
import math

import jax
import jax.numpy as jnp
from jax import lax
from jax.experimental import pallas as pl
from jax.experimental.pallas import tpu as pltpu

F32 = jnp.float32
BF16 = jnp.bfloat16

GRID_W = 64
ROPE_BASE = 10000.0
EPS = 1e-6
NEG_INF = -1e30
LOG2E = 1.4426950408889634
LANES = 128
WINDOW = 128

A_HEADS, A_KV_HEADS, A_HEAD_DIM = 16, 4, 64
A_WIDTH = A_HEADS * A_HEAD_DIM
A_KV_WIDTH = A_KV_HEADS * A_HEAD_DIM
B_HEADS, B_NOPE, B_ROPE, B_V = 16, 64, 32, 64
B_Q_RANK, B_KV_RANK = 256, 128
B_WIDTH = B_HEADS * B_V
B_FEAT = 2 * LANES

VMEM_LIMIT = 56 * 1024 * 1024
MAX_UNROLL = 24
B_QTILE = 512


def _silu(x):
    return x * (1.0 / (1.0 + jnp.exp(-x)))


def _rms(x, g):
    return x * lax.rsqrt(jnp.mean(x * x, axis=-1, keepdims=True) + EPS) * g


def _rope(x, cos, sin, quarter):
    if cos is None:
        return x
    first = (lax.broadcasted_iota(jnp.int32, x.shape, 1) & (2 * quarter - 1)) < quarter
    rot = jnp.where(first, -pltpu.roll(x, LANES - quarter, 1), pltpu.roll(x, quarter, 1))
    return x * cos + rot * sin


def _dot(a, b):
    return jnp.dot(a, b, preferred_element_type=F32)


def _dot_nt(a, b):
    return lax.dot_general(a, b, (((1,), (1,)), ((), ())), preferred_element_type=F32)


def _mod_kernel(c_ref, w_ref, b_ref, o_ref):
    a = _silu(c_ref[...])
    o_ref[...] = jnp.dot(a, w_ref[...], precision=lax.Precision.HIGHEST,
                         preferred_element_type=F32) + b_ref[...]


def _modulation(cc, ada_w, ada_b):
    d = ada_w.shape[0]
    n = ada_w.shape[1]
    return pl.pallas_call(
        _mod_kernel,
        grid=(n // d,),
        in_specs=[pl.BlockSpec((8, d), lambda j: (0, 0)),
                  pl.BlockSpec((d, d), lambda j: (0, j)),
                  pl.BlockSpec((1, d), lambda j: (0, j))],
        out_specs=pl.BlockSpec((8, d), lambda j: (0, j)),
        out_shape=jax.ShapeDtypeStruct((8, n), F32),
        name="adaln_mod",
    )(cc, ada_w, ada_b.reshape(1, n))


def _proj_a_kernel(x_ref, mod_ref, ng_ref, wq_ref, wk_ref, wv_ref, wg_ref, *rest, qscale, rope):
    cos, sin = (rest[0][...], rest[1][...]) if rope else (None, None)
    q_ref, kd_ref, vt_ref, g_ref = rest[-4:]
    mod = mod_ref[0]
    h = (_rms(x_ref[0], ng_ref[...]) * (1.0 + mod[1:2]) + mod[0:1]).astype(BF16)
    quarter = A_HEAD_DIM // 4
    q = _dot(h, wq_ref[...])
    for j in range(A_WIDTH // LANES):
        sl = slice(j * LANES, (j + 1) * LANES)
        q_ref[0, :, sl] = (_rope(q[:, sl], cos, sin, quarter) * qscale).astype(BF16)
    lo = lax.broadcasted_iota(jnp.int32, (q.shape[0], LANES), 1) < A_HEAD_DIM
    k = _dot(h, wk_ref[...])
    for m in range(A_KV_WIDTH // LANES):
        sl = slice(m * LANES, (m + 1) * LANES)
        kp = _rope(k[:, sl], cos, sin, quarter)
        ks = pltpu.roll(kp, A_HEAD_DIM, 1)
        kd_ref[0, :, 2 * m * LANES:(2 * m + 1) * LANES] = jnp.where(lo, kp, ks).astype(BF16)
        kd_ref[0, :, (2 * m + 1) * LANES:(2 * m + 2) * LANES] = jnp.where(lo, ks, kp).astype(BF16)
    vt_ref[0] = _dot(h, wv_ref[...]).astype(BF16).T
    g_ref[0] = _silu(_dot(h, wg_ref[...])).astype(BF16)


def _proj_a(x, mod, norm_g, wq, wk, wv, wg, tabs, tm):
    bsz, s, d = x.shape
    kdw = A_KV_HEADS * LANES
    full = lambda a: pl.BlockSpec(a.shape, lambda b, t: (0,) * a.ndim)
    tab = pl.BlockSpec((tm, LANES), lambda b, t: (t, 0))
    tok = lambda w: pl.BlockSpec((1, tm, w), lambda b, t: (b, t, 0))
    return pl.pallas_call(
        functools.partial(_proj_a_kernel, qscale=A_HEAD_DIM ** -0.5 * LOG2E, rope=bool(tabs)),
        grid=(bsz, s // tm),
        in_specs=[tok(d), pl.BlockSpec((1, 3, d), lambda b, t: (b, 0, 0)), full(norm_g),
                  full(wq), full(wk), full(wv), full(wg)] + [tab] * len(tabs),
        out_specs=[tok(A_WIDTH), tok(kdw),
                   pl.BlockSpec((1, A_KV_WIDTH, tm), lambda b, t: (b, 0, t)), tok(A_WIDTH)],
        out_shape=[jax.ShapeDtypeStruct((bsz, s, A_WIDTH), BF16),
                   jax.ShapeDtypeStruct((bsz, s, kdw), BF16),
                   jax.ShapeDtypeStruct((bsz, A_KV_WIDTH, s), BF16),
                   jax.ShapeDtypeStruct((bsz, s, A_WIDTH), BF16)],
        compiler_params=pltpu.CompilerParams(
            dimension_semantics=("parallel", "parallel"), vmem_limit_bytes=VMEM_LIMIT),
        name="proj_a",
    )(x, mod, norm_g, wq, wk, wv, wg, *tabs)


def _attn_a_core(sink_ref, q, kd, vt, cap_loc, tq):
    group = A_HEADS // A_KV_HEADS
    nk = kd.shape[0]
    lane = lax.broadcasted_iota(jnp.int32, (1, 2 * tq), 1)
    klane = lax.broadcasted_iota(jnp.int32, (nk, LANES), 1)
    steps = [(g, par) for g in range(A_KV_HEADS) for par in range(2)]

    def scores(g, par):
        qrows = jnp.concatenate([q[:, (2 * g) * LANES:(2 * g + 1) * LANES],
                                 q[:, (2 * g + 1) * LANES:(2 * g + 2) * LANES]], axis=0)
        kg = kd[:, g * LANES:(g + 1) * LANES]
        keep = (klane < A_HEAD_DIM) if par == 0 else (klane >= A_HEAD_DIM)
        return _dot_nt(jnp.where(keep, kg, jnp.zeros_like(kg)), qrows)

    def finish(g, par, st):
        h0 = group * g + par
        sink = jnp.where(lane < tq, sink_ref[h0] * LOG2E, sink_ref[h0 + 2] * LOG2E)
        if cap_loc is not None:
            nl = cap_loc.shape[0]
            st = jnp.concatenate([jnp.minimum(st[:nl], cap_loc), st[nl:]], axis=0)
        m = jnp.maximum(sink, jnp.max(st, axis=0, keepdims=True))
        e = jnp.exp2(st - m)
        l = jnp.exp2(sink - m) + jnp.sum(e, axis=0, keepdims=True)
        ot = _dot(vt[g * A_HEAD_DIM:(g + 1) * A_HEAD_DIM], e.astype(BF16)) * (1.0 / l)
        return {h0: ot[:, :tq], h0 + 2: ot[:, tq:]}

    heads = {}
    cur = scores(*steps[0])
    for i, (g, par) in enumerate(steps):
        nxt = scores(*steps[i + 1]) if i + 1 < len(steps) else None
        heads.update(finish(g, par, cur))
        cur = nxt
    return jnp.concatenate([heads[h] for h in range(A_HEADS)], axis=0).T


def _attn_a_finish(o, g_ref, x_ref, mod_ref, wo_ref, out_ref):
    z = (o * g_ref[0].astype(F32)).astype(BF16)
    out_ref[0] = x_ref[0] + mod_ref[0][2:3] * _dot(z, wo_ref[...])


def _attn_a_kernel(sink_ref, q_ref, kp_ref, kc_ref, kn_ref, kx_ref, vp_ref, vc_ref, vn_ref, vx_ref,
                   g_ref, x_ref, mod_ref, wo_ref, out_ref, *, tq, seq):
    i = pl.program_id(1)
    kd = jnp.concatenate([kp_ref[0], kc_ref[0], kn_ref[0], kx_ref[0]], axis=0)
    vt = jnp.concatenate([vp_ref[0], vc_ref[0], vn_ref[0], vx_ref[0]], axis=1)
    nl = tq + 2 * WINDOW
    krow = lax.broadcasted_iota(jnp.int32, (nl, 2 * tq), 0)
    qcol = lax.broadcasted_iota(jnp.int32, (nl, 2 * tq), 1) & (tq - 1)
    kpos = i * tq - WINDOW + krow
    ok = (jnp.abs(qcol + WINDOW - krow) <= WINDOW) & (kpos >= 0) & (kpos < seq)
    o = _attn_a_core(sink_ref, q_ref[0], kd, vt, jnp.where(ok, jnp.inf, NEG_INF), tq)
    _attn_a_finish(o, g_ref, x_ref, mod_ref, wo_ref, out_ref)


def _attn_a_ctx_kernel(sink_ref, q_ref, kx_ref, vx_ref, g_ref, x_ref, mod_ref, wo_ref, out_ref, *, tq):
    o = _attn_a_core(sink_ref, q_ref[0], kx_ref[0], vx_ref[0], None, tq)
    _attn_a_finish(o, g_ref, x_ref, mod_ref, wo_ref, out_ref)


def _attn_a(sinks, q, kd, vt, kdc, vtc, gact, x, mod, wo, tq):
    bsz, s, d = x.shape
    c = kdc.shape[1]
    kdw = kd.shape[2]
    vw = vt.shape[1]
    assert tq & (tq - 1) == 0 and tq % WINDOW == 0
    r = tq // WINDOW
    nblk = s // WINDOW
    tok = lambda w: pl.BlockSpec((1, tq, w), lambda b, i: (b, i, 0))
    prev_i = lambda i: jnp.maximum(i * r - 1, 0)
    next_i = lambda i: jnp.minimum((i + 1) * r, nblk - 1)
    return pl.pallas_call(
        functools.partial(_attn_a_kernel, tq=tq, seq=s),
        grid=(bsz, s // tq),
        in_specs=[pl.BlockSpec(memory_space=pltpu.SMEM), tok(A_WIDTH),
                  pl.BlockSpec((1, WINDOW, kdw), lambda b, i: (b, prev_i(i), 0)), tok(kdw),
                  pl.BlockSpec((1, WINDOW, kdw), lambda b, i: (b, next_i(i), 0)),
                  pl.BlockSpec((1, c, kdw), lambda b, i: (b, 0, 0)),
                  pl.BlockSpec((1, vw, WINDOW), lambda b, i: (b, 0, prev_i(i))),
                  pl.BlockSpec((1, vw, tq), lambda b, i: (b, 0, i)),
                  pl.BlockSpec((1, vw, WINDOW), lambda b, i: (b, 0, next_i(i))),
                  pl.BlockSpec((1, vw, c), lambda b, i: (b, 0, 0)),
                  tok(A_WIDTH), tok(d), pl.BlockSpec((1, 3, d), lambda b, i: (b, 0, 0)),
                  pl.BlockSpec(wo.shape, lambda b, i: (0, 0))],
        out_specs=tok(d),
        out_shape=jax.ShapeDtypeStruct((bsz, s, d), F32),
        compiler_params=pltpu.CompilerParams(
            dimension_semantics=("parallel", "parallel"), vmem_limit_bytes=VMEM_LIMIT),
        name="attn_a",
    )(sinks, q, kd, kd, kd, kdc, vt, vt, vt, vtc, gact, x, mod, wo)


def _attn_a_ctx(sinks, qc, kdc, vtc, gc, ctx, mod, wo):
    bsz, c, d = ctx.shape
    tok = lambda w: pl.BlockSpec((1, c, w), lambda b: (b, 0, 0))
    return pl.pallas_call(
        functools.partial(_attn_a_ctx_kernel, tq=c),
        grid=(bsz,),
        in_specs=[pl.BlockSpec(memory_space=pltpu.SMEM), tok(A_WIDTH), tok(kdc.shape[2]),
                  pl.BlockSpec((1, vtc.shape[1], c), lambda b: (b, 0, 0)),
                  tok(A_WIDTH), tok(d), pl.BlockSpec((1, 3, d), lambda b: (b, 0, 0)),
                  pl.BlockSpec(wo.shape, lambda b: (0, 0))],
        out_specs=tok(d),
        out_shape=jax.ShapeDtypeStruct((bsz, c, d), F32),
        compiler_params=pltpu.CompilerParams(
            dimension_semantics=("parallel",), vmem_limit_bytes=VMEM_LIMIT),
        name="attn_a_ctx",
    )(sinks, qc, kdc, vtc, gc, ctx, mod, wo)


def _kext(a, kvg, wv_t, cos, sin):
    ckv = _rms(a[:, B_Q_RANK:B_Q_RANK + B_KV_RANK], kvg)
    kr = a[:, B_Q_RANK + B_KV_RANK:]
    kr = _rope(kr, cos, sin, B_ROPE // 4)
    vh_t = _dot(wv_t, ckv.T.astype(BF16)).astype(BF16)
    return jnp.concatenate([ckv, kr], axis=1).astype(BF16), vh_t


def _proj_b_kernel(x_ref, mod_ref, ng_ref, w1_ref, wg_ref, qg_ref, wqn_ref, wqr_ref, wabs_ref,
                   kvg_ref, wv_ref, c_ref, s_ref, ct_ref, st_ref, qt_ref, k_ref, vt_ref, g_ref,
                   *, qscale, tm):
    mod = mod_ref[0]
    h = (_rms(x_ref[0], ng_ref[...]) * (1.0 + mod[1:2]) + mod[0:1]).astype(BF16)
    a = _dot(h, w1_ref[...])
    k_ref[0], vt_ref[0] = _kext(a, kvg_ref[...], wv_ref[...], c_ref[...], s_ref[...])
    g_ref[0] = _silu(_dot(h, wg_ref[...])).astype(BF16)
    cq_t = _rms(a[:, :B_Q_RANK], qg_ref[...]).T.astype(BF16)
    qn_t = _dot(wqn_ref[...], cq_t).astype(BF16)
    qr_t = _dot(wqr_ref[...], cq_t)
    ct, st = ct_ref[...], st_ref[...]
    quarter = B_ROPE // 4
    heads_per_chunk = LANES // B_ROPE
    for hd in range(B_HEADS):
        xh = qr_t[hd * B_ROPE:(hd + 1) * B_ROPE]
        rot = jnp.concatenate([-xh[quarter:2 * quarter], xh[:quarter],
                               -xh[3 * quarter:], xh[2 * quarter:3 * quarter]], axis=0)
        rope = ((xh * ct + rot * st) * qscale).astype(BF16)
        j = hd % heads_per_chunk
        for t in range(tm // B_QTILE):
            blk = rope[:, t * B_QTILE:(t + 1) * B_QTILE]
            pieces = [jnp.zeros((j * B_ROPE, B_QTILE), BF16)] if j else []
            pieces.append(blk)
            if j + 1 < heads_per_chunk:
                pieces.append(jnp.zeros(((heads_per_chunk - 1 - j) * B_ROPE, B_QTILE), BF16))
            qt_ref[0, t, LANES:, hd * B_QTILE:(hd + 1) * B_QTILE] = jnp.concatenate(pieces, axis=0)
    for m in range(B_HEADS // 2):
        qa_t = (_dot(wabs_ref[m], qn_t[m * LANES:(m + 1) * LANES]) * qscale).astype(BF16)
        for half in range(2):
            hd = 2 * m + half
            for t in range(tm // B_QTILE):
                qt_ref[0, t, :LANES, hd * B_QTILE:(hd + 1) * B_QTILE] = (
                    qa_t[half * LANES:(half + 1) * LANES, t * B_QTILE:(t + 1) * B_QTILE])


def _proj_b(x, mod, norm_g, w1, wg, qg, wqn_t, wqr_t, wabs_t, kvg, wv_t, tabs, tabs_t, tm):
    bsz, s, d = x.shape
    full = lambda a: pl.BlockSpec(a.shape, lambda b, t: (0,) * a.ndim)
    tab = pl.BlockSpec((tm, LANES), lambda b, t: (t, 0))
    tab_t = pl.BlockSpec((B_ROPE, tm), lambda b, t: (0, t))
    tok = lambda w: pl.BlockSpec((1, tm, w), lambda b, t: (b, t, 0))
    nt = tm // B_QTILE
    return pl.pallas_call(
        functools.partial(_proj_b_kernel, qscale=(B_NOPE + B_ROPE) ** -0.5 * LOG2E, tm=tm),
        grid=(bsz, s // tm),
        in_specs=[tok(d), pl.BlockSpec((1, 3, d), lambda b, t: (b, 0, 0)), full(norm_g),
                  full(w1), full(wg), full(qg), full(wqn_t), full(wqr_t), full(wabs_t), full(kvg),
                  full(wv_t), tab, tab, tab_t, tab_t],
        out_specs=[pl.BlockSpec((1, nt, B_FEAT, B_HEADS * B_QTILE), lambda b, t: (b, t, 0, 0)),
                   tok(B_FEAT), pl.BlockSpec((1, B_WIDTH, tm), lambda b, t: (b, 0, t)),
                   tok(B_WIDTH)],
        out_shape=[jax.ShapeDtypeStruct((bsz, s // B_QTILE, B_FEAT, B_HEADS * B_QTILE), BF16),
                   jax.ShapeDtypeStruct((bsz, s, B_FEAT), BF16),
                   jax.ShapeDtypeStruct((bsz, B_WIDTH, s), BF16),
                   jax.ShapeDtypeStruct((bsz, s, B_WIDTH), BF16)],
        compiler_params=pltpu.CompilerParams(
            dimension_semantics=("parallel", "parallel"), vmem_limit_bytes=VMEM_LIMIT),
        name="proj_b",
    )(x, mod, norm_g, w1, wg, qg, wqn_t, wqr_t, wabs_t, kvg, wv_t, *tabs, *tabs_t)


def _proj_b_ctx_kernel(x_ref, mod_ref, ng_ref, w1_ref, kvg_ref, wv_ref, k_ref, vt_ref):
    mod = mod_ref[0]
    h = (_rms(x_ref[0], ng_ref[...]) * (1.0 + mod[1:2]) + mod[0:1]).astype(BF16)
    k_ref[0], vt_ref[0] = _kext(_dot(h, w1_ref[...]), kvg_ref[...], wv_ref[...], None, None)


def _proj_b_ctx(ctx, mod, norm_g, w1, kvg, wv_t):
    bsz, c, d = ctx.shape
    full = lambda a: pl.BlockSpec(a.shape, lambda b: (0,) * a.ndim)
    return pl.pallas_call(
        _proj_b_ctx_kernel,
        grid=(bsz,),
        in_specs=[pl.BlockSpec((1, c, d), lambda b: (b, 0, 0)),
                  pl.BlockSpec((1, 3, d), lambda b: (b, 0, 0)), full(norm_g), full(w1), full(kvg),
                  full(wv_t)],
        out_specs=[pl.BlockSpec((1, c, B_FEAT), lambda b: (b, 0, 0)),
                   pl.BlockSpec((1, B_WIDTH, c), lambda b: (b, 0, 0))],
        out_shape=[jax.ShapeDtypeStruct((bsz, c, B_FEAT), BF16),
                   jax.ShapeDtypeStruct((bsz, B_WIDTH, c), BF16)],
        compiler_params=pltpu.CompilerParams(dimension_semantics=("parallel",)),
        name="proj_b_ctx",
    )(ctx, mod, norm_g, w1, kvg, wv_t)


def _flash_b_kernel(q_ref, k_ref, vt_ref, g_ref, x_ref, mod_ref, wo_ref, fg_ref, out_ref,
                    m_ref, l_ref, acc_ref, s_ref, mx_ref, *, tk, nk):
    rt = B_QTILE
    m_ref[...] = jnp.full(m_ref.shape, NEG_INF, F32)
    l_ref[...] = jnp.zeros(l_ref.shape, F32)
    acc_ref[...] = jnp.zeros(acc_ref.shape, F32)

    nsb = B_HEADS
    nsteps = (nk // tk) * nsb
    unroll = max(u for u in range(2, MAX_UNROLL + 1, 2) if nsteps % u == 0)
    shift = nsb.bit_length() - 1

    def split(k):
        return lax.shift_right_logical(k, shift), lax.bitwise_and(k, nsb - 1)

    def score_stage(k, slot):
        j, t = split(k)
        kc = k_ref[0, pl.ds(pl.multiple_of(j * tk, tk), tk), :]
        st = _dot(kc, q_ref[0, :, pl.ds(pl.multiple_of(t * rt, rt), rt)])
        s_ref[slot] = st
        mx_ref[slot] = jnp.max(st.reshape(tk // 8, 8, rt), axis=0)

    def update_stage(k, slot):
        j, t = split(k)
        vt = vt_ref[0, pl.ds(pl.multiple_of(t * B_V, B_V), B_V),
                    pl.ds(pl.multiple_of(j * tk, tk), tk)]
        m_old = m_ref[t]
        m_new = jnp.maximum(m_old, jnp.max(mx_ref[slot], axis=0, keepdims=True))
        alpha = jnp.exp2(m_old - m_new)
        e = jnp.exp2(s_ref[slot] - m_new[0:1])
        l_ref[t] = alpha * l_ref[t] + jnp.sum(e.reshape(tk // 8, 8, rt), axis=0)
        acc_ref[t] = alpha[0:1] * acc_ref[t] + _dot(vt, e.astype(BF16))
        m_ref[t] = m_new

    score_stage(0, 0)

    def body(i, carry):
        for u in range(unroll):
            k = unroll * i + u
            score_stage(jnp.minimum(k + 1, nsteps - 1), (u + 1) % 2)
            update_stage(k, u % 2)
        return carry

    lax.fori_loop(0, nsteps // unroll, body, 0)
    outs = []
    for t in range(B_HEADS):
        linv = 1.0 / jnp.sum(l_ref[t], axis=0, keepdims=True)
        outs.append(acc_ref[t] * linv)
    o = jnp.concatenate(outs, axis=0).T
    z = (o * g_ref[0].astype(F32)).astype(BF16)
    xo = x_ref[0] + mod_ref[0][2:3] * _dot(z, wo_ref[...])
    out_ref[0] = _rms(xo, fg_ref[...])


def _flash_b(qp, kall, vtall, gact, x, mod, wo, final_g, max_tk):
    bsz, s, d = x.shape
    nk = kall.shape[1]
    tq = B_QTILE
    tk = max(t for t in range(LANES, max_tk + 1, LANES) if nk % t == 0)
    tok = lambda w: pl.BlockSpec((1, tq, w), lambda b, i: (b, i, 0))
    full = lambda a: pl.BlockSpec(a.shape, lambda b, i: (0,) * a.ndim)
    once = dict(pipeline_mode=pl.Buffered(1))
    return pl.pallas_call(
        functools.partial(_flash_b_kernel, tk=tk, nk=nk),
        grid=(bsz, s // tq),
        in_specs=[pl.BlockSpec((1, None, B_FEAT, B_HEADS * tq), lambda b, i: (b, i, 0, 0)),
                  pl.BlockSpec((1, nk, B_FEAT), lambda b, i: (b, 0, 0), **once),
                  pl.BlockSpec((1, B_WIDTH, nk), lambda b, i: (b, 0, 0), **once),
                  tok(B_WIDTH), tok(d), pl.BlockSpec((1, 3, d), lambda b, i: (b, 0, 0)),
                  pl.BlockSpec(wo.shape, lambda b, i: (0, 0), **once), full(final_g)],
        out_specs=tok(d),
        out_shape=jax.ShapeDtypeStruct((bsz, s, d), F32),
        scratch_shapes=[pltpu.VMEM((B_HEADS, 8, tq), F32), pltpu.VMEM((B_HEADS, 8, tq), F32),
                        pltpu.VMEM((B_HEADS, B_V, tq), F32),
                        pltpu.VMEM((2, tk, tq), F32), pltpu.VMEM((2, 8, tq), F32)],
        compiler_params=pltpu.CompilerParams(
            dimension_semantics=("parallel", "parallel"), vmem_limit_bytes=VMEM_LIMIT),
        name="flash_b",
    )(qp, kall, vtall, gact, x, mod, wo, final_g)


def _rope_cos_sin(n_tokens, rot_dim):
    n_rows = n_tokens // GRID_W
    row = jnp.broadcast_to(jnp.arange(n_rows)[:, None], (n_rows, GRID_W)).reshape(-1)
    col = jnp.broadcast_to(jnp.arange(GRID_W)[None, :], (n_rows, GRID_W)).reshape(-1)
    nf = rot_dim // 4
    inv = ROPE_BASE ** (-jnp.arange(nf, dtype=F32) / nf)
    ar = row.astype(F32)[:, None] * inv
    ac = col.astype(F32)[:, None] * inv
    ang = jnp.concatenate([ar, ar, ac, ac], axis=-1)
    return jnp.cos(ang), jnp.sin(ang)


def _rope_tables(n_tokens, rot_dim):
    reps = LANES // rot_dim
    cos, sin = _rope_cos_sin(n_tokens, rot_dim)
    return jnp.tile(cos, (1, reps)), jnp.tile(sin, (1, reps))


def _split_mod(modout, bsz):
    d = modout.shape[1] // 3
    m = modout.reshape(8, 3, d)
    lat = m[:bsz]
    ctx = jnp.broadcast_to(m[bsz:bsz + 1], (bsz, 3, d))
    return lat, ctx


def kernel(x, c, ctx, c_ctx, norm_g_0, ada_w_0, ada_b_0, a_w_in_0, a_sinks_0, a_w_o_0, norm_g_1,
           ada_w_1, ada_b_1, b_w_in_1, b_q_norm_1, b_w_uq_1, b_kv_norm_1, b_w_ukv_1, b_w_o_1, final_g):
    bsz, s, d = x.shape
    n_ctx = ctx.shape[1]
    assert bsz + 1 <= 8
    cc = jnp.concatenate([c, c_ctx[None], jnp.zeros((8 - bsz - 1, d), F32)], axis=0)
    row = lambda g: g.reshape(1, -1)

    mod_x, mod_c = _split_mod(_modulation(cc, ada_w_0, ada_b_0), bsz)
    w = a_w_in_0.astype(BF16)
    wq, wk = w[:, :A_WIDTH], w[:, A_WIDTH:A_WIDTH + A_KV_WIDTH]
    wv, wg = w[:, A_WIDTH + A_KV_WIDTH:A_WIDTH + 2 * A_KV_WIDTH], w[:, A_WIDTH + 2 * A_KV_WIDTH:]
    tabs_a = _rope_tables(s, A_HEAD_DIM)
    wo_a = a_w_o_0.astype(BF16)
    q, kd, vt, gact = _proj_a(x, mod_x, row(norm_g_0), wq, wk, wv, wg, tabs_a, tm=1024)
    qc, kdc, vtc, gc = _proj_a(ctx, mod_c, row(norm_g_0), wq, wk, wv, wg, (), tm=n_ctx)
    x1 = _attn_a(a_sinks_0, q, kd, vt, kdc, vtc, gact, x, mod_x, wo_a, tq=256)
    ctx1 = _attn_a_ctx(a_sinks_0, qc, kdc, vtc, gc, ctx, mod_c, wo_a)

    mod_x, mod_c = _split_mod(_modulation(cc, ada_w_1, ada_b_1), bsz)
    wb = b_w_in_1.astype(BF16)
    n_kv = B_Q_RANK + B_KV_RANK
    w1 = jnp.concatenate([wb[:, :n_kv]] + [wb[:, n_kv:n_kv + B_ROPE]] * (LANES // B_ROPE), axis=1)
    wgb = wb[:, n_kv + B_ROPE:]
    wuq = b_w_uq_1.astype(BF16).reshape(B_Q_RANK, B_HEADS, B_NOPE + B_ROPE)
    wqn_t = wuq[:, :, :B_NOPE].reshape(B_Q_RANK, B_HEADS * B_NOPE).T
    wqr_t = wuq[:, :, B_NOPE:].reshape(B_Q_RANK, B_HEADS * B_ROPE).T
    wukv = b_w_ukv_1.astype(BF16).reshape(B_KV_RANK, B_HEADS, B_NOPE + B_V)
    wkn_t = jnp.transpose(wukv[:, :, :B_NOPE], (1, 2, 0)).reshape(B_HEADS // 2, 2, B_NOPE, B_KV_RANK)
    zk = jnp.zeros_like(wkn_t[:, 0])
    wabs_t = jnp.transpose(
        jnp.concatenate([jnp.concatenate([wkn_t[:, 0], zk], axis=2),
                         jnp.concatenate([zk, wkn_t[:, 1]], axis=2)], axis=1), (0, 2, 1))
    wv_t = jnp.transpose(wukv[:, :, B_NOPE:], (1, 2, 0)).reshape(B_WIDTH, B_KV_RANK)
    tabs_b = _rope_tables(s, B_ROPE)
    cos_b, sin_b = _rope_cos_sin(s, B_ROPE)
    qp, kx, vtx, gb = _proj_b(x1, mod_x, row(norm_g_1), w1, wgb, row(b_q_norm_1), wqn_t, wqr_t,
                              wabs_t, row(b_kv_norm_1), wv_t, tabs_b, (cos_b.T, sin_b.T), tm=512)
    kc, vtc = _proj_b_ctx(ctx1, mod_c, row(norm_g_1), w1, row(b_kv_norm_1), wv_t)
    kall = jnp.concatenate([kc, kx], axis=1)
    vtall = jnp.concatenate([vtc, vtx], axis=2)
    return _flash_b(qp, kall, vtall, gb, x1, mod_x, b_w_o_1.astype(BF16), row(final_g), max_tk=1408)
```

```python
import functools

import jax
import jax.numpy as jnp
from jax import lax
from jax.experimental import pallas as pl
from jax.experimental.pallas import tpu as pltpu

F32 = jnp.float32
BF16 = jnp.bfloat16

GRID_W = 64
ROPE_BASE = 10000.0
EPS = 1e-6
NEG_INF = -1e30
LOG2E = 1.4426950408889634
LANES = 128
WINDOW = 128

A_HEADS, A_KV_HEADS, A_HEAD_DIM = 16, 4, 64
A_WIDTH = A_HEADS * A_HEAD_DIM
A_KV_WIDTH = A_KV_HEADS * A_HEAD_DIM
B_HEADS, B_NOPE, B_ROPE, B_V = 16, 64, 32, 64
B_Q_RANK, B_KV_RANK = 256, 128
B_WIDTH = B_HEADS * B_V
B_FEAT = 2 * LANES

VMEM_LIMIT = 56 * 1024 * 1024
MAX_UNROLL = 24

PROJ_A_TM = 1024
ATTN_A_TQ = 256
PROJ_B_TM = 512
FLASH_MAX_TK = 1408
FLASH_HEADS = 4
FLASH_TILES = 4


def _silu(x):
    return x * (1.0 / (1.0 + jnp.exp(-x)))


def _rms(x, g):
    return x * lax.rsqrt(jnp.mean(x * x, axis=-1, keepdims=True) + EPS) * g


def _rope(x, cos, sin, quarter):
    if cos is None:
        return x
    first = (lax.broadcasted_iota(jnp.int32, x.shape, 1) & (2 * quarter - 1)) < quarter
    rot = jnp.where(first, -pltpu.roll(x, LANES - quarter, 1), pltpu.roll(x, quarter, 1))
    return x * cos + rot * sin


def _dot(a, b):
    return jnp.dot(a, b, preferred_element_type=F32)


def _dot_nt(a, b):
    return lax.dot_general(a, b, (((1,), (1,)), ((), ())), preferred_element_type=F32)


def _mod_kernel(c_ref, w_ref, b_ref, o_ref):
    a = _silu(c_ref[...])
    o_ref[...] = jnp.dot(a, w_ref[...], precision=lax.Precision.HIGHEST,
                         preferred_element_type=F32) + b_ref[...]


def _modulation(cc, ada_w, ada_b):
    d = ada_w.shape[0]
    n = ada_w.shape[1]
    return pl.pallas_call(
        _mod_kernel,
        grid=(n // d,),
        in_specs=[pl.BlockSpec((8, d), lambda j: (0, 0)),
                  pl.BlockSpec((d, d), lambda j: (0, j)),
                  pl.BlockSpec((1, d), lambda j: (0, j))],
        out_specs=pl.BlockSpec((8, d), lambda j: (0, j)),
        out_shape=jax.ShapeDtypeStruct((8, n), F32),
        name="adaln_mod",
    )(cc, ada_w, ada_b.reshape(1, n))


def _proj_a_kernel(x_ref, mod_ref, ng_ref, wq_ref, wk_ref, wv_ref, wg_ref, *rest, qscale, rope):
    cos, sin = (rest[0][...], rest[1][...]) if rope else (None, None)
    q_ref, kd_ref, vt_ref, g_ref = rest[-4:]
    mod = mod_ref[0]
    h = (_rms(x_ref[0], ng_ref[...]) * (1.0 + mod[1:2]) + mod[0:1]).astype(BF16)
    quarter = A_HEAD_DIM // 4
    q = _dot(h, wq_ref[...])
    for j in range(A_WIDTH // LANES):
        sl = slice(j * LANES, (j + 1) * LANES)
        q_ref[0, :, sl] = (_rope(q[:, sl], cos, sin, quarter) * qscale).astype(BF16)
    lo = lax.broadcasted_iota(jnp.int32, (q.shape[0], LANES), 1) < A_HEAD_DIM
    k = _dot(h, wk_ref[...])
    for m in range(A_KV_WIDTH // LANES):
        sl = slice(m * LANES, (m + 1) * LANES)
        kp = _rope(k[:, sl], cos, sin, quarter)
        ks = pltpu.roll(kp, A_HEAD_DIM, 1)
        kd_ref[0, :, 2 * m * LANES:(2 * m + 1) * LANES] = jnp.where(lo, kp, ks).astype(BF16)
        kd_ref[0, :, (2 * m + 1) * LANES:(2 * m + 2) * LANES] = jnp.where(lo, ks, kp).astype(BF16)
    vt_ref[0] = _dot(h, wv_ref[...]).astype(BF16).T
    g_ref[0] = _silu(_dot(h, wg_ref[...])).astype(BF16)


def _proj_a(x, mod, norm_g, wq, wk, wv, wg, tabs, tm):
    bsz, s, d = x.shape
    kdw = A_KV_HEADS * LANES
    full = lambda a: pl.BlockSpec(a.shape, lambda b, t: (0,) * a.ndim)
    tab = pl.BlockSpec((tm, LANES), lambda b, t: (t, 0))
    tok = lambda w: pl.BlockSpec((1, tm, w), lambda b, t: (b, t, 0))
    return pl.pallas_call(
        functools.partial(_proj_a_kernel, qscale=A_HEAD_DIM ** -0.5 * LOG2E, rope=bool(tabs)),
        grid=(bsz, s // tm),
        in_specs=[tok(d), pl.BlockSpec((1, 3, d), lambda b, t: (b, 0, 0)), full(norm_g),
                  full(wq), full(wk), full(wv), full(wg)] + [tab] * len(tabs),
        out_specs=[tok(A_WIDTH), tok(kdw),
                   pl.BlockSpec((1, A_KV_WIDTH, tm), lambda b, t: (b, 0, t)), tok(A_WIDTH)],
        out_shape=[jax.ShapeDtypeStruct((bsz, s, A_WIDTH), BF16),
                   jax.ShapeDtypeStruct((bsz, s, kdw), BF16),
                   jax.ShapeDtypeStruct((bsz, A_KV_WIDTH, s), BF16),
                   jax.ShapeDtypeStruct((bsz, s, A_WIDTH), BF16)],
        compiler_params=pltpu.CompilerParams(
            dimension_semantics=("parallel", "parallel"), vmem_limit_bytes=VMEM_LIMIT),
        name="proj_a",
    )(x, mod, norm_g, wq, wk, wv, wg, *tabs)


def _attn_a_core(sink_ref, q, kd, vt, cap_loc, tq):
    group = A_HEADS // A_KV_HEADS
    nk = kd.shape[0]
    lane = lax.broadcasted_iota(jnp.int32, (1, 2 * tq), 1)
    klane = lax.broadcasted_iota(jnp.int32, (nk, LANES), 1)
    steps = [(g, par) for g in range(A_KV_HEADS) for par in range(2)]

    def scores(g, par):
        qrows = jnp.concatenate([q[:, (2 * g) * LANES:(2 * g + 1) * LANES],
                                 q[:, (2 * g + 1) * LANES:(2 * g + 2) * LANES]], axis=0)
        kg = kd[:, g * LANES:(g + 1) * LANES]
        keep = (klane < A_HEAD_DIM) if par == 0 else (klane >= A_HEAD_DIM)
        return _dot_nt(jnp.where(keep, kg, jnp.zeros_like(kg)), qrows)

    def finish(g, par, st):
        h0 = group * g + par
        sink = jnp.where(lane < tq, sink_ref[h0] * LOG2E, sink_ref[h0 + 2] * LOG2E)
        if cap_loc is not None:
            nl = cap_loc.shape[0]
            st = jnp.concatenate([jnp.minimum(st[:nl], cap_loc), st[nl:]], axis=0)
        m = jnp.maximum(sink, jnp.max(st, axis=0, keepdims=True))
        e = jnp.exp2(st - m)
        l = jnp.exp2(sink - m) + jnp.sum(e, axis=0, keepdims=True)
        ot = _dot(vt[g * A_HEAD_DIM:(g + 1) * A_HEAD_DIM], e.astype(BF16)) * (1.0 / l)
        return {h0: ot[:, :tq], h0 + 2: ot[:, tq:]}

    heads = {}
    cur = scores(*steps[0])
    for i, (g, par) in enumerate(steps):
        nxt = scores(*steps[i + 1]) if i + 1 < len(steps) else None
        heads.update(finish(g, par, cur))
        cur = nxt
    return jnp.concatenate([heads[h] for h in range(A_HEADS)], axis=0).T


def _attn_a_finish(o, g_ref, x_ref, mod_ref, wo_ref, out_ref):
    z = (o * g_ref[0].astype(F32)).astype(BF16)
    out_ref[0] = x_ref[0] + mod_ref[0][2:3] * _dot(z, wo_ref[...])


def _attn_a_kernel(sink_ref, q_ref, kp_ref, kc_ref, kn_ref, kx_ref, vp_ref, vc_ref, vn_ref, vx_ref,
                   g_ref, x_ref, mod_ref, wo_ref, out_ref, *, tq, seq):
    i = pl.program_id(1)
    kd = jnp.concatenate([kp_ref[0], kc_ref[0], kn_ref[0], kx_ref[0]], axis=0)
    vt = jnp.concatenate([vp_ref[0], vc_ref[0], vn_ref[0], vx_ref[0]], axis=1)
    nl = tq + 2 * WINDOW
    krow = lax.broadcasted_iota(jnp.int32, (nl, 2 * tq), 0)
    qcol = lax.broadcasted_iota(jnp.int32, (nl, 2 * tq), 1) & (tq - 1)
    kpos = i * tq - WINDOW + krow
    ok = (jnp.abs(qcol + WINDOW - krow) <= WINDOW) & (kpos >= 0) & (kpos < seq)
    o = _attn_a_core(sink_ref, q_ref[0], kd, vt, jnp.where(ok, jnp.inf, NEG_INF), tq)
    _attn_a_finish(o, g_ref, x_ref, mod_ref, wo_ref, out_ref)


def _attn_a_ctx_kernel(sink_ref, q_ref, kx_ref, vx_ref, g_ref, x_ref, mod_ref, wo_ref, out_ref, *, tq):
    o = _attn_a_core(sink_ref, q_ref[0], kx_ref[0], vx_ref[0], None, tq)
    _attn_a_finish(o, g_ref, x_ref, mod_ref, wo_ref, out_ref)


def _attn_a(sinks, q, kd, vt, kdc, vtc, gact, x, mod, wo, tq):
    bsz, s, d = x.shape
    c = kdc.shape[1]
    kdw = kd.shape[2]
    vw = vt.shape[1]
    assert tq & (tq - 1) == 0 and tq % WINDOW == 0
    r = tq // WINDOW
    nblk = s // WINDOW
    tok = lambda w: pl.BlockSpec((1, tq, w), lambda b, i: (b, i, 0))
    prev_i = lambda i: jnp.maximum(i * r - 1, 0)
    next_i = lambda i: jnp.minimum((i + 1) * r, nblk - 1)
    return pl.pallas_call(
        functools.partial(_attn_a_kernel, tq=tq, seq=s),
        grid=(bsz, s // tq),
        in_specs=[pl.BlockSpec(memory_space=pltpu.SMEM), tok(A_WIDTH),
                  pl.BlockSpec((1, WINDOW, kdw), lambda b, i: (b, prev_i(i), 0)), tok(kdw),
                  pl.BlockSpec((1, WINDOW, kdw), lambda b, i: (b, next_i(i), 0)),
                  pl.BlockSpec((1, c, kdw), lambda b, i: (b, 0, 0)),
                  pl.BlockSpec((1, vw, WINDOW), lambda b, i: (b, 0, prev_i(i))),
                  pl.BlockSpec((1, vw, tq), lambda b, i: (b, 0, i)),
                  pl.BlockSpec((1, vw, WINDOW), lambda b, i: (b, 0, next_i(i))),
                  pl.BlockSpec((1, vw, c), lambda b, i: (b, 0, 0)),
                  tok(A_WIDTH), tok(d), pl.BlockSpec((1, 3, d), lambda b, i: (b, 0, 0)),
                  pl.BlockSpec(wo.shape, lambda b, i: (0, 0))],
        out_specs=tok(d),
        out_shape=jax.ShapeDtypeStruct((bsz, s, d), F32),
        compiler_params=pltpu.CompilerParams(
            dimension_semantics=("parallel", "parallel"), vmem_limit_bytes=VMEM_LIMIT),
        name="attn_a",
    )(sinks, q, kd, kd, kd, kdc, vt, vt, vt, vtc, gact, x, mod, wo)


def _attn_a_ctx(sinks, qc, kdc, vtc, gc, ctx, mod, wo):
    bsz, c, d = ctx.shape
    tok = lambda w: pl.BlockSpec((1, c, w), lambda b: (b, 0, 0))
    return pl.pallas_call(
        functools.partial(_attn_a_ctx_kernel, tq=c),
        grid=(bsz,),
        in_specs=[pl.BlockSpec(memory_space=pltpu.SMEM), tok(A_WIDTH), tok(kdc.shape[2]),
                  pl.BlockSpec((1, vtc.shape[1], c), lambda b: (b, 0, 0)),
                  tok(A_WIDTH), tok(d), pl.BlockSpec((1, 3, d), lambda b: (b, 0, 0)),
                  pl.BlockSpec(wo.shape, lambda b: (0, 0))],
        out_specs=tok(d),
        out_shape=jax.ShapeDtypeStruct((bsz, c, d), F32),
        compiler_params=pltpu.CompilerParams(
            dimension_semantics=("parallel",), vmem_limit_bytes=VMEM_LIMIT),
        name="attn_a_ctx",
    )(sinks, qc, kdc, vtc, gc, ctx, mod, wo)


def _kext(a, kvg, cos, sin):
    ckv = _rms(a[:, B_Q_RANK:B_Q_RANK + B_KV_RANK], kvg)
    kr = a[:, B_Q_RANK + B_KV_RANK:]
    kr = _rope(kr, cos, sin, B_ROPE // 4)
    return jnp.concatenate([ckv, kr], axis=1).astype(BF16), ckv.T.astype(BF16)


def _proj_b_kernel(x_ref, mod_ref, ng_ref, w1_ref, wg_ref, qg_ref, wqn_ref, wqr_ref, wabs_ref,
                   kvg_ref, c_ref, s_ref, ct_ref, st_ref, qt_ref, k_ref, vt_ref, g_ref,
                   *, qscale, tm):
    mod = mod_ref[0]
    h = (_rms(x_ref[0], ng_ref[...]) * (1.0 + mod[1:2]) + mod[0:1]).astype(BF16)
    a = _dot(h, w1_ref[...])
    k_ref[0], vt_ref[0] = _kext(a, kvg_ref[...], c_ref[...], s_ref[...])
    g_ref[0] = _silu(_dot(h, wg_ref[...])).astype(BF16)
    cq_t = _rms(a[:, :B_Q_RANK], qg_ref[...]).T.astype(BF16)
    qn_t = _dot(wqn_ref[...], cq_t).astype(BF16)
    qr_t = _dot(wqr_ref[...], cq_t)
    ct, st = ct_ref[...], st_ref[...]
    quarter = B_ROPE // 4
    heads_per_chunk = LANES // B_ROPE
    for hd in range(B_HEADS):
        xh = qr_t[hd * B_ROPE:(hd + 1) * B_ROPE]
        rot = jnp.concatenate([-xh[quarter:2 * quarter], xh[:quarter],
                               -xh[3 * quarter:], xh[2 * quarter:3 * quarter]], axis=0)
        rope = ((xh * ct + rot * st) * qscale).astype(BF16)
        j = hd % heads_per_chunk
        for t in range(tm // LANES):
            blk = rope[:, t * LANES:(t + 1) * LANES]
            pieces = [jnp.zeros((j * B_ROPE, LANES), BF16)] if j else []
            pieces.append(blk)
            if j + 1 < heads_per_chunk:
                pieces.append(jnp.zeros(((heads_per_chunk - 1 - j) * B_ROPE, LANES), BF16))
            qt_ref[0, t, LANES:, hd * LANES:(hd + 1) * LANES] = jnp.concatenate(pieces, axis=0)
    for m in range(B_HEADS // 2):
        qa_t = (_dot(wabs_ref[m], qn_t[m * LANES:(m + 1) * LANES]) * qscale).astype(BF16)
        for half in range(2):
            hd = 2 * m + half
            for t in range(tm // LANES):
                qt_ref[0, t, :LANES, hd * LANES:(hd + 1) * LANES] = (
                    qa_t[half * LANES:(half + 1) * LANES, t * LANES:(t + 1) * LANES])


def _proj_b(x, mod, norm_g, w1, wg, qg, wqn_t, wqr_t, wabs_t, kvg, tabs, tabs_t, tm):
    bsz, s, d = x.shape
    full = lambda a: pl.BlockSpec(a.shape, lambda b, t: (0,) * a.ndim)
    tab = pl.BlockSpec((tm, LANES), lambda b, t: (t, 0))
    tab_t = pl.BlockSpec((B_ROPE, tm), lambda b, t: (0, t))
    tok = lambda w: pl.BlockSpec((1, tm, w), lambda b, t: (b, t, 0))
    nt = tm // LANES
    return pl.pallas_call(
        functools.partial(_proj_b_kernel, qscale=(B_NOPE + B_ROPE) ** -0.5 * LOG2E, tm=tm),
        grid=(bsz, s // tm),
        in_specs=[tok(d), pl.BlockSpec((1, 3, d), lambda b, t: (b, 0, 0)), full(norm_g),
                  full(w1), full(wg), full(qg), full(wqn_t), full(wqr_t), full(wabs_t), full(kvg),
                  tab, tab, tab_t, tab_t],
        out_specs=[pl.BlockSpec((1, nt, B_FEAT, B_HEADS * LANES), lambda b, t: (b, t, 0, 0)),
                   tok(B_FEAT), pl.BlockSpec((1, B_KV_RANK, tm), lambda b, t: (b, 0, t)),
                   tok(B_WIDTH)],
        out_shape=[jax.ShapeDtypeStruct((bsz, s // LANES, B_FEAT, B_HEADS * LANES), BF16),
                   jax.ShapeDtypeStruct((bsz, s, B_FEAT), BF16),
                   jax.ShapeDtypeStruct((bsz, B_KV_RANK, s), BF16),
                   jax.ShapeDtypeStruct((bsz, s, B_WIDTH), BF16)],
        compiler_params=pltpu.CompilerParams(
            dimension_semantics=("parallel", "parallel"), vmem_limit_bytes=VMEM_LIMIT),
        name="proj_b",
    )(x, mod, norm_g, w1, wg, qg, wqn_t, wqr_t, wabs_t, kvg, *tabs, *tabs_t)


def _proj_b_ctx_kernel(x_ref, mod_ref, ng_ref, w1_ref, kvg_ref, k_ref, vt_ref):
    mod = mod_ref[0]
    h = (_rms(x_ref[0], ng_ref[...]) * (1.0 + mod[1:2]) + mod[0:1]).astype(BF16)
    k_ref[0], vt_ref[0] = _kext(_dot(h, w1_ref[...]), kvg_ref[...], None, None)


def _proj_b_ctx(ctx, mod, norm_g, w1, kvg):
    bsz, c, d = ctx.shape
    full = lambda a: pl.BlockSpec(a.shape, lambda b: (0,) * a.ndim)
    return pl.pallas_call(
        _proj_b_ctx_kernel,
        grid=(bsz,),
        in_specs=[pl.BlockSpec((1, c, d), lambda b: (b, 0, 0)),
                  pl.BlockSpec((1, 3, d), lambda b: (b, 0, 0)), full(norm_g), full(w1), full(kvg)],
        out_specs=[pl.BlockSpec((1, c, B_FEAT), lambda b: (b, 0, 0)),
                   pl.BlockSpec((1, B_KV_RANK, c), lambda b: (b, 0, 0))],
        out_shape=[jax.ShapeDtypeStruct((bsz, c, B_FEAT), BF16),
                   jax.ShapeDtypeStruct((bsz, B_KV_RANK, c), BF16)],
        compiler_params=pltpu.CompilerParams(dimension_semantics=("parallel",)),
        name="proj_b_ctx",
    )(ctx, mod, norm_g, w1, kvg)


def _flash_b_kernel(q_ref, k_ref, vt_ref, g_ref, x_ref, mod_ref, wvt_ref, wo_ref, fg_ref, out_ref,
                    m_ref, l_ref, acc_ref, s_ref, mx_ref, *, tiles, tk, nk, hb):
    tq = LANES
    rt = hb * tq
    m_ref[...] = jnp.full(m_ref.shape, NEG_INF, F32)
    l_ref[...] = jnp.zeros(l_ref.shape, F32)
    acc_ref[...] = jnp.zeros(acc_ref.shape, F32)

    hgroups = B_HEADS // hb
    nsb = tiles * hgroups
    nsteps = (nk // tk) * nsb
    unroll = max(u for u in range(2, MAX_UNROLL + 1, 2) if nsteps % u == 0)
    assert nsb & (nsb - 1) == 0 and hgroups & (hgroups - 1) == 0
    shift = nsb.bit_length() - 1
    hshift = hgroups.bit_length() - 1

    def split(k):
        return lax.shift_right_logical(k, shift), lax.bitwise_and(k, nsb - 1)

    def score_stage(k, slot):
        j, t = split(k)
        kc = k_ref[0, pl.ds(pl.multiple_of(j * tk, tk), tk), :]
        tile, hg = lax.shift_right_logical(t, hshift), lax.bitwise_and(t, hgroups - 1)
        st = _dot(kc, q_ref[0, tile, :, pl.ds(pl.multiple_of(hg * rt, rt), rt)])
        s_ref[slot] = st
        mx_ref[slot] = jnp.max(st.reshape(tk // 8, 8, rt), axis=0)

    def update_stage(k, slot):
        j, t = split(k)
        vt = vt_ref[0, :, pl.ds(pl.multiple_of(j * tk, tk), tk)]
        m_old = m_ref[t]
        m_new = jnp.maximum(m_old, jnp.max(mx_ref[slot], axis=0, keepdims=True))
        alpha = jnp.exp2(m_old - m_new)
        e = jnp.exp2(s_ref[slot] - m_new[0:1])
        l_ref[t] = alpha * l_ref[t] + jnp.sum(e.reshape(tk // 8, 8, rt), axis=0)
        acc_ref[t] = alpha[0:1] * acc_ref[t] + _dot(vt, e.astype(BF16))
        m_ref[t] = m_new

    score_stage(0, 0)

    def body(i, carry):
        for u in range(unroll):
            k = unroll * i + u
            score_stage(jnp.minimum(k + 1, nsteps - 1), (u + 1) % 2)
            update_stage(k, u % 2)
        return carry

    lax.fori_loop(0, nsteps // unroll, body, 0)
    rows = []
    for tile in range(tiles):
        outs = []
        for hg in range(hgroups):
            t = tile * hgroups + hg
            linv = 1.0 / jnp.sum(l_ref[t], axis=0, keepdims=True)
            ot = (acc_ref[t] * linv).astype(BF16)
            for p in range(hb // 2):
                pair = jnp.concatenate([ot[:, (2 * p) * tq:(2 * p + 1) * tq],
                                        ot[:, (2 * p + 1) * tq:(2 * p + 2) * tq]], axis=0)
                outs.append(_dot(wvt_ref[hg * (hb // 2) + p], pair))
        rows.append(jnp.concatenate(outs, axis=0).T)
    o = jnp.concatenate(rows, axis=0)
    z = (o * g_ref[0].astype(F32)).astype(BF16)
    xo = x_ref[0] + mod_ref[0][2:3] * _dot(z, wo_ref[...])
    out_ref[0] = _rms(xo, fg_ref[...])


def _flash_b(qp, kall, vtall, gact, x, mod, wvt, wo, final_g, max_tk, hb, tiles):
    bsz, s, d = x.shape
    nk = kall.shape[1]
    tq = tiles * LANES
    tk = max(t for t in range(LANES, max_tk + 1, LANES) if nk % t == 0)
    assert hb % 2 == 0 and B_HEADS % hb == 0
    nt = tiles * (B_HEADS // hb)
    tok = lambda w: pl.BlockSpec((1, tq, w), lambda b, i: (b, i, 0))
    full = lambda a: pl.BlockSpec(a.shape, lambda b, i: (0,) * a.ndim)
    return pl.pallas_call(
        functools.partial(_flash_b_kernel, tiles=tiles, tk=tk, nk=nk, hb=hb),
        grid=(bsz, s // tq),
        in_specs=[pl.BlockSpec((1, tiles, B_FEAT, B_HEADS * LANES), lambda b, i: (b, i, 0, 0)),
                  pl.BlockSpec((1, nk, B_FEAT), lambda b, i: (b, 0, 0)),
                  pl.BlockSpec((1, B_KV_RANK, nk), lambda b, i: (b, 0, 0)),
                  tok(B_WIDTH), tok(d), pl.BlockSpec((1, 3, d), lambda b, i: (b, 0, 0)),
                  full(wvt), full(wo), full(final_g)],
        out_specs=tok(d),
        out_shape=jax.ShapeDtypeStruct((bsz, s, d), F32),
        scratch_shapes=[pltpu.VMEM((nt, 8, hb * LANES), F32), pltpu.VMEM((nt, 8, hb * LANES), F32),
                        pltpu.VMEM((nt, B_KV_RANK, hb * LANES), F32),
                        pltpu.VMEM((2, tk, hb * LANES), F32), pltpu.VMEM((2, 8, hb * LANES), F32)],
        compiler_params=pltpu.CompilerParams(
            dimension_semantics=("parallel", "parallel"), vmem_limit_bytes=VMEM_LIMIT),
        name="flash_b",
    )(qp, kall, vtall, gact, x, mod, wvt, wo, final_g)


def _rope_cos_sin(n_tokens, rot_dim):
    n_rows = n_tokens // GRID_W
    row = jnp.broadcast_to(jnp.arange(n_rows)[:, None], (n_rows, GRID_W)).reshape(-1)
    col = jnp.broadcast_to(jnp.arange(GRID_W)[None, :], (n_rows, GRID_W)).reshape(-1)
    nf = rot_dim // 4
    inv = ROPE_BASE ** (-jnp.arange(nf, dtype=F32) / nf)
    ar = row.astype(F32)[:, None] * inv
    ac = col.astype(F32)[:, None] * inv
    ang = jnp.concatenate([ar, ar, ac, ac], axis=-1)
    return jnp.cos(ang), jnp.sin(ang)


def _rope_tables(n_tokens, rot_dim):
    reps = LANES // rot_dim
    cos, sin = _rope_cos_sin(n_tokens, rot_dim)
    return jnp.tile(cos, (1, reps)), jnp.tile(sin, (1, reps))


def _split_mod(modout, bsz):
    d = modout.shape[1] // 3
    m = modout.reshape(8, 3, d)
    lat = m[:bsz]
    ctx = jnp.broadcast_to(m[bsz:bsz + 1], (bsz, 3, d))
    return lat, ctx


def kernel(x, c, ctx, c_ctx, norm_g_0, ada_w_0, ada_b_0, a_w_in_0, a_sinks_0, a_w_o_0, norm_g_1,
           ada_w_1, ada_b_1, b_w_in_1, b_q_norm_1, b_w_uq_1, b_kv_norm_1, b_w_ukv_1, b_w_o_1, final_g):
    bsz, s, d = x.shape
    n_ctx = ctx.shape[1]
    assert bsz + 1 <= 8
    cc = jnp.concatenate([c, c_ctx[None], jnp.zeros((8 - bsz - 1, d), F32)], axis=0)
    row = lambda g: g.reshape(1, -1)

    mod_x, mod_c = _split_mod(_modulation(cc, ada_w_0, ada_b_0), bsz)
    w = a_w_in_0.astype(BF16)
    wq, wk = w[:, :A_WIDTH], w[:, A_WIDTH:A_WIDTH + A_KV_WIDTH]
    wv, wg = w[:, A_WIDTH + A_KV_WIDTH:A_WIDTH + 2 * A_KV_WIDTH], w[:, A_WIDTH + 2 * A_KV_WIDTH:]
    tabs_a = _rope_tables(s, A_HEAD_DIM)
    wo_a = a_w_o_0.astype(BF16)
    q, kd, vt, gact = _proj_a(x, mod_x, row(norm_g_0), wq, wk, wv, wg, tabs_a, tm=PROJ_A_TM)
    qc, kdc, vtc, gc = _proj_a(ctx, mod_c, row(norm_g_0), wq, wk, wv, wg, (), tm=n_ctx)
    x1 = _attn_a(a_sinks_0, q, kd, vt, kdc, vtc, gact, x, mod_x, wo_a, tq=ATTN_A_TQ)
    ctx1 = _attn_a_ctx(a_sinks_0, qc, kdc, vtc, gc, ctx, mod_c, wo_a)

    mod_x, mod_c = _split_mod(_modulation(cc, ada_w_1, ada_b_1), bsz)
    wb = b_w_in_1.astype(BF16)
    n_kv = B_Q_RANK + B_KV_RANK
    w1 = jnp.concatenate([wb[:, :n_kv]] + [wb[:, n_kv:n_kv + B_ROPE]] * (LANES // B_ROPE), axis=1)
    wgb = wb[:, n_kv + B_ROPE:]
    wuq = b_w_uq_1.astype(BF16).reshape(B_Q_RANK, B_HEADS, B_NOPE + B_ROPE)
    wqn_t = wuq[:, :, :B_NOPE].reshape(B_Q_RANK, B_HEADS * B_NOPE).T
    wqr_t = wuq[:, :, B_NOPE:].reshape(B_Q_RANK, B_HEADS * B_ROPE).T
    wukv = b_w_ukv_1.astype(BF16).reshape(B_KV_RANK, B_HEADS, B_NOPE + B_V)
    wkn_t = jnp.transpose(wukv[:, :, :B_NOPE], (1, 2, 0)).reshape(B_HEADS // 2, 2, B_NOPE, B_KV_RANK)
    zk = jnp.zeros_like(wkn_t[:, 0])
    wabs_t = jnp.transpose(
        jnp.concatenate([jnp.concatenate([wkn_t[:, 0], zk], axis=2),
                         jnp.concatenate([zk, wkn_t[:, 1]], axis=2)], axis=1), (0, 2, 1))
    wvh = jnp.transpose(wukv[:, :, B_NOPE:], (1, 0, 2)).reshape(B_HEADS // 2, 2, B_KV_RANK, B_V)
    zv = jnp.zeros_like(wvh[:, 0])
    wvup_t = jnp.transpose(
        jnp.concatenate([jnp.concatenate([wvh[:, 0], zv], axis=2),
                         jnp.concatenate([zv, wvh[:, 1]], axis=2)], axis=1), (0, 2, 1))
    tabs_b = _rope_tables(s, B_ROPE)
    cos_b, sin_b = _rope_cos_sin(s, B_ROPE)
    qp, kx, vtx, gb = _proj_b(x1, mod_x, row(norm_g_1), w1, wgb, row(b_q_norm_1), wqn_t, wqr_t,
                              wabs_t, row(b_kv_norm_1), tabs_b, (cos_b.T, sin_b.T), tm=PROJ_B_TM)
    kc, vtc = _proj_b_ctx(ctx1, mod_c, row(norm_g_1), w1, row(b_kv_norm_1))
    kall = jnp.concatenate([kc, kx], axis=1)
    vtall = jnp.concatenate([vtc, vtx], axis=2)
    return _flash_b(qp, kall, vtall, gb, x1, mod_x, wvup_t, b_w_o_1.astype(BF16), row(final_g),
                    max_tk=FLASH_MAX_TK, hb=FLASH_HEADS, tiles=FLASH_TILES)
```

```python
import functools

import jax
import jax.numpy as jnp
from jax import lax
from jax.experimental import pallas as pl
from jax.experimental.pallas import tpu as pltpu

F32 = jnp.float32
BF16 = jnp.bfloat16

GRID_W = 64
ROPE_BASE = 10000.0
EPS = 1e-6
NEG_INF = -1e30
LOG2E = 1.4426950408889634
LANES = 128
WINDOW = 128

A_HEADS, A_KV_HEADS, A_HEAD_DIM = 16, 4, 64
A_WIDTH = A_HEADS * A_HEAD_DIM
A_KV_WIDTH = A_KV_HEADS * A_HEAD_DIM
B_HEADS, B_NOPE, B_ROPE, B_V = 16, 64, 32, 64
B_Q_RANK, B_KV_RANK = 256, 128
B_WIDTH = B_HEADS * B_V
B_FEAT = 2 * LANES

VMEM_LIMIT = 56 * 1024 * 1024
MAX_UNROLL = 12

PROJ_A_TM = 1024
ATTN_A_TQ = 256
PROJ_B_TM = 512
FLASH_MAX_TK = 2816
FLASH_HEADS = 4
FLASH_TILES = 4


def _silu(x):
    return x * (1.0 / (1.0 + jnp.exp(-x)))


def _rms(x, g):
    return x * lax.rsqrt(jnp.mean(x * x, axis=-1, keepdims=True) + EPS) * g


def _rope(x, cos, sin, quarter):
    if cos is None:
        return x
    first = (lax.broadcasted_iota(jnp.int32, x.shape, 1) & (2 * quarter - 1)) < quarter
    rot = jnp.where(first, -pltpu.roll(x, LANES - quarter, 1), pltpu.roll(x, quarter, 1))
    return x * cos + rot * sin


def _dot(a, b):
    return jnp.dot(a, b, preferred_element_type=F32)


def _dot_nt(a, b):
    return lax.dot_general(a, b, (((1,), (1,)), ((), ())), preferred_element_type=F32)


def _mod_kernel(c_ref, w_ref, b_ref, o_ref):
    a = _silu(c_ref[...])
    o_ref[...] = jnp.dot(a, w_ref[...], precision=lax.Precision.HIGHEST,
                         preferred_element_type=F32) + b_ref[...]


def _modulation(cc, ada_w, ada_b):
    d = ada_w.shape[0]
    n = ada_w.shape[1]
    return pl.pallas_call(
        _mod_kernel,
        grid=(n // d,),
        in_specs=[pl.BlockSpec((8, d), lambda j: (0, 0)),
                  pl.BlockSpec((d, d), lambda j: (0, j)),
                  pl.BlockSpec((1, d), lambda j: (0, j))],
        out_specs=pl.BlockSpec((8, d), lambda j: (0, j)),
        out_shape=jax.ShapeDtypeStruct((8, n), F32),
        name="adaln_mod",
    )(cc, ada_w, ada_b.reshape(1, n))


def _proj_a_kernel(x_ref, mod_ref, ng_ref, wq_ref, wk_ref, wv_ref, wg_ref, *rest, qscale, rope):
    cos, sin = (rest[0][...], rest[1][...]) if rope else (None, None)
    q_ref, kd_ref, vt_ref, g_ref = rest[-4:]
    mod = mod_ref[0]
    h = (_rms(x_ref[0], ng_ref[...]) * (1.0 + mod[1:2]) + mod[0:1]).astype(BF16)
    quarter = A_HEAD_DIM // 4
    q = _dot(h, wq_ref[...])
    for j in range(A_WIDTH // LANES):
        sl = slice(j * LANES, (j + 1) * LANES)
        q_ref[0, :, sl] = (_rope(q[:, sl], cos, sin, quarter) * qscale).astype(BF16)
    lo = lax.broadcasted_iota(jnp.int32, (q.shape[0], LANES), 1) < A_HEAD_DIM
    k = _dot(h, wk_ref[...])
    for m in range(A_KV_WIDTH // LANES):
        sl = slice(m * LANES, (m + 1) * LANES)
        kp = _rope(k[:, sl], cos, sin, quarter)
        ks = pltpu.roll(kp, A_HEAD_DIM, 1)
        kd_ref[0, :, 2 * m * LANES:(2 * m + 1) * LANES] = jnp.where(lo, kp, ks).astype(BF16)
        kd_ref[0, :, (2 * m + 1) * LANES:(2 * m + 2) * LANES] = jnp.where(lo, ks, kp).astype(BF16)
    vt_ref[0] = _dot(h, wv_ref[...]).astype(BF16).T
    g_ref[0] = _silu(_dot(h, wg_ref[...])).astype(BF16)


def _proj_a(x, mod, norm_g, wq, wk, wv, wg, tabs, tm):
    bsz, s, d = x.shape
    kdw = A_KV_HEADS * LANES
    full = lambda a: pl.BlockSpec(a.shape, lambda b, t: (0,) * a.ndim)
    tab = pl.BlockSpec((tm, LANES), lambda b, t: (t, 0))
    tok = lambda w: pl.BlockSpec((1, tm, w), lambda b, t: (b, t, 0))
    return pl.pallas_call(
        functools.partial(_proj_a_kernel, qscale=A_HEAD_DIM ** -0.5 * LOG2E, rope=bool(tabs)),
        grid=(bsz, s // tm),
        in_specs=[tok(d), pl.BlockSpec((1, 3, d), lambda b, t: (b, 0, 0)), full(norm_g),
                  full(wq), full(wk), full(wv), full(wg)] + [tab] * len(tabs),
        out_specs=[tok(A_WIDTH), tok(kdw),
                   pl.BlockSpec((1, A_KV_WIDTH, tm), lambda b, t: (b, 0, t)), tok(A_WIDTH)],
        out_shape=[jax.ShapeDtypeStruct((bsz, s, A_WIDTH), BF16),
                   jax.ShapeDtypeStruct((bsz, s, kdw), BF16),
                   jax.ShapeDtypeStruct((bsz, A_KV_WIDTH, s), BF16),
                   jax.ShapeDtypeStruct((bsz, s, A_WIDTH), BF16)],
        compiler_params=pltpu.CompilerParams(
            dimension_semantics=("parallel", "parallel"), vmem_limit_bytes=VMEM_LIMIT),
        name="proj_a",
    )(x, mod, norm_g, wq, wk, wv, wg, *tabs)


def _attn_a_core(sink_ref, q, kd, vt, cap_loc, tq):
    group = A_HEADS // A_KV_HEADS
    nk = kd.shape[0]
    lane = lax.broadcasted_iota(jnp.int32, (1, 2 * tq), 1)
    klane = lax.broadcasted_iota(jnp.int32, (nk, LANES), 1)
    steps = [(g, par) for g in range(A_KV_HEADS) for par in range(2)]

    def scores(g, par):
        qrows = jnp.concatenate([q[:, (2 * g) * LANES:(2 * g + 1) * LANES],
                                 q[:, (2 * g + 1) * LANES:(2 * g + 2) * LANES]], axis=0)
        kg = kd[:, g * LANES:(g + 1) * LANES]
        keep = (klane < A_HEAD_DIM) if par == 0 else (klane >= A_HEAD_DIM)
        return _dot_nt(jnp.where(keep, kg, jnp.zeros_like(kg)), qrows)

    def finish(g, par, st):
        h0 = group * g + par
        sink = jnp.where(lane < tq, sink_ref[h0] * LOG2E, sink_ref[h0 + 2] * LOG2E)
        if cap_loc is not None:
            nl = cap_loc.shape[0]
            st = jnp.concatenate([jnp.minimum(st[:nl], cap_loc), st[nl:]], axis=0)
        m = jnp.maximum(sink, jnp.max(st, axis=0, keepdims=True))
        e = jnp.exp2(st - m)
        l = jnp.exp2(sink - m) + jnp.sum(e, axis=0, keepdims=True)
        ot = _dot(vt[g * A_HEAD_DIM:(g + 1) * A_HEAD_DIM], e.astype(BF16)) * (1.0 / l)
        return {h0: ot[:, :tq], h0 + 2: ot[:, tq:]}

    heads = {}
    cur = scores(*steps[0])
    for i, (g, par) in enumerate(steps):
        nxt = scores(*steps[i + 1]) if i + 1 < len(steps) else None
        heads.update(finish(g, par, cur))
        cur = nxt
    return jnp.concatenate([heads[h] for h in range(A_HEADS)], axis=0).T


def _attn_a_finish(o, g_ref, x_ref, mod_ref, wo_ref, out_ref):
    z = (o * g_ref[0].astype(F32)).astype(BF16)
    out_ref[0] = x_ref[0] + mod_ref[0][2:3] * _dot(z, wo_ref[...])


def _attn_a_kernel(sink_ref, q_ref, kp_ref, kc_ref, kn_ref, kx_ref, vp_ref, vc_ref, vn_ref, vx_ref,
                   g_ref, x_ref, mod_ref, wo_ref, out_ref, *, tq, seq):
    i = pl.program_id(1)
    kd = jnp.concatenate([kp_ref[0], kc_ref[0], kn_ref[0], kx_ref[0]], axis=0)
    vt = jnp.concatenate([vp_ref[0], vc_ref[0], vn_ref[0], vx_ref[0]], axis=1)
    nl = tq + 2 * WINDOW
    krow = lax.broadcasted_iota(jnp.int32, (nl, 2 * tq), 0)
    qcol = lax.broadcasted_iota(jnp.int32, (nl, 2 * tq), 1) & (tq - 1)
    kpos = i * tq - WINDOW + krow
    ok = (jnp.abs(qcol + WINDOW - krow) <= WINDOW) & (kpos >= 0) & (kpos < seq)
    o = _attn_a_core(sink_ref, q_ref[0], kd, vt, jnp.where(ok, jnp.inf, NEG_INF), tq)
    _attn_a_finish(o, g_ref, x_ref, mod_ref, wo_ref, out_ref)


def _attn_a_ctx_kernel(sink_ref, q_ref, kx_ref, vx_ref, g_ref, x_ref, mod_ref, wo_ref, out_ref, *, tq):
    o = _attn_a_core(sink_ref, q_ref[0], kx_ref[0], vx_ref[0], None, tq)
    _attn_a_finish(o, g_ref, x_ref, mod_ref, wo_ref, out_ref)


def _attn_a(sinks, q, kd, vt, kdc, vtc, gact, x, mod, wo, tq):
    bsz, s, d = x.shape
    c = kdc.shape[1]
    kdw = kd.shape[2]
    vw = vt.shape[1]
    assert tq & (tq - 1) == 0 and tq % WINDOW == 0
    r = tq // WINDOW
    nblk = s // WINDOW
    tok = lambda w: pl.BlockSpec((1, tq, w), lambda b, i: (b, i, 0))
    prev_i = lambda i: jnp.maximum(i * r - 1, 0)
    next_i = lambda i: jnp.minimum((i + 1) * r, nblk - 1)
    return pl.pallas_call(
        functools.partial(_attn_a_kernel, tq=tq, seq=s),
        grid=(bsz, s // tq),
        in_specs=[pl.BlockSpec(memory_space=pltpu.SMEM), tok(A_WIDTH),
                  pl.BlockSpec((1, WINDOW, kdw), lambda b, i: (b, prev_i(i), 0)), tok(kdw),
                  pl.BlockSpec((1, WINDOW, kdw), lambda b, i: (b, next_i(i), 0)),
                  pl.BlockSpec((1, c, kdw), lambda b, i: (b, 0, 0)),
                  pl.BlockSpec((1, vw, WINDOW), lambda b, i: (b, 0, prev_i(i))),
                  pl.BlockSpec((1, vw, tq), lambda b, i: (b, 0, i)),
                  pl.BlockSpec((1, vw, WINDOW), lambda b, i: (b, 0, next_i(i))),
                  pl.BlockSpec((1, vw, c), lambda b, i: (b, 0, 0)),
                  tok(A_WIDTH), tok(d), pl.BlockSpec((1, 3, d), lambda b, i: (b, 0, 0)),
                  pl.BlockSpec(wo.shape, lambda b, i: (0, 0))],
        out_specs=tok(d),
        out_shape=jax.ShapeDtypeStruct((bsz, s, d), F32),
        compiler_params=pltpu.CompilerParams(
            dimension_semantics=("parallel", "parallel"), vmem_limit_bytes=VMEM_LIMIT),
        name="attn_a",
    )(sinks, q, kd, kd, kd, kdc, vt, vt, vt, vtc, gact, x, mod, wo)


def _attn_a_ctx(sinks, qc, kdc, vtc, gc, ctx, mod, wo):
    bsz, c, d = ctx.shape
    tok = lambda w: pl.BlockSpec((1, c, w), lambda b: (b, 0, 0))
    return pl.pallas_call(
        functools.partial(_attn_a_ctx_kernel, tq=c),
        grid=(bsz,),
        in_specs=[pl.BlockSpec(memory_space=pltpu.SMEM), tok(A_WIDTH), tok(kdc.shape[2]),
                  pl.BlockSpec((1, vtc.shape[1], c), lambda b: (b, 0, 0)),
                  tok(A_WIDTH), tok(d), pl.BlockSpec((1, 3, d), lambda b: (b, 0, 0)),
                  pl.BlockSpec(wo.shape, lambda b: (0, 0))],
        out_specs=tok(d),
        out_shape=jax.ShapeDtypeStruct((bsz, c, d), F32),
        compiler_params=pltpu.CompilerParams(
            dimension_semantics=("parallel",), vmem_limit_bytes=VMEM_LIMIT),
        name="attn_a_ctx",
    )(sinks, qc, kdc, vtc, gc, ctx, mod, wo)


def _kext(a, kvg, cos, sin):
    ckv = _rms(a[:, B_Q_RANK:B_Q_RANK + B_KV_RANK], kvg)
    kr = a[:, B_Q_RANK + B_KV_RANK:]
    kr = _rope(kr, cos, sin, B_ROPE // 4)
    return jnp.concatenate([ckv, kr], axis=1).astype(BF16), ckv.T.astype(BF16)


def _proj_b_kernel(x_ref, mod_ref, ng_ref, w1_ref, wg_ref, qg_ref, wqn_ref, wqr_ref, wabs_ref,
                   kvg_ref, c_ref, s_ref, ct_ref, st_ref, qt_ref, k_ref, vt_ref, g_ref,
                   *, qscale, tm):
    mod = mod_ref[0]
    h = (_rms(x_ref[0], ng_ref[...]) * (1.0 + mod[1:2]) + mod[0:1]).astype(BF16)
    a = _dot(h, w1_ref[...])
    k_ref[0], vt_ref[0] = _kext(a, kvg_ref[...], c_ref[...], s_ref[...])
    g_ref[0] = _silu(_dot(h, wg_ref[...])).astype(BF16)
    cq_t = _rms(a[:, :B_Q_RANK], qg_ref[...]).T.astype(BF16)
    qn_t = _dot(wqn_ref[...], cq_t).astype(BF16)
    qr_t = _dot(wqr_ref[...], cq_t)
    ct, st = ct_ref[...], st_ref[...]
    quarter = B_ROPE // 4
    heads_per_chunk = LANES // B_ROPE
    for hd in range(B_HEADS):
        xh = qr_t[hd * B_ROPE:(hd + 1) * B_ROPE]
        rot = jnp.concatenate([-xh[quarter:2 * quarter], xh[:quarter],
                               -xh[3 * quarter:], xh[2 * quarter:3 * quarter]], axis=0)
        rope = ((xh * ct + rot * st) * qscale).astype(BF16)
        j = hd % heads_per_chunk
        for t in range(tm // LANES):
            blk = rope[:, t * LANES:(t + 1) * LANES]
            pieces = [jnp.zeros((j * B_ROPE, LANES), BF16)] if j else []
            pieces.append(blk)
            if j + 1 < heads_per_chunk:
                pieces.append(jnp.zeros(((heads_per_chunk - 1 - j) * B_ROPE, LANES), BF16))
            qt_ref[0, t, LANES:, hd * LANES:(hd + 1) * LANES] = jnp.concatenate(pieces, axis=0)
    for m in range(B_HEADS // 2):
        qa_t = (_dot(wabs_ref[m], qn_t[m * LANES:(m + 1) * LANES]) * qscale).astype(BF16)
        for half in range(2):
            hd = 2 * m + half
            for t in range(tm // LANES):
                qt_ref[0, t, :LANES, hd * LANES:(hd + 1) * LANES] = (
                    qa_t[half * LANES:(half + 1) * LANES, t * LANES:(t + 1) * LANES])


def _proj_b(x, mod, norm_g, w1, wg, qg, wqn_t, wqr_t, wabs_t, kvg, tabs, tabs_t, tm):
    bsz, s, d = x.shape
    full = lambda a: pl.BlockSpec(a.shape, lambda b, t: (0,) * a.ndim)
    tab = pl.BlockSpec((tm, LANES), lambda b, t: (t, 0))
    tab_t = pl.BlockSpec((B_ROPE, tm), lambda b, t: (0, t))
    tok = lambda w: pl.BlockSpec((1, tm, w), lambda b, t: (b, t, 0))
    nt = tm // LANES
    return pl.pallas_call(
        functools.partial(_proj_b_kernel, qscale=(B_NOPE + B_ROPE) ** -0.5 * LOG2E, tm=tm),
        grid=(bsz, s // tm),
        in_specs=[tok(d), pl.BlockSpec((1, 3, d), lambda b, t: (b, 0, 0)), full(norm_g),
                  full(w1), full(wg), full(qg), full(wqn_t), full(wqr_t), full(wabs_t), full(kvg),
                  tab, tab, tab_t, tab_t],
        out_specs=[pl.BlockSpec((1, nt, B_FEAT, B_HEADS * LANES), lambda b, t: (b, t, 0, 0)),
                   tok(B_FEAT), pl.BlockSpec((1, B_KV_RANK, tm), lambda b, t: (b, 0, t)),
                   tok(B_WIDTH)],
        out_shape=[jax.ShapeDtypeStruct((bsz, s // LANES, B_FEAT, B_HEADS * LANES), BF16),
                   jax.ShapeDtypeStruct((bsz, s, B_FEAT), BF16),
                   jax.ShapeDtypeStruct((bsz, B_KV_RANK, s), BF16),
                   jax.ShapeDtypeStruct((bsz, s, B_WIDTH), BF16)],
        compiler_params=pltpu.CompilerParams(
            dimension_semantics=("parallel", "parallel"), vmem_limit_bytes=VMEM_LIMIT),
        name="proj_b",
    )(x, mod, norm_g, w1, wg, qg, wqn_t, wqr_t, wabs_t, kvg, *tabs, *tabs_t)


def _proj_b_ctx_kernel(x_ref, mod_ref, ng_ref, w1_ref, kvg_ref, k_ref, vt_ref):
    mod = mod_ref[0]
    h = (_rms(x_ref[0], ng_ref[...]) * (1.0 + mod[1:2]) + mod[0:1]).astype(BF16)
    k_ref[0], vt_ref[0] = _kext(_dot(h, w1_ref[...]), kvg_ref[...], None, None)


def _proj_b_ctx(ctx, mod, norm_g, w1, kvg):
    bsz, c, d = ctx.shape
    full = lambda a: pl.BlockSpec(a.shape, lambda b: (0,) * a.ndim)
    return pl.pallas_call(
        _proj_b_ctx_kernel,
        grid=(bsz,),
        in_specs=[pl.BlockSpec((1, c, d), lambda b: (b, 0, 0)),
                  pl.BlockSpec((1, 3, d), lambda b: (b, 0, 0)), full(norm_g), full(w1), full(kvg)],
        out_specs=[pl.BlockSpec((1, c, B_FEAT), lambda b: (b, 0, 0)),
                   pl.BlockSpec((1, B_KV_RANK, c), lambda b: (b, 0, 0))],
        out_shape=[jax.ShapeDtypeStruct((bsz, c, B_FEAT), BF16),
                   jax.ShapeDtypeStruct((bsz, B_KV_RANK, c), BF16)],
        compiler_params=pltpu.CompilerParams(dimension_semantics=("parallel",)),
        name="proj_b_ctx",
    )(ctx, mod, norm_g, w1, kvg)


def _flash_b_kernel(q_ref, k_ref, vt_ref, g_ref, x_ref, mod_ref, wvt_ref, wo_ref, fg_ref, out_ref,
                    m_ref, l_ref, acc_ref, s_ref, mx_ref, *, tiles, tk, nk, hb):
    tq = LANES
    rt = hb * tq
    m_ref[...] = jnp.full(m_ref.shape, NEG_INF, F32)
    l_ref[...] = jnp.zeros(l_ref.shape, F32)
    acc_ref[...] = jnp.zeros(acc_ref.shape, F32)

    hgroups = B_HEADS // hb
    nsb = tiles * hgroups
    nsteps = (nk // tk) * nsb
    unroll = max(u for u in range(2, MAX_UNROLL + 1, 2) if nsteps % u == 0)
    assert nsb & (nsb - 1) == 0 and hgroups & (hgroups - 1) == 0
    shift = nsb.bit_length() - 1
    hshift = hgroups.bit_length() - 1

    def split(k):
        return lax.shift_right_logical(k, shift), lax.bitwise_and(k, nsb - 1)

    def score_stage(k, slot):
        j, t = split(k)
        kc = k_ref[0, pl.ds(pl.multiple_of(j * tk, tk), tk), :]
        tile, hg = lax.shift_right_logical(t, hshift), lax.bitwise_and(t, hgroups - 1)
        st = _dot(kc, q_ref[0, tile, :, pl.ds(pl.multiple_of(hg * rt, rt), rt)])
        s_ref[slot] = st
        mx_ref[slot] = jnp.max(st.reshape(tk // 8, 8, rt), axis=0)

    def update_stage(k, slot):
        j, t = split(k)
        vt = vt_ref[0, :, pl.ds(pl.multiple_of(j * tk, tk), tk)]
        m_old = m_ref[t]
        m_new = jnp.maximum(m_old, jnp.max(mx_ref[slot], axis=0, keepdims=True))
        alpha = jnp.exp2(m_old - m_new)
        e = jnp.exp2(s_ref[slot] - m_new[0:1])
        l_ref[t] = alpha * l_ref[t] + jnp.sum(e.reshape(tk // 8, 8, rt), axis=0)
        acc_ref[t] = alpha[0:1] * acc_ref[t] + _dot(vt, e.astype(BF16))
        m_ref[t] = m_new

    score_stage(0, 0)

    def body(i, carry):
        for u in range(unroll):
            k = unroll * i + u
            score_stage(jnp.minimum(k + 1, nsteps - 1), (u + 1) % 2)
            update_stage(k, u % 2)
        return carry

    lax.fori_loop(0, nsteps // unroll, body, 0)
    rows = []
    for tile in range(tiles):
        outs = []
        for hg in range(hgroups):
            t = tile * hgroups + hg
            linv = 1.0 / jnp.sum(l_ref[t], axis=0, keepdims=True)
            ot = (acc_ref[t] * linv).astype(BF16)
            for p in range(hb // 2):
                pair = jnp.concatenate([ot[:, (2 * p) * tq:(2 * p + 1) * tq],
                                        ot[:, (2 * p + 1) * tq:(2 * p + 2) * tq]], axis=0)
                outs.append(_dot(wvt_ref[hg * (hb // 2) + p], pair))
        rows.append(jnp.concatenate(outs, axis=0).T)
    o = jnp.concatenate(rows, axis=0)
    z = (o * g_ref[0].astype(F32)).astype(BF16)
    xo = x_ref[0] + mod_ref[0][2:3] * _dot(z, wo_ref[...])
    out_ref[0] = _rms(xo, fg_ref[...])


def _flash_b(qp, kall, vtall, gact, x, mod, wvt, wo, final_g, max_tk, hb, tiles):
    bsz, s, d = x.shape
    nk = kall.shape[1]
    tq = tiles * LANES
    tk = max(t for t in range(LANES, max_tk + 1, LANES) if nk % t == 0)
    assert hb % 2 == 0 and B_HEADS % hb == 0
    nt = tiles * (B_HEADS // hb)
    tok = lambda w: pl.BlockSpec((1, tq, w), lambda b, i: (b, i, 0))
    full = lambda a: pl.BlockSpec(a.shape, lambda b, i: (0,) * a.ndim)
    return pl.pallas_call(
        functools.partial(_flash_b_kernel, tiles=tiles, tk=tk, nk=nk, hb=hb),
        grid=(bsz, s // tq),
        in_specs=[pl.BlockSpec((1, tiles, B_FEAT, B_HEADS * LANES), lambda b, i: (b, i, 0, 0)),
                  pl.BlockSpec((1, nk, B_FEAT), lambda b, i: (b, 0, 0)),
                  pl.BlockSpec((1, B_KV_RANK, nk), lambda b, i: (b, 0, 0)),
                  tok(B_WIDTH), tok(d), pl.BlockSpec((1, 3, d), lambda b, i: (b, 0, 0)),
                  full(wvt), full(wo), full(final_g)],
        out_specs=tok(d),
        out_shape=jax.ShapeDtypeStruct((bsz, s, d), F32),
        scratch_shapes=[pltpu.VMEM((nt, 8, hb * LANES), F32), pltpu.VMEM((nt, 8, hb * LANES), F32),
                        pltpu.VMEM((nt, B_KV_RANK, hb * LANES), F32),
                        pltpu.VMEM((2, tk, hb * LANES), F32), pltpu.VMEM((2, 8, hb * LANES), F32)],
        compiler_params=pltpu.CompilerParams(
            dimension_semantics=("parallel", "parallel"), vmem_limit_bytes=VMEM_LIMIT),
        name="flash_b",
    )(qp, kall, vtall, gact, x, mod, wvt, wo, final_g)


def _rope_cos_sin(n_tokens, rot_dim):
    n_rows = n_tokens // GRID_W
    row = jnp.broadcast_to(jnp.arange(n_rows)[:, None], (n_rows, GRID_W)).reshape(-1)
    col = jnp.broadcast_to(jnp.arange(GRID_W)[None, :], (n_rows, GRID_W)).reshape(-1)
    nf = rot_dim // 4
    inv = ROPE_BASE ** (-jnp.arange(nf, dtype=F32) / nf)
    ar = row.astype(F32)[:, None] * inv
    ac = col.astype(F32)[:, None] * inv
    ang = jnp.concatenate([ar, ar, ac, ac], axis=-1)
    return jnp.cos(ang), jnp.sin(ang)


def _rope_tables(n_tokens, rot_dim):
    reps = LANES // rot_dim
    cos, sin = _rope_cos_sin(n_tokens, rot_dim)
    return jnp.tile(cos, (1, reps)), jnp.tile(sin, (1, reps))


def _split_mod(modout, bsz):
    d = modout.shape[1] // 3
    m = modout.reshape(8, 3, d)
    lat = m[:bsz]
    ctx = jnp.broadcast_to(m[bsz:bsz + 1], (bsz, 3, d))
    return lat, ctx


def kernel(x, c, ctx, c_ctx, norm_g_0, ada_w_0, ada_b_0, a_w_in_0, a_sinks_0, a_w_o_0, norm_g_1,
           ada_w_1, ada_b_1, b_w_in_1, b_q_norm_1, b_w_uq_1, b_kv_norm_1, b_w_ukv_1, b_w_o_1, final_g):
    bsz, s, d = x.shape
    n_ctx = ctx.shape[1]
    assert bsz + 1 <= 8
    cc = jnp.concatenate([c, c_ctx[None], jnp.zeros((8 - bsz - 1, d), F32)], axis=0)
    row = lambda g: g.reshape(1, -1)

    mod_x, mod_c = _split_mod(_modulation(cc, ada_w_0, ada_b_0), bsz)
    w = a_w_in_0.astype(BF16)
    wq, wk = w[:, :A_WIDTH], w[:, A_WIDTH:A_WIDTH + A_KV_WIDTH]
    wv, wg = w[:, A_WIDTH + A_KV_WIDTH:A_WIDTH + 2 * A_KV_WIDTH], w[:, A_WIDTH + 2 * A_KV_WIDTH:]
    tabs_a = _rope_tables(s, A_HEAD_DIM)
    wo_a = a_w_o_0.astype(BF16)
    q, kd, vt, gact = _proj_a(x, mod_x, row(norm_g_0), wq, wk, wv, wg, tabs_a, tm=PROJ_A_TM)
    qc, kdc, vtc, gc = _proj_a(ctx, mod_c, row(norm_g_0), wq, wk, wv, wg, (), tm=n_ctx)
    x1 = _attn_a(a_sinks_0, q, kd, vt, kdc, vtc, gact, x, mod_x, wo_a, tq=ATTN_A_TQ)
    ctx1 = _attn_a_ctx(a_sinks_0, qc, kdc, vtc, gc, ctx, mod_c, wo_a)

    mod_x, mod_c = _split_mod(_modulation(cc, ada_w_1, ada_b_1), bsz)
    wb = b_w_in_1.astype(BF16)
    n_kv = B_Q_RANK + B_KV_RANK
    w1 = jnp.concatenate([wb[:, :n_kv]] + [wb[:, n_kv:n_kv + B_ROPE]] * (LANES // B_ROPE), axis=1)
    wgb = wb[:, n_kv + B_ROPE:]
    wuq = b_w_uq_1.astype(BF16).reshape(B_Q_RANK, B_HEADS, B_NOPE + B_ROPE)
    wqn_t = wuq[:, :, :B_NOPE].reshape(B_Q_RANK, B_HEADS * B_NOPE).T
    wqr_t = wuq[:, :, B_NOPE:].reshape(B_Q_RANK, B_HEADS * B_ROPE).T
    wukv = b_w_ukv_1.astype(BF16).reshape(B_KV_RANK, B_HEADS, B_NOPE + B_V)
    wkn_t = jnp.transpose(wukv[:, :, :B_NOPE], (1, 2, 0)).reshape(B_HEADS // 2, 2, B_NOPE, B_KV_RANK)
    zk = jnp.zeros_like(wkn_t[:, 0])
    wabs_t = jnp.transpose(
        jnp.concatenate([jnp.concatenate([wkn_t[:, 0], zk], axis=2),
                         jnp.concatenate([zk, wkn_t[:, 1]], axis=2)], axis=1), (0, 2, 1))
    wvh = jnp.transpose(wukv[:, :, B_NOPE:], (1, 0, 2)).reshape(B_HEADS // 2, 2, B_KV_RANK, B_V)
    zv = jnp.zeros_like(wvh[:, 0])
    wvup_t = jnp.transpose(
        jnp.concatenate([jnp.concatenate([wvh[:, 0], zv], axis=2),
                         jnp.concatenate([zv, wvh[:, 1]], axis=2)], axis=1), (0, 2, 1))
    tabs_b = _rope_tables(s, B_ROPE)
    cos_b, sin_b = _rope_cos_sin(s, B_ROPE)
    qp, kx, vtx, gb = _proj_b(x1, mod_x, row(norm_g_1), w1, wgb, row(b_q_norm_1), wqn_t, wqr_t,
                              wabs_t, row(b_kv_norm_1), tabs_b, (cos_b.T, sin_b.T), tm=PROJ_B_TM)
    kc, vtc = _proj_b_ctx(ctx1, mod_c, row(norm_g_1), w1, row(b_kv_norm_1))
    kall = jnp.concatenate([kc, kx], axis=1)
    vtall = jnp.concatenate([vtc, vtx], axis=2)
    return _flash_b(qp, kall, vtall, gb, x1, mod_x, wvup_t, b_w_o_1.astype(BF16), row(final_g),
                    max_tk=FLASH_MAX_TK, hb=FLASH_HEADS, tiles=FLASH_TILES)
```

```python
import functools

import jax
import jax.numpy as jnp
from jax import lax
from jax.experimental import pallas as pl
from jax.experimental.pallas import tpu as pltpu

F32 = jnp.float32
BF16 = jnp.bfloat16

GRID_W = 64
ROPE_BASE = 10000.0
EPS = 1e-6
NEG_INF = -1e30
LOG2E = 1.4426950408889634
LANES = 128
SUBLANES = 8
WINDOW = 128

A_HEADS, A_KV_HEADS, A_HEAD_DIM = 16, 4, 64
A_WIDTH = A_HEADS * A_HEAD_DIM
A_KV_WIDTH = A_KV_HEADS * A_HEAD_DIM
B_HEADS, B_NOPE, B_ROPE, B_V = 16, 64, 32, 64
B_Q_RANK, B_KV_RANK = 256, 128
B_WIDTH = B_HEADS * B_V
B_FEAT = 2 * LANES

VMEM_LIMIT = 56 * 1024 * 1024
MAX_UNROLL = 24

PROJ_A_TM = 1024
ATTN_A_TQ = 256
PROJ_B_TM = 512
FLASH_MAX_TK = 1408
FLASH_HEADS = 4
FLASH_TILES = 4


def _silu(x):
    return x * (1.0 / (1.0 + jnp.exp(-x)))


def _rms(x, g):
    return x * lax.rsqrt(jnp.mean(x * x, axis=-1, keepdims=True) + EPS) * g


def _rope(x, cos, sin, quarter):
    if cos is None:
        return x
    first = (lax.broadcasted_iota(jnp.int32, x.shape, 1) & (2 * quarter - 1)) < quarter
    rot = jnp.where(first, -pltpu.roll(x, LANES - quarter, 1), pltpu.roll(x, quarter, 1))
    return x * cos + rot * sin


def _dot(a, b):
    return jnp.dot(a, b, preferred_element_type=F32)


def _dot_nt(a, b):
    return lax.dot_general(a, b, (((1,), (1,)), ((), ())), preferred_element_type=F32)


def _mod_kernel(c_ref, w_ref, b_ref, o_ref):
    a = _silu(c_ref[...])
    o_ref[...] = jnp.dot(a, w_ref[...], precision=lax.Precision.HIGHEST,
                         preferred_element_type=F32) + b_ref[...]


def _modulation(cc, ada_w, ada_b):
    d = ada_w.shape[0]
    n = ada_w.shape[1]
    return pl.pallas_call(
        _mod_kernel,
        grid=(n // d,),
        in_specs=[pl.BlockSpec((SUBLANES, d), lambda j: (0, 0)),
                  pl.BlockSpec((d, d), lambda j: (0, j)),
                  pl.BlockSpec((1, d), lambda j: (0, j))],
        out_specs=pl.BlockSpec((SUBLANES, d), lambda j: (0, j)),
        out_shape=jax.ShapeDtypeStruct((SUBLANES, n), F32),
        name="adaln_mod",
    )(cc, ada_w, ada_b.reshape(1, n))


def _proj_a_kernel(x_ref, mod_ref, ng_ref, wq_ref, wk_ref, wv_ref, wg_ref, *rest, qscale, rope):
    cos, sin = (rest[0][...], rest[1][...]) if rope else (None, None)
    q_ref, kd_ref, vt_ref, g_ref = rest[-4:]
    mod = mod_ref[0]
    h = (_rms(x_ref[0], ng_ref[...]) * (1.0 + mod[1:2]) + mod[0:1]).astype(BF16)
    quarter = A_HEAD_DIM // 4
    q = _dot(h, wq_ref[...])
    for j in range(A_WIDTH // LANES):
        sl = slice(j * LANES, (j + 1) * LANES)
        q_ref[0, :, sl] = (_rope(q[:, sl], cos, sin, quarter) * qscale).astype(BF16)
    lo = lax.broadcasted_iota(jnp.int32, (q.shape[0], LANES), 1) < A_HEAD_DIM
    k = _dot(h, wk_ref[...])
    for m in range(A_KV_WIDTH // LANES):
        sl = slice(m * LANES, (m + 1) * LANES)
        kp = _rope(k[:, sl], cos, sin, quarter)
        ks = pltpu.roll(kp, A_HEAD_DIM, 1)
        kd_ref[0, :, 2 * m * LANES:(2 * m + 1) * LANES] = jnp.where(lo, kp, ks).astype(BF16)
        kd_ref[0, :, (2 * m + 1) * LANES:(2 * m + 2) * LANES] = jnp.where(lo, ks, kp).astype(BF16)
    vt_ref[0] = _dot(h, wv_ref[...]).astype(BF16).T
    g_ref[0] = _silu(_dot(h, wg_ref[...])).astype(BF16)


def _proj_a(x, mod, norm_g, wq, wk, wv, wg, tabs, tm):
    bsz, s, d = x.shape
    assert s % tm == 0 and tm % LANES == 0
    kdw = A_KV_HEADS * LANES
    full = lambda a: pl.BlockSpec(a.shape, lambda b, t: (0,) * a.ndim)
    tab = pl.BlockSpec((tm, LANES), lambda b, t: (t, 0))
    tok = lambda w: pl.BlockSpec((1, tm, w), lambda b, t: (b, t, 0))
    return pl.pallas_call(
        functools.partial(_proj_a_kernel, qscale=A_HEAD_DIM ** -0.5 * LOG2E, rope=bool(tabs)),
        grid=(bsz, s // tm),
        in_specs=[tok(d), pl.BlockSpec((1, 3, d), lambda b, t: (b, 0, 0)), full(norm_g),
                  full(wq), full(wk), full(wv), full(wg)] + [tab] * len(tabs),
        out_specs=[tok(A_WIDTH), tok(kdw),
                   pl.BlockSpec((1, A_KV_WIDTH, tm), lambda b, t: (b, 0, t)), tok(A_WIDTH)],
        out_shape=[jax.ShapeDtypeStruct((bsz, s, A_WIDTH), BF16),
                   jax.ShapeDtypeStruct((bsz, s, kdw), BF16),
                   jax.ShapeDtypeStruct((bsz, A_KV_WIDTH, s), BF16),
                   jax.ShapeDtypeStruct((bsz, s, A_WIDTH), BF16)],
        compiler_params=pltpu.CompilerParams(
            dimension_semantics=("parallel", "parallel"), vmem_limit_bytes=VMEM_LIMIT),
        name="proj_a",
    )(x, mod, norm_g, wq, wk, wv, wg, *tabs)


def _attn_a_core(sink_ref, q, kd, vt, cap_loc, tq):
    group = A_HEADS // A_KV_HEADS
    nk = kd.shape[0]
    lane = lax.broadcasted_iota(jnp.int32, (1, 2 * tq), 1)
    klane = lax.broadcasted_iota(jnp.int32, (nk, LANES), 1)
    steps = [(g, par) for g in range(A_KV_HEADS) for par in range(2)]

    def scores(g, par):
        qrows = jnp.concatenate([q[:, (2 * g) * LANES:(2 * g + 1) * LANES],
                                 q[:, (2 * g + 1) * LANES:(2 * g + 2) * LANES]], axis=0)
        kg = kd[:, g * LANES:(g + 1) * LANES]
        keep = (klane < A_HEAD_DIM) if par == 0 else (klane >= A_HEAD_DIM)
        return _dot_nt(jnp.where(keep, kg, jnp.zeros_like(kg)), qrows)

    def finish(g, par, st):
        h0 = group * g + par
        sink = jnp.where(lane < tq, sink_ref[h0] * LOG2E, sink_ref[h0 + 2] * LOG2E)
        if cap_loc is not None:
            nl = cap_loc.shape[0]
            st = jnp.concatenate([jnp.minimum(st[:nl], cap_loc), st[nl:]], axis=0)
        m = jnp.maximum(sink, jnp.max(st, axis=0, keepdims=True))
        e = jnp.exp2(st - m)
        l = jnp.exp2(sink - m) + jnp.sum(e, axis=0, keepdims=True)
        ot = _dot(vt[g * A_HEAD_DIM:(g + 1) * A_HEAD_DIM], e.astype(BF16)) * (1.0 / l)
        return {h0: ot[:, :tq], h0 + 2: ot[:, tq:]}

    heads = {}
    cur = scores(*steps[0])
    for i, (g, par) in enumerate(steps):
        nxt = scores(*steps[i + 1]) if i + 1 < len(steps) else None
        heads.update(finish(g, par, cur))
        cur = nxt
    return jnp.concatenate([heads[h] for h in range(A_HEADS)], axis=0).T


def _attn_a_finish(o, g_ref, x_ref, mod_ref, wo_ref, out_ref):
    z = (o * g_ref[0].astype(F32)).astype(BF16)
    out_ref[0] = x_ref[0] + mod_ref[0][2:3] * _dot(z, wo_ref[...])


def _attn_a_kernel(sink_ref, q_ref, kp_ref, kc_ref, kn_ref, kx_ref, vp_ref, vc_ref, vn_ref, vx_ref,
                   g_ref, x_ref, mod_ref, wo_ref, out_ref, *, tq, seq):
    i = pl.program_id(1)
    kd = jnp.concatenate([kp_ref[0], kc_ref[0], kn_ref[0], kx_ref[0]], axis=0)
    vt = jnp.concatenate([vp_ref[0], vc_ref[0], vn_ref[0], vx_ref[0]], axis=1)
    nl = tq + 2 * WINDOW
    krow = lax.broadcasted_iota(jnp.int32, (nl, 2 * tq), 0)
    qcol = lax.broadcasted_iota(jnp.int32, (nl, 2 * tq), 1) & (tq - 1)
    kpos = i * tq - WINDOW + krow
    ok = (jnp.abs(qcol + WINDOW - krow) <= WINDOW) & (kpos >= 0) & (kpos < seq)
    o = _attn_a_core(sink_ref, q_ref[0], kd, vt, jnp.where(ok, jnp.inf, NEG_INF), tq)
    _attn_a_finish(o, g_ref, x_ref, mod_ref, wo_ref, out_ref)


def _attn_a_ctx_kernel(sink_ref, q_ref, kx_ref, vx_ref, g_ref, x_ref, mod_ref, wo_ref, out_ref, *, tq):
    o = _attn_a_core(sink_ref, q_ref[0], kx_ref[0], vx_ref[0], None, tq)
    _attn_a_finish(o, g_ref, x_ref, mod_ref, wo_ref, out_ref)


def _attn_a(sinks, q, kd, vt, kdc, vtc, gact, x, mod, wo, tq):
    bsz, s, d = x.shape
    c = kdc.shape[1]
    kdw = kd.shape[2]
    vw = vt.shape[1]
    assert tq & (tq - 1) == 0 and tq % WINDOW == 0 and s % tq == 0
    r = tq // WINDOW
    nblk = s // WINDOW
    tok = lambda w: pl.BlockSpec((1, tq, w), lambda b, i: (b, i, 0))
    prev_i = lambda i: jnp.maximum(i * r - 1, 0)
    next_i = lambda i: jnp.minimum((i + 1) * r, nblk - 1)
    return pl.pallas_call(
        functools.partial(_attn_a_kernel, tq=tq, seq=s),
        grid=(bsz, s // tq),
        in_specs=[pl.BlockSpec(memory_space=pltpu.SMEM), tok(A_WIDTH),
                  pl.BlockSpec((1, WINDOW, kdw), lambda b, i: (b, prev_i(i), 0)), tok(kdw),
                  pl.BlockSpec((1, WINDOW, kdw), lambda b, i: (b, next_i(i), 0)),
                  pl.BlockSpec((1, c, kdw), lambda b, i: (b, 0, 0)),
                  pl.BlockSpec((1, vw, WINDOW), lambda b, i: (b, 0, prev_i(i))),
                  pl.BlockSpec((1, vw, tq), lambda b, i: (b, 0, i)),
                  pl.BlockSpec((1, vw, WINDOW), lambda b, i: (b, 0, next_i(i))),
                  pl.BlockSpec((1, vw, c), lambda b, i: (b, 0, 0)),
                  tok(A_WIDTH), tok(d), pl.BlockSpec((1, 3, d), lambda b, i: (b, 0, 0)),
                  pl.BlockSpec(wo.shape, lambda b, i: (0, 0))],
        out_specs=tok(d),
        out_shape=jax.ShapeDtypeStruct((bsz, s, d), F32),
        compiler_params=pltpu.CompilerParams(
            dimension_semantics=("parallel", "parallel"), vmem_limit_bytes=VMEM_LIMIT),
        name="attn_a",
    )(sinks, q, kd, kd, kd, kdc, vt, vt, vt, vtc, gact, x, mod, wo)


def _attn_a_ctx(sinks, qc, kdc, vtc, gc, ctx, mod, wo):
    bsz, c, d = ctx.shape
    tok = lambda w: pl.BlockSpec((1, c, w), lambda b: (b, 0, 0))
    return pl.pallas_call(
        functools.partial(_attn_a_ctx_kernel, tq=c),
        grid=(bsz,),
        in_specs=[pl.BlockSpec(memory_space=pltpu.SMEM), tok(A_WIDTH), tok(kdc.shape[2]),
                  pl.BlockSpec((1, vtc.shape[1], c), lambda b: (b, 0, 0)),
                  tok(A_WIDTH), tok(d), pl.BlockSpec((1, 3, d), lambda b: (b, 0, 0)),
                  pl.BlockSpec(wo.shape, lambda b: (0, 0))],
        out_specs=tok(d),
        out_shape=jax.ShapeDtypeStruct((bsz, c, d), F32),
        compiler_params=pltpu.CompilerParams(
            dimension_semantics=("parallel",), vmem_limit_bytes=VMEM_LIMIT),
        name="attn_a_ctx",
    )(sinks, qc, kdc, vtc, gc, ctx, mod, wo)


def _kext(a, kvg, cos, sin):
    ckv = _rms(a[:, B_Q_RANK:B_Q_RANK + B_KV_RANK], kvg)
    kr = a[:, B_Q_RANK + B_KV_RANK:]
    kr = _rope(kr, cos, sin, B_ROPE // 4)
    return jnp.concatenate([ckv, kr], axis=1).astype(BF16), ckv.T.astype(BF16)


def _proj_b_kernel(x_ref, mod_ref, ng_ref, w1_ref, wg_ref, qg_ref, wqn_ref, wqr_ref, wabs_ref,
                   kvg_ref, c_ref, s_ref, ct_ref, st_ref, qt_ref, k_ref, vt_ref, g_ref,
                   *, qscale, tm):
    mod = mod_ref[0]
    h = (_rms(x_ref[0], ng_ref[...]) * (1.0 + mod[1:2]) + mod[0:1]).astype(BF16)
    a = _dot(h, w1_ref[...])
    k_ref[0], vt_ref[0] = _kext(a, kvg_ref[...], c_ref[...], s_ref[...])
    g_ref[0] = _silu(_dot(h, wg_ref[...])).astype(BF16)
    cq_t = _rms(a[:, :B_Q_RANK], qg_ref[...]).T.astype(BF16)
    qn_t = _dot(wqn_ref[...], cq_t).astype(BF16)
    qr_t = _dot(wqr_ref[...], cq_t)
    ct, st = ct_ref[...], st_ref[...]
    quarter = B_ROPE // 4
    heads_per_chunk = LANES // B_ROPE
    for hd in range(B_HEADS):
        xh = qr_t[hd * B_ROPE:(hd + 1) * B_ROPE]
        rot = jnp.concatenate([-xh[quarter:2 * quarter], xh[:quarter],
                               -xh[3 * quarter:], xh[2 * quarter:3 * quarter]], axis=0)
        rope = ((xh * ct + rot * st) * qscale).astype(BF16)
        j = hd % heads_per_chunk
        for t in range(tm // LANES):
            blk = rope[:, t * LANES:(t + 1) * LANES]
            pieces = [jnp.zeros((j * B_ROPE, LANES), BF16)] if j else []
            pieces.append(blk)
            if j + 1 < heads_per_chunk:
                pieces.append(jnp.zeros(((heads_per_chunk - 1 - j) * B_ROPE, LANES), BF16))
            qt_ref[0, t, LANES:, hd * LANES:(hd + 1) * LANES] = jnp.concatenate(pieces, axis=0)
    for m in range(B_HEADS // 2):
        qa_t = (_dot(wabs_ref[m], qn_t[m * LANES:(m + 1) * LANES]) * qscale).astype(BF16)
        for half in range(2):
            hd = 2 * m + half
            for t in range(tm // LANES):
                qt_ref[0, t, :LANES, hd * LANES:(hd + 1) * LANES] = (
                    qa_t[half * LANES:(half + 1) * LANES, t * LANES:(t + 1) * LANES])


def _proj_b(x, mod, norm_g, w1, wg, qg, wqn_t, wqr_t, wabs_t, kvg, tabs, tabs_t, tm):
    bsz, s, d = x.shape
    full = lambda a: pl.BlockSpec(a.shape, lambda b, t: (0,) * a.ndim)
    tab = pl.BlockSpec((tm, LANES), lambda b, t: (t, 0))
    tab_t = pl.BlockSpec((B_ROPE, tm), lambda b, t: (0, t))
    tok = lambda w: pl.BlockSpec((1, tm, w), lambda b, t: (b, t, 0))
    assert s % tm == 0 and tm % LANES == 0
    nt = tm // LANES
    return pl.pallas_call(
        functools.partial(_proj_b_kernel, qscale=(B_NOPE + B_ROPE) ** -0.5 * LOG2E, tm=tm),
        grid=(bsz, s // tm),
        in_specs=[tok(d), pl.BlockSpec((1, 3, d), lambda b, t: (b, 0, 0)), full(norm_g),
                  full(w1), full(wg), full(qg), full(wqn_t), full(wqr_t), full(wabs_t), full(kvg),
                  tab, tab, tab_t, tab_t],
        out_specs=[pl.BlockSpec((1, nt, B_FEAT, B_HEADS * LANES), lambda b, t: (b, t, 0, 0)),
                   tok(B_FEAT), pl.BlockSpec((1, B_KV_RANK, tm), lambda b, t: (b, 0, t)),
                   tok(B_WIDTH)],
        out_shape=[jax.ShapeDtypeStruct((bsz, s // LANES, B_FEAT, B_HEADS * LANES), BF16),
                   jax.ShapeDtypeStruct((bsz, s, B_FEAT), BF16),
                   jax.ShapeDtypeStruct((bsz, B_KV_RANK, s), BF16),
                   jax.ShapeDtypeStruct((bsz, s, B_WIDTH), BF16)],
        compiler_params=pltpu.CompilerParams(
            dimension_semantics=("parallel", "parallel"), vmem_limit_bytes=VMEM_LIMIT),
        name="proj_b",
    )(x, mod, norm_g, w1, wg, qg, wqn_t, wqr_t, wabs_t, kvg, *tabs, *tabs_t)


def _proj_b_ctx_kernel(x_ref, mod_ref, ng_ref, w1_ref, kvg_ref, k_ref, vt_ref):
    mod = mod_ref[0]
    h = (_rms(x_ref[0], ng_ref[...]) * (1.0 + mod[1:2]) + mod[0:1]).astype(BF16)
    k_ref[0], vt_ref[0] = _kext(_dot(h, w1_ref[...]), kvg_ref[...], None, None)


def _proj_b_ctx(ctx, mod, norm_g, w1, kvg):
    bsz, c, d = ctx.shape
    full = lambda a: pl.BlockSpec(a.shape, lambda b: (0,) * a.ndim)
    return pl.pallas_call(
        _proj_b_ctx_kernel,
        grid=(bsz,),
        in_specs=[pl.BlockSpec((1, c, d), lambda b: (b, 0, 0)),
                  pl.BlockSpec((1, 3, d), lambda b: (b, 0, 0)), full(norm_g), full(w1), full(kvg)],
        out_specs=[pl.BlockSpec((1, c, B_FEAT), lambda b: (b, 0, 0)),
                   pl.BlockSpec((1, B_KV_RANK, c), lambda b: (b, 0, 0))],
        out_shape=[jax.ShapeDtypeStruct((bsz, c, B_FEAT), BF16),
                   jax.ShapeDtypeStruct((bsz, B_KV_RANK, c), BF16)],
        compiler_params=pltpu.CompilerParams(dimension_semantics=("parallel",)),
        name="proj_b_ctx",
    )(ctx, mod, norm_g, w1, kvg)


def _flash_b_kernel(q_ref, k_ref, vt_ref, g_ref, x_ref, mod_ref, wvt_ref, wo_ref, fg_ref, out_ref,
                    m_ref, l_ref, acc_ref, s_ref, mx_ref, *, tiles, tk, nk, hb):
    tq = LANES
    rt = hb * tq
    m_ref[...] = jnp.full(m_ref.shape, NEG_INF, F32)
    l_ref[...] = jnp.zeros(l_ref.shape, F32)
    acc_ref[...] = jnp.zeros(acc_ref.shape, F32)

    hgroups = B_HEADS // hb
    nsb = tiles * hgroups
    nsteps = (nk // tk) * nsb
    unroll = max(u for u in range(2, MAX_UNROLL + 1, 2) if nsteps % u == 0)
    assert nsb & (nsb - 1) == 0 and hgroups & (hgroups - 1) == 0
    shift = nsb.bit_length() - 1
    hshift = hgroups.bit_length() - 1

    def split(k):
        return lax.shift_right_logical(k, shift), lax.bitwise_and(k, nsb - 1)

    def score_stage(k, slot):
        j, t = split(k)
        kc = k_ref[0, pl.ds(pl.multiple_of(j * tk, tk), tk), :]
        tile, hg = lax.shift_right_logical(t, hshift), lax.bitwise_and(t, hgroups - 1)
        st = _dot(kc, q_ref[0, tile, :, pl.ds(pl.multiple_of(hg * rt, rt), rt)])
        s_ref[slot] = st
        mx_ref[slot] = jnp.max(st.reshape(tk // SUBLANES, SUBLANES, rt), axis=0)

    def update_stage(k, slot):
        j, t = split(k)
        vt = vt_ref[0, :, pl.ds(pl.multiple_of(j * tk, tk), tk)]
        m_old = m_ref[t]
        m_new = jnp.maximum(m_old, jnp.max(mx_ref[slot], axis=0, keepdims=True))
        alpha = jnp.exp2(m_old - m_new)
        e = jnp.exp2(s_ref[slot] - m_new[0:1])
        l_ref[t] = alpha * l_ref[t] + jnp.sum(e.reshape(tk // SUBLANES, SUBLANES, rt), axis=0)
        acc_ref[t] = alpha[0:1] * acc_ref[t] + _dot(vt, e.astype(BF16))
        m_ref[t] = m_new

    score_stage(0, 0)

    def body(i, carry):
        for u in range(unroll):
            k = unroll * i + u
            score_stage(jnp.minimum(k + 1, nsteps - 1), (u + 1) % 2)
            update_stage(k, u % 2)
        return carry

    lax.fori_loop(0, nsteps // unroll, body, 0)
    rows = []
    for tile in range(tiles):
        outs = []
        for hg in range(hgroups):
            t = tile * hgroups + hg
            linv = 1.0 / jnp.sum(l_ref[t], axis=0, keepdims=True)
            ot = (acc_ref[t] * linv).astype(BF16)
            for p in range(hb // 2):
                pair = jnp.concatenate([ot[:, (2 * p) * tq:(2 * p + 1) * tq],
                                        ot[:, (2 * p + 1) * tq:(2 * p + 2) * tq]], axis=0)
                outs.append(_dot(wvt_ref[hg * (hb // 2) + p], pair))
        rows.append(jnp.concatenate(outs, axis=0).T)
    o = jnp.concatenate(rows, axis=0)
    z = (o * g_ref[0].astype(F32)).astype(BF16)
    xo = x_ref[0] + mod_ref[0][2:3] * _dot(z, wo_ref[...])
    out_ref[0] = _rms(xo, fg_ref[...])


def _flash_b(qp, kall, vtall, gact, x, mod, wvt, wo, final_g, max_tk, hb, tiles):
    bsz, s, d = x.shape
    nk = kall.shape[1]
    tq = tiles * LANES
    tk = max(t for t in range(LANES, max_tk + 1, LANES) if nk % t == 0)
    assert hb % 2 == 0 and B_HEADS % hb == 0 and s % tq == 0 and tk % SUBLANES == 0
    nt = tiles * (B_HEADS // hb)
    tok = lambda w: pl.BlockSpec((1, tq, w), lambda b, i: (b, i, 0))
    full = lambda a: pl.BlockSpec(a.shape, lambda b, i: (0,) * a.ndim)
    return pl.pallas_call(
        functools.partial(_flash_b_kernel, tiles=tiles, tk=tk, nk=nk, hb=hb),
        grid=(bsz, s // tq),
        in_specs=[pl.BlockSpec((1, tiles, B_FEAT, B_HEADS * LANES), lambda b, i: (b, i, 0, 0)),
                  pl.BlockSpec((1, nk, B_FEAT), lambda b, i: (b, 0, 0)),
                  pl.BlockSpec((1, B_KV_RANK, nk), lambda b, i: (b, 0, 0)),
                  tok(B_WIDTH), tok(d), pl.BlockSpec((1, 3, d), lambda b, i: (b, 0, 0)),
                  full(wvt), full(wo), full(final_g)],
        out_specs=tok(d),
        out_shape=jax.ShapeDtypeStruct((bsz, s, d), F32),
        scratch_shapes=[pltpu.VMEM((nt, SUBLANES, hb * LANES), F32),
                        pltpu.VMEM((nt, SUBLANES, hb * LANES), F32),
                        pltpu.VMEM((nt, B_KV_RANK, hb * LANES), F32),
                        pltpu.VMEM((2, tk, hb * LANES), F32),
                        pltpu.VMEM((2, SUBLANES, hb * LANES), F32)],
        compiler_params=pltpu.CompilerParams(
            dimension_semantics=("parallel", "parallel"), vmem_limit_bytes=VMEM_LIMIT),
        name="flash_b",
    )(qp, kall, vtall, gact, x, mod, wvt, wo, final_g)


def _rope_cos_sin(n_tokens, rot_dim):
    n_rows = n_tokens // GRID_W
    row = jnp.broadcast_to(jnp.arange(n_rows)[:, None], (n_rows, GRID_W)).reshape(-1)
    col = jnp.broadcast_to(jnp.arange(GRID_W)[None, :], (n_rows, GRID_W)).reshape(-1)
    nf = rot_dim // 4
    inv = ROPE_BASE ** (-jnp.arange(nf, dtype=F32) / nf)
    ar = row.astype(F32)[:, None] * inv
    ac = col.astype(F32)[:, None] * inv
    ang = jnp.concatenate([ar, ar, ac, ac], axis=-1)
    return jnp.cos(ang), jnp.sin(ang)


def _rope_tables(n_tokens, rot_dim):
    reps = LANES // rot_dim
    cos, sin = _rope_cos_sin(n_tokens, rot_dim)
    return jnp.tile(cos, (1, reps)), jnp.tile(sin, (1, reps))


def _split_mod(modout, bsz):
    d = modout.shape[1] // 3
    m = modout.reshape(SUBLANES, 3, d)
    lat = m[:bsz]
    ctx = jnp.broadcast_to(m[bsz:bsz + 1], (bsz, 3, d))
    return lat, ctx


def kernel(x, c, ctx, c_ctx, norm_g_0, ada_w_0, ada_b_0, a_w_in_0, a_sinks_0, a_w_o_0, norm_g_1,
           ada_w_1, ada_b_1, b_w_in_1, b_q_norm_1, b_w_uq_1, b_kv_norm_1, b_w_ukv_1, b_w_o_1, final_g):
    bsz, s, d = x.shape
    n_ctx = ctx.shape[1]
    assert bsz + 1 <= SUBLANES and s % GRID_W == 0 and n_ctx % LANES == 0
    cc = jnp.concatenate([c, c_ctx[None], jnp.zeros((SUBLANES - bsz - 1, d), F32)], axis=0)
    row = lambda g: g.reshape(1, -1)

    mod_x, mod_c = _split_mod(_modulation(cc, ada_w_0, ada_b_0), bsz)
    w = a_w_in_0.astype(BF16)
    wq, wk = w[:, :A_WIDTH], w[:, A_WIDTH:A_WIDTH + A_KV_WIDTH]
    wv, wg = w[:, A_WIDTH + A_KV_WIDTH:A_WIDTH + 2 * A_KV_WIDTH], w[:, A_WIDTH + 2 * A_KV_WIDTH:]
    tabs_a = _rope_tables(s, A_HEAD_DIM)
    wo_a = a_w_o_0.astype(BF16)
    q, kd, vt, gact = _proj_a(x, mod_x, row(norm_g_0), wq, wk, wv, wg, tabs_a, tm=PROJ_A_TM)
    qc, kdc, vtc, gc = _proj_a(ctx, mod_c, row(norm_g_0), wq, wk, wv, wg, (), tm=n_ctx)
    x1 = _attn_a(a_sinks_0, q, kd, vt, kdc, vtc, gact, x, mod_x, wo_a, tq=ATTN_A_TQ)
    ctx1 = _attn_a_ctx(a_sinks_0, qc, kdc, vtc, gc, ctx, mod_c, wo_a)

    mod_x, mod_c = _split_mod(_modulation(cc, ada_w_1, ada_b_1), bsz)
    wb = b_w_in_1.astype(BF16)
    n_kv = B_Q_RANK + B_KV_RANK
    w1 = jnp.concatenate([wb[:, :n_kv]] + [wb[:, n_kv:n_kv + B_ROPE]] * (LANES // B_ROPE), axis=1)
    wgb = wb[:, n_kv + B_ROPE:]
    wuq = b_w_uq_1.astype(BF16).reshape(B_Q_RANK, B_HEADS, B_NOPE + B_ROPE)
    wqn_t = wuq[:, :, :B_NOPE].reshape(B_Q_RANK, B_HEADS * B_NOPE).T
    wqr_t = wuq[:, :, B_NOPE:].reshape(B_Q_RANK, B_HEADS * B_ROPE).T
    wukv = b_w_ukv_1.astype(BF16).reshape(B_KV_RANK, B_HEADS, B_NOPE + B_V)
    wkn_t = jnp.transpose(wukv[:, :, :B_NOPE], (1, 2, 0)).reshape(B_HEADS // 2, 2, B_NOPE, B_KV_RANK)
    zk = jnp.zeros_like(wkn_t[:, 0])
    wabs_t = jnp.transpose(
        jnp.concatenate([jnp.concatenate([wkn_t[:, 0], zk], axis=2),
                         jnp.concatenate([zk, wkn_t[:, 1]], axis=2)], axis=1), (0, 2, 1))
    wvh = jnp.transpose(wukv[:, :, B_NOPE:], (1, 0, 2)).reshape(B_HEADS // 2, 2, B_KV_RANK, B_V)
    zv = jnp.zeros_like(wvh[:, 0])
    wvup_t = jnp.transpose(
        jnp.concatenate([jnp.concatenate([wvh[:, 0], zv], axis=2),
                         jnp.concatenate([zv, wvh[:, 1]], axis=2)], axis=1), (0, 2, 1))
    tabs_b = _rope_tables(s, B_ROPE)
    cos_b, sin_b = _rope_cos_sin(s, B_ROPE)
    qp, kx, vtx, gb = _proj_b(x1, mod_x, row(norm_g_1), w1, wgb, row(b_q_norm_1), wqn_t, wqr_t,
                              wabs_t, row(b_kv_norm_1), tabs_b, (cos_b.T, sin_b.T), tm=PROJ_B_TM)
    kc, vtc = _proj_b_ctx(ctx1, mod_c, row(norm_g_1), w1, row(b_kv_norm_1))
    kall = jnp.concatenate([kc, kx], axis=1)
    vtall = jnp.concatenate([vtc, vtx], axis=2)
    return _flash_b(qp, kall, vtall, gb, x1, mod_x, wvup_t, b_w_o_1.astype(BF16), row(final_g),
                    max_tk=FLASH_MAX_TK, hb=FLASH_HEADS, tiles=FLASH_TILES)
```

```python
import functools

import jax
import jax.numpy as jnp
from jax import lax
from jax.experimental import pallas as pl
from jax.experimental.pallas import tpu as pltpu

F32 = jnp.float32
BF16 = jnp.bfloat16

GRID_W = 64
ROPE_BASE = 10000.0
EPS = 1e-6
NEG_INF = -1e30
LOG2E = 1.4426950408889634
LANES = 128
SUBLANES = 8
WINDOW = 128

A_HEADS, A_KV_HEADS, A_HEAD_DIM = 16, 4, 64
A_WIDTH = A_HEADS * A_HEAD_DIM
A_KV_WIDTH = A_KV_HEADS * A_HEAD_DIM
B_HEADS, B_NOPE, B_ROPE, B_V = 16, 64, 32, 64
B_Q_RANK, B_KV_RANK = 256, 128
B_WIDTH = B_HEADS * B_V
B_FEAT = 2 * LANES

VMEM_LIMIT = 56 * 1024 * 1024
MAX_UNROLL = 24

PROJ_A_TM = 1024
ATTN_A_TQ = 256
PROJ_B_TM = 512
FLASH_MAX_TK = 1408
FLASH_HEADS = 4
FLASH_TILES = 4


def _silu(x):
    return x * (1.0 / (1.0 + jnp.exp(-x)))


def _rms(x, g):
    return x * lax.rsqrt(jnp.mean(x * x, axis=-1, keepdims=True) + EPS) * g


def _rope(x, cos, sin, quarter):
    if cos is None:
        return x
    first = (lax.broadcasted_iota(jnp.int32, x.shape, 1) & (2 * quarter - 1)) < quarter
    rot = jnp.where(first, -pltpu.roll(x, LANES - quarter, 1), pltpu.roll(x, quarter, 1))
    return x * cos + rot * sin


def _dot(a, b):
    return jnp.dot(a, b, preferred_element_type=F32)


def _dot_nt(a, b):
    return lax.dot_general(a, b, (((1,), (1,)), ((), ())), preferred_element_type=F32)


def _mod_kernel(c_ref, w_ref, b_ref, o_ref):
    a = _silu(c_ref[...])
    o_ref[...] = jnp.dot(a, w_ref[...], precision=lax.Precision.HIGHEST,
                         preferred_element_type=F32) + b_ref[...]


def _modulation(cc, ada_w, ada_b):
    d = ada_w.shape[0]
    n = ada_w.shape[1]
    return pl.pallas_call(
        _mod_kernel,
        grid=(n // d,),
        in_specs=[pl.BlockSpec((SUBLANES, d), lambda j: (0, 0)),
                  pl.BlockSpec((d, d), lambda j: (0, j)),
                  pl.BlockSpec((1, d), lambda j: (0, j))],
        out_specs=pl.BlockSpec((SUBLANES, d), lambda j: (0, j)),
        out_shape=jax.ShapeDtypeStruct((SUBLANES, n), F32),
        name="adaln_mod",
    )(cc, ada_w, ada_b.reshape(1, n))


def _proj_a_kernel(x_ref, mod_ref, ng_ref, wq_ref, wk_ref, wv_ref, wg_ref, *rest, qscale, rope):
    cos, sin = (rest[0][...], rest[1][...]) if rope else (None, None)
    q_ref, kd_ref, vt_ref, g_ref = rest[-4:]
    mod = mod_ref[0]
    h = (_rms(x_ref[0], ng_ref[...]) * (1.0 + mod[1:2]) + mod[0:1]).astype(BF16)
    quarter = A_HEAD_DIM // 4
    q = _dot(h, wq_ref[...])
    for j in range(A_WIDTH // LANES):
        sl = slice(j * LANES, (j + 1) * LANES)
        q_ref[0, :, sl] = (_rope(q[:, sl], cos, sin, quarter) * qscale).astype(BF16)
    lo = lax.broadcasted_iota(jnp.int32, (q.shape[0], LANES), 1) < A_HEAD_DIM
    k = _dot(h, wk_ref[...])
    for m in range(A_KV_WIDTH // LANES):
        sl = slice(m * LANES, (m + 1) * LANES)
        kp = _rope(k[:, sl], cos, sin, quarter)
        ks = pltpu.roll(kp, A_HEAD_DIM, 1)
        kd_ref[0, :, 2 * m * LANES:(2 * m + 1) * LANES] = jnp.where(lo, kp, ks).astype(BF16)
        kd_ref[0, :, (2 * m + 1) * LANES:(2 * m + 2) * LANES] = jnp.where(lo, ks, kp).astype(BF16)
    vt_ref[0] = _dot(h, wv_ref[...]).astype(BF16).T
    g_ref[0] = _silu(_dot(h, wg_ref[...])).astype(BF16)


def _proj_a(x, mod, norm_g, wq, wk, wv, wg, tabs, tm):
    bsz, s, d = x.shape
    assert s % tm == 0 and tm % LANES == 0
    kdw = A_KV_HEADS * LANES
    full = lambda a: pl.BlockSpec(a.shape, lambda b, t: (0,) * a.ndim)
    tab = pl.BlockSpec((tm, LANES), lambda b, t: (t, 0))
    tok = lambda w: pl.BlockSpec((1, tm, w), lambda b, t: (b, t, 0))
    return pl.pallas_call(
        functools.partial(_proj_a_kernel, qscale=A_HEAD_DIM ** -0.5 * LOG2E, rope=bool(tabs)),
        grid=(bsz, s // tm),
        in_specs=[tok(d), pl.BlockSpec((1, 3, d), lambda b, t: (b, 0, 0)), full(norm_g),
                  full(wq), full(wk), full(wv), full(wg)] + [tab] * len(tabs),
        out_specs=[tok(A_WIDTH), tok(kdw),
                   pl.BlockSpec((1, A_KV_WIDTH, tm), lambda b, t: (b, 0, t)), tok(A_WIDTH)],
        out_shape=[jax.ShapeDtypeStruct((bsz, s, A_WIDTH), BF16),
                   jax.ShapeDtypeStruct((bsz, s, kdw), BF16),
                   jax.ShapeDtypeStruct((bsz, A_KV_WIDTH, s), BF16),
                   jax.ShapeDtypeStruct((bsz, s, A_WIDTH), BF16)],
        compiler_params=pltpu.CompilerParams(
            dimension_semantics=("parallel", "parallel"), vmem_limit_bytes=VMEM_LIMIT),
        name="proj_a",
    )(x, mod, norm_g, wq, wk, wv, wg, *tabs)


def _attn_a_core(sink_ref, q, kd, vt, cap_loc, tq):
    group = A_HEADS // A_KV_HEADS
    nk = kd.shape[0]
    lane = lax.broadcasted_iota(jnp.int32, (1, 2 * tq), 1)
    klane = lax.broadcasted_iota(jnp.int32, (nk, LANES), 1)
    steps = [(g, par) for g in range(A_KV_HEADS) for par in range(2)]

    def scores(g, par):
        qrows = jnp.concatenate([q[:, (2 * g) * LANES:(2 * g + 1) * LANES],
                                 q[:, (2 * g + 1) * LANES:(2 * g + 2) * LANES]], axis=0)
        kg = kd[:, g * LANES:(g + 1) * LANES]
        keep = (klane < A_HEAD_DIM) if par == 0 else (klane >= A_HEAD_DIM)
        return _dot_nt(jnp.where(keep, kg, jnp.zeros_like(kg)), qrows)

    def finish(g, par, st):
        h0 = group * g + par
        sink = jnp.where(lane < tq, sink_ref[h0] * LOG2E, sink_ref[h0 + 2] * LOG2E)
        if cap_loc is not None:
            nl = cap_loc.shape[0]
            st = jnp.concatenate([jnp.minimum(st[:nl], cap_loc), st[nl:]], axis=0)
        m = jnp.maximum(sink, jnp.max(st, axis=0, keepdims=True))
        e = jnp.exp2(st - m)
        l = jnp.exp2(sink - m) + jnp.sum(e, axis=0, keepdims=True)
        ot = _dot(vt[g * A_HEAD_DIM:(g + 1) * A_HEAD_DIM], e.astype(BF16)) * (1.0 / l)
        return {h0: ot[:, :tq], h0 + 2: ot[:, tq:]}

    heads = {}
    cur = scores(*steps[0])
    for i, (g, par) in enumerate(steps):
        nxt = scores(*steps[i + 1]) if i + 1 < len(steps) else None
        heads.update(finish(g, par, cur))
        cur = nxt
    return jnp.concatenate([heads[h] for h in range(A_HEADS)], axis=0).T


def _attn_a_finish(o, g_ref, x_ref, mod_ref, wo_ref, out_ref):
    z = (o * g_ref[0].astype(F32)).astype(BF16)
    out_ref[0] = x_ref[0] + mod_ref[0][2:3] * _dot(z, wo_ref[...])


def _attn_a_kernel(sink_ref, q_ref, kp_ref, kc_ref, kn_ref, kx_ref, vp_ref, vc_ref, vn_ref, vx_ref,
                   g_ref, x_ref, mod_ref, wo_ref, out_ref, *, tq, seq):
    i = pl.program_id(1)
    kd = jnp.concatenate([kp_ref[0], kc_ref[0], kn_ref[0], kx_ref[0]], axis=0)
    vt = jnp.concatenate([vp_ref[0], vc_ref[0], vn_ref[0], vx_ref[0]], axis=1)
    nl = tq + 2 * WINDOW
    krow = lax.broadcasted_iota(jnp.int32, (nl, 2 * tq), 0)
    qcol = lax.broadcasted_iota(jnp.int32, (nl, 2 * tq), 1) & (tq - 1)
    kpos = i * tq - WINDOW + krow
    ok = (jnp.abs(qcol + WINDOW - krow) <= WINDOW) & (kpos >= 0) & (kpos < seq)
    o = _attn_a_core(sink_ref, q_ref[0], kd, vt, jnp.where(ok, jnp.inf, NEG_INF), tq)
    _attn_a_finish(o, g_ref, x_ref, mod_ref, wo_ref, out_ref)


def _attn_a_ctx_kernel(sink_ref, q_ref, kx_ref, vx_ref, g_ref, x_ref, mod_ref, wo_ref, out_ref, *, tq):
    o = _attn_a_core(sink_ref, q_ref[0], kx_ref[0], vx_ref[0], None, tq)
    _attn_a_finish(o, g_ref, x_ref, mod_ref, wo_ref, out_ref)


def _attn_a(sinks, q, kd, vt, kdc, vtc, gact, x, mod, wo, tq):
    bsz, s, d = x.shape
    c = kdc.shape[1]
    kdw = kd.shape[2]
    vw = vt.shape[1]
    assert tq & (tq - 1) == 0 and tq % WINDOW == 0 and s % tq == 0
    r = tq // WINDOW
    nblk = s // WINDOW
    tok = lambda w: pl.BlockSpec((1, tq, w), lambda b, i: (b, i, 0))
    prev_i = lambda i: jnp.maximum(i * r - 1, 0)
    next_i = lambda i: jnp.minimum((i + 1) * r, nblk - 1)
    return pl.pallas_call(
        functools.partial(_attn_a_kernel, tq=tq, seq=s),
        grid=(bsz, s // tq),
        in_specs=[pl.BlockSpec(memory_space=pltpu.SMEM), tok(A_WIDTH),
                  pl.BlockSpec((1, WINDOW, kdw), lambda b, i: (b, prev_i(i), 0)), tok(kdw),
                  pl.BlockSpec((1, WINDOW, kdw), lambda b, i: (b, next_i(i), 0)),
                  pl.BlockSpec((1, c, kdw), lambda b, i: (b, 0, 0)),
                  pl.BlockSpec((1, vw, WINDOW), lambda b, i: (b, 0, prev_i(i))),
                  pl.BlockSpec((1, vw, tq), lambda b, i: (b, 0, i)),
                  pl.BlockSpec((1, vw, WINDOW), lambda b, i: (b, 0, next_i(i))),
                  pl.BlockSpec((1, vw, c), lambda b, i: (b, 0, 0)),
                  tok(A_WIDTH), tok(d), pl.BlockSpec((1, 3, d), lambda b, i: (b, 0, 0)),
                  pl.BlockSpec(wo.shape, lambda b, i: (0, 0))],
        out_specs=tok(d),
        out_shape=jax.ShapeDtypeStruct((bsz, s, d), F32),
        compiler_params=pltpu.CompilerParams(
            dimension_semantics=("parallel", "parallel"), vmem_limit_bytes=VMEM_LIMIT),
        name="attn_a",
    )(sinks, q, kd, kd, kd, kdc, vt, vt, vt, vtc, gact, x, mod, wo)


def _attn_a_ctx(sinks, qc, kdc, vtc, gc, ctx, mod, wo):
    bsz, c, d = ctx.shape
    tok = lambda w: pl.BlockSpec((1, c, w), lambda b: (b, 0, 0))
    return pl.pallas_call(
        functools.partial(_attn_a_ctx_kernel, tq=c),
        grid=(bsz,),
        in_specs=[pl.BlockSpec(memory_space=pltpu.SMEM), tok(A_WIDTH), tok(kdc.shape[2]),
                  pl.BlockSpec((1, vtc.shape[1], c), lambda b: (b, 0, 0)),
                  tok(A_WIDTH), tok(d), pl.BlockSpec((1, 3, d), lambda b: (b, 0, 0)),
                  pl.BlockSpec(wo.shape, lambda b: (0, 0))],
        out_specs=tok(d),
        out_shape=jax.ShapeDtypeStruct((bsz, c, d), F32),
        compiler_params=pltpu.CompilerParams(
            dimension_semantics=("parallel",), vmem_limit_bytes=VMEM_LIMIT),
        name="attn_a_ctx",
    )(sinks, qc, kdc, vtc, gc, ctx, mod, wo)


def _kext(a, kvg, cos, sin):
    ckv = _rms(a[:, B_Q_RANK:B_Q_RANK + B_KV_RANK], kvg)
    kr = a[:, B_Q_RANK + B_KV_RANK:]
    kr = _rope(kr, cos, sin, B_ROPE // 4)
    return jnp.concatenate([ckv, kr], axis=1).astype(BF16), ckv.T.astype(BF16)


def _proj_b_kernel(x_ref, mod_ref, ng_ref, w1_ref, wg_ref, qg_ref, wqn_ref, wqr_ref, wabs_ref,
                   kvg_ref, c_ref, s_ref, ct_ref, st_ref, qt_ref, k_ref, vt_ref, g_ref,
                   *, qscale, tm):
    mod = mod_ref[0]
    h = (_rms(x_ref[0], ng_ref[...]) * (1.0 + mod[1:2]) + mod[0:1]).astype(BF16)
    a = _dot(h, w1_ref[...])
    k_ref[0], vt_ref[0] = _kext(a, kvg_ref[...], c_ref[...], s_ref[...])
    g_ref[0] = _silu(_dot(h, wg_ref[...])).astype(BF16)
    cq_t = _rms(a[:, :B_Q_RANK], qg_ref[...]).T.astype(BF16)
    qn_t = _dot(wqn_ref[...], cq_t).astype(BF16)
    qr_t = _dot(wqr_ref[...], cq_t)
    ct, st = ct_ref[...], st_ref[...]
    quarter = B_ROPE // 4
    heads_per_chunk = LANES // B_ROPE
    for hd in range(B_HEADS):
        xh = qr_t[hd * B_ROPE:(hd + 1) * B_ROPE]
        rot = jnp.concatenate([-xh[quarter:2 * quarter], xh[:quarter],
                               -xh[3 * quarter:], xh[2 * quarter:3 * quarter]], axis=0)
        rope = ((xh * ct + rot * st) * qscale).astype(BF16)
        j = hd % heads_per_chunk
        for t in range(tm // LANES):
            blk = rope[:, t * LANES:(t + 1) * LANES]
            pieces = [jnp.zeros((j * B_ROPE, LANES), BF16)] if j else []
            pieces.append(blk)
            if j + 1 < heads_per_chunk:
                pieces.append(jnp.zeros(((heads_per_chunk - 1 - j) * B_ROPE, LANES), BF16))
            qt_ref[0, t, LANES:, hd * LANES:(hd + 1) * LANES] = jnp.concatenate(pieces, axis=0)
    for m in range(B_HEADS // 2):
        qa_t = (_dot(wabs_ref[m], qn_t[m * LANES:(m + 1) * LANES]) * qscale).astype(BF16)
        for half in range(2):
            hd = 2 * m + half
            for t in range(tm // LANES):
                qt_ref[0, t, :LANES, hd * LANES:(hd + 1) * LANES] = (
                    qa_t[half * LANES:(half + 1) * LANES, t * LANES:(t + 1) * LANES])


def _proj_b(x, mod, norm_g, w1, wg, qg, wqn_t, wqr_t, wabs_t, kvg, tabs, tabs_t, tm):
    bsz, s, d = x.shape
    full = lambda a: pl.BlockSpec(a.shape, lambda b, t: (0,) * a.ndim)
    tab = pl.BlockSpec((tm, LANES), lambda b, t: (t, 0))
    tab_t = pl.BlockSpec((B_ROPE, tm), lambda b, t: (0, t))
    tok = lambda w: pl.BlockSpec((1, tm, w), lambda b, t: (b, t, 0))
    assert s % tm == 0 and tm % LANES == 0
    nt = tm // LANES
    return pl.pallas_call(
        functools.partial(_proj_b_kernel, qscale=(B_NOPE + B_ROPE) ** -0.5 * LOG2E, tm=tm),
        grid=(bsz, s // tm),
        in_specs=[tok(d), pl.BlockSpec((1, 3, d), lambda b, t: (b, 0, 0)), full(norm_g),
                  full(w1), full(wg), full(qg), full(wqn_t), full(wqr_t), full(wabs_t), full(kvg),
                  tab, tab, tab_t, tab_t],
        out_specs=[pl.BlockSpec((1, nt, B_FEAT, B_HEADS * LANES), lambda b, t: (b, t, 0, 0)),
                   tok(B_FEAT), pl.BlockSpec((1, B_KV_RANK, tm), lambda b, t: (b, 0, t)),
                   tok(B_WIDTH)],
        out_shape=[jax.ShapeDtypeStruct((bsz, s // LANES, B_FEAT, B_HEADS * LANES), BF16),
                   jax.ShapeDtypeStruct((bsz, s, B_FEAT), BF16),
                   jax.ShapeDtypeStruct((bsz, B_KV_RANK, s), BF16),
                   jax.ShapeDtypeStruct((bsz, s, B_WIDTH), BF16)],
        compiler_params=pltpu.CompilerParams(
            dimension_semantics=("parallel", "parallel"), vmem_limit_bytes=VMEM_LIMIT),
        name="proj_b",
    )(x, mod, norm_g, w1, wg, qg, wqn_t, wqr_t, wabs_t, kvg, *tabs, *tabs_t)


def _proj_b_ctx_kernel(x_ref, mod_ref, ng_ref, w1_ref, kvg_ref, k_ref, vt_ref):
    mod = mod_ref[0]
    h = (_rms(x_ref[0], ng_ref[...]) * (1.0 + mod[1:2]) + mod[0:1]).astype(BF16)
    k_ref[0], vt_ref[0] = _kext(_dot(h, w1_ref[...]), kvg_ref[...], None, None)


def _proj_b_ctx(ctx, mod, norm_g, w1, kvg):
    bsz, c, d = ctx.shape
    full = lambda a: pl.BlockSpec(a.shape, lambda b: (0,) * a.ndim)
    return pl.pallas_call(
        _proj_b_ctx_kernel,
        grid=(bsz,),
        in_specs=[pl.BlockSpec((1, c, d), lambda b: (b, 0, 0)),
                  pl.BlockSpec((1, 3, d), lambda b: (b, 0, 0)), full(norm_g), full(w1), full(kvg)],
        out_specs=[pl.BlockSpec((1, c, B_FEAT), lambda b: (b, 0, 0)),
                   pl.BlockSpec((1, B_KV_RANK, c), lambda b: (b, 0, 0))],
        out_shape=[jax.ShapeDtypeStruct((bsz, c, B_FEAT), BF16),
                   jax.ShapeDtypeStruct((bsz, B_KV_RANK, c), BF16)],
        compiler_params=pltpu.CompilerParams(dimension_semantics=("parallel",)),
        name="proj_b_ctx",
    )(ctx, mod, norm_g, w1, kvg)


def _flash_b_kernel(q_ref, k_ref, vt_ref, g_ref, x_ref, mod_ref, wvt_ref, wo_ref, fg_ref, out_ref,
                    m_ref, l_ref, acc_ref, s_ref, mx_ref, *, tiles, tk, nk, hb):
    tq = LANES
    rt = hb * tq
    m_ref[...] = jnp.full(m_ref.shape, NEG_INF, F32)
    l_ref[...] = jnp.zeros(l_ref.shape, F32)
    acc_ref[...] = jnp.zeros(acc_ref.shape, F32)

    hgroups = B_HEADS // hb
    nsb = tiles * hgroups
    nsteps = (nk // tk) * nsb
    unroll = max(u for u in range(2, MAX_UNROLL + 1, 2) if nsteps % u == 0)
    assert nsb & (nsb - 1) == 0 and hgroups & (hgroups - 1) == 0
    shift = nsb.bit_length() - 1
    hshift = hgroups.bit_length() - 1

    def split(k):
        return lax.shift_right_logical(k, shift), lax.bitwise_and(k, nsb - 1)

    def score_stage(k, slot):
        j, t = split(k)
        kc = k_ref[0, pl.ds(pl.multiple_of(j * tk, tk), tk), :]
        tile, hg = lax.shift_right_logical(t, hshift), lax.bitwise_and(t, hgroups - 1)
        st = _dot(kc, q_ref[0, tile, :, pl.ds(pl.multiple_of(hg * rt, rt), rt)])
        s_ref[slot] = st
        mx_ref[slot] = jnp.max(st.reshape(tk // SUBLANES, SUBLANES, rt), axis=0)

    def update_stage(k, slot):
        j, t = split(k)
        vt = vt_ref[0, :, pl.ds(pl.multiple_of(j * tk, tk), tk)]
        m_old = m_ref[t]
        m_new = jnp.maximum(m_old, jnp.max(mx_ref[slot], axis=0, keepdims=True))
        alpha = jnp.exp2(m_old - m_new)
        cut = ((tk // (2 * LANES) + 1) // 2) * (2 * LANES)
        l_new, pv = alpha * l_ref[t], None
        for lo, hi in ((0, cut), (cut, tk)):
            e = jnp.exp2(s_ref[slot, lo:hi] - m_new[0:1])
            l_new = l_new + jnp.sum(e.reshape((hi - lo) // SUBLANES, SUBLANES, rt), axis=0)
            part = _dot(vt[:, lo:hi], e.astype(BF16))
            pv = part if pv is None else pv + part
        l_ref[t] = l_new
        acc_ref[t] = alpha[0:1] * acc_ref[t] + pv
        m_ref[t] = m_new

    score_stage(0, 0)

    def body(i, carry):
        for u in range(unroll):
            k = unroll * i + u
            score_stage(jnp.minimum(k + 1, nsteps - 1), (u + 1) % 2)
            update_stage(k, u % 2)
        return carry

    lax.fori_loop(0, nsteps // unroll, body, 0)
    rows = []
    for tile in range(tiles):
        outs = []
        for hg in range(hgroups):
            t = tile * hgroups + hg
            linv = 1.0 / jnp.sum(l_ref[t], axis=0, keepdims=True)
            ot = (acc_ref[t] * linv).astype(BF16)
            for p in range(hb // 2):
                pair = jnp.concatenate([ot[:, (2 * p) * tq:(2 * p + 1) * tq],
                                        ot[:, (2 * p + 1) * tq:(2 * p + 2) * tq]], axis=0)
                outs.append(_dot(wvt_ref[hg * (hb // 2) + p], pair))
        rows.append(jnp.concatenate(outs, axis=0).T)
    o = jnp.concatenate(rows, axis=0)
    z = (o * g_ref[0].astype(F32)).astype(BF16)
    xo = x_ref[0] + mod_ref[0][2:3] * _dot(z, wo_ref[...])
    out_ref[0] = _rms(xo, fg_ref[...])


def _flash_b(qp, kall, vtall, gact, x, mod, wvt, wo, final_g, max_tk, hb, tiles):
    bsz, s, d = x.shape
    nk = kall.shape[1]
    tq = tiles * LANES
    tk = max(t for t in range(LANES, max_tk + 1, LANES) if nk % t == 0)
    assert hb % 2 == 0 and B_HEADS % hb == 0 and s % tq == 0 and tk % SUBLANES == 0
    nt = tiles * (B_HEADS // hb)
    tok = lambda w: pl.BlockSpec((1, tq, w), lambda b, i: (b, i, 0))
    full = lambda a: pl.BlockSpec(a.shape, lambda b, i: (0,) * a.ndim)
    return pl.pallas_call(
        functools.partial(_flash_b_kernel, tiles=tiles, tk=tk, nk=nk, hb=hb),
        grid=(bsz, s // tq),
        in_specs=[pl.BlockSpec((1, tiles, B_FEAT, B_HEADS * LANES), lambda b, i: (b, i, 0, 0)),
                  pl.BlockSpec((1, nk, B_FEAT), lambda b, i: (b, 0, 0)),
                  pl.BlockSpec((1, B_KV_RANK, nk), lambda b, i: (b, 0, 0)),
                  tok(B_WIDTH), tok(d), pl.BlockSpec((1, 3, d), lambda b, i: (b, 0, 0)),
                  full(wvt), full(wo), full(final_g)],
        out_specs=tok(d),
        out_shape=jax.ShapeDtypeStruct((bsz, s, d), F32),
        scratch_shapes=[pltpu.VMEM((nt, SUBLANES, hb * LANES), F32),
                        pltpu.VMEM((nt, SUBLANES, hb * LANES), F32),
                        pltpu.VMEM((nt, B_KV_RANK, hb * LANES), F32),
                        pltpu.VMEM((2, tk, hb * LANES), F32),
                        pltpu.VMEM((2, SUBLANES, hb * LANES), F32)],
        compiler_params=pltpu.CompilerParams(
            dimension_semantics=("parallel", "parallel"), vmem_limit_bytes=VMEM_LIMIT),
        name="flash_b",
    )(qp, kall, vtall, gact, x, mod, wvt, wo, final_g)


def _rope_cos_sin(n_tokens, rot_dim):
    n_rows = n_tokens // GRID_W
    row = jnp.broadcast_to(jnp.arange(n_rows)[:, None], (n_rows, GRID_W)).reshape(-1)
    col = jnp.broadcast_to(jnp.arange(GRID_W)[None, :], (n_rows, GRID_W)).reshape(-1)
    nf = rot_dim // 4
    inv = ROPE_BASE ** (-jnp.arange(nf, dtype=F32) / nf)
    ar = row.astype(F32)[:, None] * inv
    ac = col.astype(F32)[:, None] * inv
    ang = jnp.concatenate([ar, ar, ac, ac], axis=-1)
    return jnp.cos(ang), jnp.sin(ang)


def _rope_tables(n_tokens, rot_dim):
    reps = LANES // rot_dim
    cos, sin = _rope_cos_sin(n_tokens, rot_dim)
    return jnp.tile(cos, (1, reps)), jnp.tile(sin, (1, reps))


def _split_mod(modout, bsz):
    d = modout.shape[1] // 3
    m = modout.reshape(SUBLANES, 3, d)
    lat = m[:bsz]
    ctx = jnp.broadcast_to(m[bsz:bsz + 1], (bsz, 3, d))
    return lat, ctx


def kernel(x, c, ctx, c_ctx, norm_g_0, ada_w_0, ada_b_0, a_w_in_0, a_sinks_0, a_w_o_0, norm_g_1,
           ada_w_1, ada_b_1, b_w_in_1, b_q_norm_1, b_w_uq_1, b_kv_norm_1, b_w_ukv_1, b_w_o_1, final_g):
    bsz, s, d = x.shape
    n_ctx = ctx.shape[1]
    assert bsz + 1 <= SUBLANES and s % GRID_W == 0 and n_ctx % LANES == 0
    cc = jnp.concatenate([c, c_ctx[None], jnp.zeros((SUBLANES - bsz - 1, d), F32)], axis=0)
    row = lambda g: g.reshape(1, -1)

    mod_x, mod_c = _split_mod(_modulation(cc, ada_w_0, ada_b_0), bsz)
    w = a_w_in_0.astype(BF16)
    wq, wk = w[:, :A_WIDTH], w[:, A_WIDTH:A_WIDTH + A_KV_WIDTH]
    wv, wg = w[:, A_WIDTH + A_KV_WIDTH:A_WIDTH + 2 * A_KV_WIDTH], w[:, A_WIDTH + 2 * A_KV_WIDTH:]
    tabs_a = _rope_tables(s, A_HEAD_DIM)
    wo_a = a_w_o_0.astype(BF16)
    q, kd, vt, gact = _proj_a(x, mod_x, row(norm_g_0), wq, wk, wv, wg, tabs_a, tm=PROJ_A_TM)
    qc, kdc, vtc, gc = _proj_a(ctx, mod_c, row(norm_g_0), wq, wk, wv, wg, (), tm=n_ctx)
    x1 = _attn_a(a_sinks_0, q, kd, vt, kdc, vtc, gact, x, mod_x, wo_a, tq=ATTN_A_TQ)
    ctx1 = _attn_a_ctx(a_sinks_0, qc, kdc, vtc, gc, ctx, mod_c, wo_a)

    mod_x, mod_c = _split_mod(_modulation(cc, ada_w_1, ada_b_1), bsz)
    wb = b_w_in_1.astype(BF16)
    n_kv = B_Q_RANK + B_KV_RANK
    w1 = jnp.concatenate([wb[:, :n_kv]] + [wb[:, n_kv:n_kv + B_ROPE]] * (LANES // B_ROPE), axis=1)
    wgb = wb[:, n_kv + B_ROPE:]
    wuq = b_w_uq_1.astype(BF16).reshape(B_Q_RANK, B_HEADS, B_NOPE + B_ROPE)
    wqn_t = wuq[:, :, :B_NOPE].reshape(B_Q_RANK, B_HEADS * B_NOPE).T
    wqr_t = wuq[:, :, B_NOPE:].reshape(B_Q_RANK, B_HEADS * B_ROPE).T
    wukv = b_w_ukv_1.astype(BF16).reshape(B_KV_RANK, B_HEADS, B_NOPE + B_V)
    wkn_t = jnp.transpose(wukv[:, :, :B_NOPE], (1, 2, 0)).reshape(B_HEADS // 2, 2, B_NOPE, B_KV_RANK)
    zk = jnp.zeros_like(wkn_t[:, 0])
    wabs_t = jnp.transpose(
        jnp.concatenate([jnp.concatenate([wkn_t[:, 0], zk], axis=2),
                         jnp.concatenate([zk, wkn_t[:, 1]], axis=2)], axis=1), (0, 2, 1))
    wvh = jnp.transpose(wukv[:, :, B_NOPE:], (1, 0, 2)).reshape(B_HEADS // 2, 2, B_KV_RANK, B_V)
    zv = jnp.zeros_like(wvh[:, 0])
    wvup_t = jnp.transpose(
        jnp.concatenate([jnp.concatenate([wvh[:, 0], zv], axis=2),
                         jnp.concatenate([zv, wvh[:, 1]], axis=2)], axis=1), (0, 2, 1))
    tabs_b = _rope_tables(s, B_ROPE)
    cos_b, sin_b = _rope_cos_sin(s, B_ROPE)
    qp, kx, vtx, gb = _proj_b(x1, mod_x, row(norm_g_1), w1, wgb, row(b_q_norm_1), wqn_t, wqr_t,
                              wabs_t, row(b_kv_norm_1), tabs_b, (cos_b.T, sin_b.T), tm=PROJ_B_TM)
    kc, vtc = _proj_b_ctx(ctx1, mod_c, row(norm_g_1), w1, row(b_kv_norm_1))
    kall = jnp.concatenate([kc, kx], axis=1)
    vtall = jnp.concatenate([vtc, vtx], axis=2)
    return _flash_b(qp, kall, vtall, gb, x1, mod_x, wvup_t, b_w_o_1.astype(BF16), row(final_g),
                    max_tk=FLASH_MAX_TK, hb=FLASH_HEADS, tiles=FLASH_TILES)
```
